```python
import jax, jax.numpy as jnp
from jax import lax
import numpy as np

D_MODEL = 4096
BATCH = 4
SEQ = 2048
DEPTH = 2
DEC_BATCH = 32
DEC_SEQ = 64
PAST_LEN = 2048

CHUNK = 64
Q_BLOCK = 128
MLSTM_HEADS = 8
MLSTM_W = D_MODEL // 2
MLSTM_DV = MLSTM_W // MLSTM_HEADS
MLSTM_DK = MLSTM_DV // 2
MLSTM_QK_W = MLSTM_HEADS * MLSTM_DK
SB_HEADS = 8
SB_W = D_MODEL // 4
SB_DH = SB_W // SB_HEADS
POOL_WINDOWS = (2, 4, 8, 16)
POOL_GROUPS = len(POOL_WINDOWS)
POOL_W = D_MODEL // 4
POOL_G = POOL_W // POOL_GROUPS
POOL_BUF = max(POOL_WINDOWS) - 1
MEM_LEN = 256
X_HEADS = 4
X_W = D_MODEL // 4
X_DH = X_W // X_HEADS
D_FF = 4 * D_MODEL
N_BRANCH = 3
ALPHA = (2 * DEPTH) ** 0.25
BETA = (8 * DEPTH) ** -0.25
LN_EPS = 1e-5
HEAD_NORM_EPS = 1e-6

OFF_AQ = 0
OFF_AK = OFF_AQ + MLSTM_QK_W
OFF_AV = OFF_AK + MLSTM_QK_W
OFF_AO = OFF_AV + MLSTM_W
OFF_AI = OFF_AO + MLSTM_W
OFF_AF = OFF_AI + MLSTM_HEADS
OFF_BQ = OFF_AF + MLSTM_HEADS
OFF_BK = OFF_BQ + SB_W
OFF_BV = OFF_BK + SB_W
OFF_CU = OFF_BV + SB_W
OFF_G = OFF_CU + POOL_W
IN_W = OFF_G + N_BRANCH * D_MODEL

kernel_name = 'hybrid_mlstm_stickbreak_pool_stream_step'


def layer_norm(x, g, b, eps=LN_EPS):
    xf = x.astype(jnp.float32)
    mu = jnp.mean(xf, -1, keepdims=True)
    var = jnp.mean(jnp.square(xf - mu), -1, keepdims=True)
    return ((xf - mu) * lax.rsqrt(var + eps) * g + b).astype(x.dtype)


def mlstm_chunk(state, q, k, v, i_pre, log_f):
    c, n, m = state
    L = q.shape[2]
    b = jnp.cumsum(log_f, axis=-1)
    causal = jnp.tril(jnp.ones((L, L), dtype=bool))
    dmat = jnp.where(causal, b[..., :, None] - b[..., None, :] + i_pre[..., None, :], -jnp.inf)
    m_inter = b + m[..., None]
    m_t = jnp.maximum(m_inter, jnp.max(dmat, axis=-1))
    s = jnp.einsum('bhtd,bhsd->bhts', q, k) * jnp.exp(dmat - m_t[..., None])
    w_inter = jnp.exp(m_inter - m_t)
    num = jnp.einsum('bhts,bhsv->bhtv', s, v) + w_inter[..., None] * jnp.einsum('bhtd,bhdv->bhtv', q, c)
    den = jnp.sum(s, -1) + w_inter * jnp.einsum('bhtd,bhd->bht', q, n)
    h = num / jnp.maximum(jnp.abs(den), jnp.exp(-m_t))[..., None]
    m_new = m_t[..., -1]
    g = jnp.exp(b[..., -1:] - b + i_pre - m_new[..., None])
    decay = jnp.exp(b[..., -1] + m - m_new)
    c_new = decay[..., None, None] * c + jnp.einsum('bhs,bhsd,bhsv->bhdv', g, k, v)
    n_new = decay[..., None] * n + jnp.einsum('bhs,bhsd->bhd', g, k)
    return (c_new, n_new, m_new), h


def mlstm_forward(state, q, k, v, i_pre, log_f, chunk_len):
    bsz, nh, t = q.shape[:3]
    nc = t // chunk_len

    def split(a):
        return jnp.moveaxis(a.reshape(bsz, nh, nc, chunk_len, *a.shape[3:]), 2, 0)

    xs = (split(q), split(k), split(v), split(i_pre), split(log_f))
    state, hs = lax.scan(lambda carry, xc: mlstm_chunk(carry, *xc), state, xs)
    return state, jnp.moveaxis(hs, 0, 2).reshape(bsz, nh, t, -1)


def sb_block(q, k, v, q_pos, k_pos):
    z = jnp.einsum('bqhd,bkhd->bhqk', q, k).astype(jnp.float32) * (q.shape[-1] ** -0.5)
    mask = k_pos[None, :] < q_pos[:, None]
    log_keep = jnp.where(mask, jax.nn.log_sigmoid(-z), 0.0)
    after = lax.cumsum(log_keep, axis=3, reverse=True) - log_keep
    attn = jnp.where(mask, jnp.exp(jax.nn.log_sigmoid(z) + after), 0.0)
    return jnp.einsum('bhqk,bkhd->bqhd', attn.astype(v.dtype), v)


def stick_breaking(q, k, v, pos0):
    bsz, t, nh, dh = q.shape
    k_pos = jnp.arange(k.shape[1])
    nb = max(t // Q_BLOCK, 1)
    blk = t // nb
    qb = jnp.moveaxis(q.reshape(bsz, nb, blk, nh, dh), 1, 0)
    qpos = (pos0 + jnp.arange(t)).reshape(nb, blk)
    out = lax.map(lambda a: sb_block(a[0], k, v, a[1], k_pos), (qb, qpos))
    return jnp.moveaxis(out, 0, 1).reshape(bsz, t, nh, dh)


def pool_mix(u, buf, pos0, w_pool, scale):
    bsz, t, _ = u.shape
    ue = jnp.concatenate([buf.astype(jnp.float32), u.astype(jnp.float32)], axis=1)
    cs = jnp.concatenate([jnp.zeros((bsz, 1, POOL_W), jnp.float32), jnp.cumsum(ue, axis=1)], axis=1)
    pos = pos0 + jnp.arange(t)
    e = POOL_BUF + 1
    means = []
    for gi, w in enumerate(POOL_WINDOWS):
        sl = slice(gi * POOL_G, (gi + 1) * POOL_G)
        wsum = cs[:, e:e + t, sl] - cs[:, e - w:e - w + t, sl]
        cnt = jnp.minimum(pos + 1, w).astype(jnp.float32)
        means.append(wsum / cnt[None, :, None])
    d = (jnp.concatenate(means, -1) - ue[:, POOL_BUF:, :]).astype(u.dtype)
    y = jnp.einsum('btgc,gcd->btgd', d.reshape(bsz, t, POOL_GROUPS, POOL_G), w_pool).reshape(bsz, t, POOL_W) * scale
    return y, ue[:, -POOL_BUF:, :].astype(u.dtype)


def mem_kv(mem, wk, wv):
    bsz = mem.shape[0]
    return ((mem @ wk).reshape(bsz, -1, X_HEADS, X_DH), (mem @ wv).reshape(bsz, -1, X_HEADS, X_DH))


def cross_attend(x, mk, mv, wq, wo):
    bsz, t, _ = x.shape
    q = (x @ wq).reshape(bsz, t, X_HEADS, X_DH)
    s = jnp.einsum('bthd,bmhd->bhtm', q, mk).astype(jnp.float32) * (X_DH ** -0.5)
    p = jax.nn.softmax(s, axis=-1).astype(x.dtype)
    o = jnp.einsum('bhtm,bmhd->bthd', p, mv).reshape(bsz, t, X_W)
    return o @ wo


def trunk_layer(x, W, l, mstate, sb_past, pool_buf, pos0, mk, mv, chunk_len):
    bsz, t, _ = x.shape
    f32 = jnp.float32
    proj = x @ W['w_in'][l] + W['b_in'][l]

    def heads(a, nh, dh):
        return a.reshape(bsz, t, nh, dh).transpose(0, 2, 1, 3).astype(f32)

    aq = heads(proj[..., OFF_AQ:OFF_AK], MLSTM_HEADS, MLSTM_DK)
    ak = heads(proj[..., OFF_AK:OFF_AV], MLSTM_HEADS, MLSTM_DK) * (MLSTM_DK ** -0.5)
    av = heads(proj[..., OFF_AV:OFF_AO], MLSTM_HEADS, MLSTM_DV)
    a_o = jax.nn.sigmoid(proj[..., OFF_AO:OFF_AI].astype(f32))
    a_i = proj[..., OFF_AI:OFF_AF].astype(f32).transpose(0, 2, 1)
    a_f = jax.nn.log_sigmoid(proj[..., OFF_AF:OFF_BQ].astype(f32)).transpose(0, 2, 1)
    mstate_new, ha = mlstm_forward(mstate, aq, ak, av, a_i, a_f, chunk_len)
    ha = ha.transpose(0, 2, 1, 3)
    mu = jnp.mean(ha, -1, keepdims=True)
    var = jnp.mean(jnp.square(ha - mu), -1, keepdims=True)
    ha = ((ha - mu) * lax.rsqrt(var + HEAD_NORM_EPS) * W['mlstm_norm_g'][l].reshape(MLSTM_HEADS, MLSTM_DV)).reshape(bsz, t, MLSTM_W)
    ha = (a_o * ha).astype(x.dtype)

    bq = proj[..., OFF_BQ:OFF_BK].reshape(bsz, t, SB_HEADS, SB_DH)
    bk = proj[..., OFF_BK:OFF_BV].reshape(bsz, t, SB_HEADS, SB_DH)
    bv = proj[..., OFF_BV:OFF_CU].reshape(bsz, t, SB_HEADS, SB_DH)
    if sb_past is None:
        keys, vals = bk, bv
    else:
        keys = jnp.concatenate([sb_past[0].astype(bk.dtype), bk], axis=1)
        vals = jnp.concatenate([sb_past[1].astype(bv.dtype), bv], axis=1)
    hb = stick_breaking(bq, keys, vals, pos0).reshape(bsz, t, SB_W)

    hc, pool_new = pool_mix(proj[..., OFF_CU:OFF_G], pool_buf, pos0, W['w_pool'][l], W['pool_scale'][l])

    gates = jax.nn.sigmoid(proj[..., OFF_G:].astype(f32)).reshape(bsz, t, N_BRANCH, D_MODEL)
    merged = (gates[..., 0, :] * (ha @ W['w_pa'][l])
              + gates[..., 1, :] * (hb @ W['w_pb'][l])
              + gates[..., 2, :] * (hc.astype(x.dtype) @ W['w_pc'][l]))
    x = layer_norm(ALPHA * x + merged.astype(x.dtype) @ W['w_out'][l], W['ln1_g'][l], W['ln1_b'][l])

    x = layer_norm(ALPHA * x + cross_attend(x, mk, mv, W['wq_x'][l], W['wo_x'][l]), W['ln2_g'][l], W['ln2_b'][l])

    hid = jnp.square(jax.nn.relu(x @ W['w_up'][l]))
    x = layer_norm(ALPHA * x + hid @ W['w_down'][l], W['ln3_g'][l], W['ln3_b'][l])
    return x, (bk, bv, mstate_new[0], mstate_new[1], mstate_new[2], pool_new)


def setup_inputs(seed: int = 0) -> dict:
    key = jax.random.key(seed)
    ks = iter(jax.random.split(key, 48))
    f32 = jnp.float32

    def nrm(shape, scale=1.0):
        return jax.random.normal(next(ks), shape, f32) * scale

    def gain(shape):
        return 1.0 + nrm(shape, 0.01)

    b_in = nrm((DEPTH, IN_W), 0.01).at[:, OFF_AF:OFF_BQ].add(jnp.linspace(3.0, 6.0, MLSTM_HEADS, dtype=f32))
    return {
        'x_prompt': nrm((BATCH, SEQ, D_MODEL)),
        'x_sample': nrm((DEC_BATCH, DEC_SEQ, D_MODEL)),
        'cache_sb_k': nrm((DEPTH, DEC_BATCH, PAST_LEN, SB_HEADS, SB_DH)),
        'cache_sb_v': nrm((DEPTH, DEC_BATCH, PAST_LEN, SB_HEADS, SB_DH)),
        'state_mlstm_c': nrm((DEPTH, DEC_BATCH, MLSTM_HEADS, MLSTM_DK, MLSTM_DV), 0.1),
        'state_mlstm_n': nrm((DEPTH, DEC_BATCH, MLSTM_HEADS, MLSTM_DK), 0.1),
        'state_mlstm_m': nrm((DEPTH, DEC_BATCH, MLSTM_HEADS), 0.5),
        'state_pool': nrm((DEPTH, DEC_BATCH, POOL_BUF, POOL_W)),
        'cache_mem_k': nrm((DEPTH, DEC_BATCH, MEM_LEN, X_HEADS, X_DH)),
        'cache_mem_v': nrm((DEPTH, DEC_BATCH, MEM_LEN, X_HEADS, X_DH)),
        'mem_prompt': nrm((BATCH, MEM_LEN, D_MODEL)),
        'ln_in_g': gain((D_MODEL,)),
        'ln_in_b': nrm((D_MODEL,), 0.01),
        'w_in': nrm((DEPTH, D_MODEL, IN_W), D_MODEL ** -0.5),
        'b_in': b_in,
        'mlstm_norm_g': gain((DEPTH, MLSTM_W)),
        'w_pool': nrm((DEPTH, POOL_GROUPS, POOL_G, POOL_G), POOL_G ** -0.5),
        'pool_scale': gain((DEPTH, POOL_W)),
        'w_pa': nrm((DEPTH, MLSTM_W, D_MODEL), MLSTM_W ** -0.5),
        'w_pb': nrm((DEPTH, SB_W, D_MODEL), SB_W ** -0.5),
        'w_pc': nrm((DEPTH, POOL_W, D_MODEL), POOL_W ** -0.5),
        'w_out': nrm((DEPTH, D_MODEL, D_MODEL), BETA * D_MODEL ** -0.5),
        'ln1_g': gain((DEPTH, D_MODEL)),
        'ln1_b': nrm((DEPTH, D_MODEL), 0.01),
        'wq_x': nrm((DEPTH, D_MODEL, X_W), D_MODEL ** -0.5),
        'wk_x': nrm((DEPTH, D_MODEL, X_W), D_MODEL ** -0.5),
        'wv_x': nrm((DEPTH, D_MODEL, X_W), D_MODEL ** -0.5),
        'wo_x': nrm((DEPTH, X_W, D_MODEL), BETA * X_W ** -0.5),
        'ln2_g': gain((DEPTH, D_MODEL)),
        'ln2_b': nrm((DEPTH, D_MODEL), 0.01),
        'w_up': nrm((DEPTH, D_MODEL, D_FF), D_MODEL ** -0.5),
        'w_down': nrm((DEPTH, D_FF, D_MODEL), BETA * D_FF ** -0.5),
        'ln3_g': gain((DEPTH, D_MODEL)),
        'ln3_b': nrm((DEPTH, D_MODEL), 0.01),
    }


def reference(x_prompt, x_sample, cache_sb_k, cache_sb_v, state_mlstm_c, state_mlstm_n, state_mlstm_m,
              state_pool, cache_mem_k, cache_mem_v, mem_prompt, ln_in_g, ln_in_b, w_in, b_in, mlstm_norm_g,
              w_pool, pool_scale, w_pa, w_pb, w_pc, w_out, ln1_g, ln1_b, wq_x, wk_x, wv_x, wo_x, ln2_g, ln2_b,
              w_up, w_down, ln3_g, ln3_b):
    W = dict(w_in=w_in, b_in=b_in, mlstm_norm_g=mlstm_norm_g, w_pool=w_pool, pool_scale=pool_scale,
             w_pa=w_pa, w_pb=w_pb, w_pc=w_pc, w_out=w_out, ln1_g=ln1_g, ln1_b=ln1_b, wq_x=wq_x, wo_x=wo_x,
             ln2_g=ln2_g, ln2_b=ln2_b, w_up=w_up, w_down=w_down, ln3_g=ln3_g, ln3_b=ln3_b)
    f32 = jnp.float32

    bp = x_prompt.shape[0]
    h = layer_norm(x_prompt, ln_in_g, ln_in_b)
    pro = []
    for l in range(DEPTH):
        mk, mv = mem_kv(mem_prompt, wk_x[l], wv_x[l])
        init = (jnp.zeros((bp, MLSTM_HEADS, MLSTM_DK, MLSTM_DV), f32),
                jnp.zeros((bp, MLSTM_HEADS, MLSTM_DK), f32),
                jnp.zeros((bp, MLSTM_HEADS), f32))
        h, st = trunk_layer(h, W, l, init, None, jnp.zeros((bp, POOL_BUF, POOL_W), h.dtype), 0, mk, mv, CHUNK)
        pro.append(st + (mk, mv))
    y_prompt = h

    past = cache_sb_k.shape[2]
    h = layer_norm(x_sample, ln_in_g, ln_in_b)
    smp = []
    for l in range(DEPTH):
        mstate = (state_mlstm_c[l].astype(f32), state_mlstm_n[l].astype(f32), state_mlstm_m[l].astype(f32))
        h, st = trunk_layer(h, W, l, mstate, (cache_sb_k[l], cache_sb_v[l]), state_pool[l], past,
                            cache_mem_k[l], cache_mem_v[l], h.shape[1])
        smp.append(st)
    y_sample = h

    def stk(lst, i):
        return jnp.stack([s[i] for s in lst])

    return (y_prompt, y_sample,
            stk(pro, 0), stk(pro, 1), stk(pro, 2), stk(pro, 3), stk(pro, 4), stk(pro, 5), stk(pro, 6), stk(pro, 7),
            stk(smp, 0), stk(smp, 1), stk(smp, 2), stk(smp, 3), stk(smp, 4), stk(smp, 5))
```

```python
import functools

import jax
import jax.numpy as jnp
from jax import lax
from jax.experimental import pallas as pl
from jax.experimental.pallas import tpu as pltpu

F32 = jnp.float32
BF16 = jnp.bfloat16

D_MODEL = 4096
DEPTH = 2
CHUNK = 64
MLSTM_HEADS = 8
MLSTM_W = D_MODEL // 2
MLSTM_DV = MLSTM_W // MLSTM_HEADS
MLSTM_DK = MLSTM_DV // 2
MLSTM_QK_W = MLSTM_HEADS * MLSTM_DK
SB_HEADS = 8
SB_W = D_MODEL // 4
SB_DH = SB_W // SB_HEADS
POOL_WINDOWS = (2, 4, 8, 16)
POOL_W = D_MODEL // 4
POOL_G = POOL_W // len(POOL_WINDOWS)
POOL_BUF = max(POOL_WINDOWS) - 1
POOL_HALO = POOL_BUF + 1
X_HEADS = 4
X_W = D_MODEL // 4
X_DH = X_W // X_HEADS
D_FF = 4 * D_MODEL
N_BRANCH = 3
ALPHA = (2 * DEPTH) ** 0.25
LN_EPS = 1e-5
HEAD_NORM_EPS = 1e-6

OFF_AO = 2 * MLSTM_QK_W + MLSTM_W
OFF_AI = OFF_AO + MLSTM_W
OFF_BQ = OFF_AI + 2 * MLSTM_HEADS
OFF_G = OFF_BQ + 3 * SB_W + POOL_W

V7X_VMEM_LIMIT_BYTES = 56 * 1024 * 1024
LANES = 128
SB_TQ = 256
GATE_PAD = LANES

_NT = (((1,), (1,)), ((), ()))
_TN = (((0,), (0,)), ((), ()))


def _params(*sem):
    return pltpu.CompilerParams(dimension_semantics=sem,
                                vmem_limit_bytes=V7X_VMEM_LIMIT_BYTES)


def _dot(a, b):
    return jnp.dot(a, b, preferred_element_type=F32)


def _split3(x):
    h1 = x.astype(BF16)
    r1 = x - h1.astype(F32)
    h2 = r1.astype(BF16)
    h3 = (r1 - h2.astype(F32)).astype(BF16)
    return h1, h2, h3


def _mm_kernel(*refs, act, has_bias, has_scale, nk):
    it = iter(refs)
    x_ref, w_ref = next(it), next(it)
    b_ref = next(it) if has_bias else None
    s_ref = next(it) if has_scale else None
    o_ref = next(it)
    acc_ref = next(it) if nk > 1 else None

    def epilogue(y):
        if has_bias:
            y = y + b_ref[...]
        if has_scale:
            y = y * s_ref[...]
        if act == "sigmoid":
            y = jax.nn.sigmoid(y)
        elif act == "relu2":
            y = jnp.square(jnp.maximum(y, 0.0))
        o_ref[...] = y.astype(o_ref.dtype)

    part = _dot(x_ref[...], w_ref[...])
    if nk == 1:
        epilogue(part)
    else:
        k = pl.program_id(2)

        @pl.when(k == 0)
        def _():
            acc_ref[...] = part

        @pl.when(k > 0)
        def _():
            acc_ref[...] += part

        @pl.when(k == nk - 1)
        def _():
            epilogue(acc_ref[...])


def _matmul(x, w, bias=None, scale=None, *, act="none", out_dtype=F32,
            bm=1024, bn=1024, bk=None, name="matmul"):
    m, kdim = x.shape
    n = w.shape[1]
    bm, bn = min(bm, m), min(bn, n)
    bk = kdim if bk is None else bk
    assert m % bm == 0 and n % bn == 0 and kdim % bk == 0
    nk = kdim // bk
    in_specs = [pl.BlockSpec((bm, bk), lambda i, j, k: (i, k)),
                pl.BlockSpec((bk, bn), lambda i, j, k: (k, j))]
    args = [x, w]
    for v in (bias, scale):
        if v is not None:
            in_specs.append(pl.BlockSpec((1, bn), lambda i, j, k: (0, j)))
            args.append(v.reshape(1, n).astype(F32))
    kern = functools.partial(_mm_kernel, act=act, has_bias=bias is not None,
                             has_scale=scale is not None, nk=nk)
    return pl.pallas_call(
        kern,
        out_shape=jax.ShapeDtypeStruct((m, n), out_dtype),
        grid=(m // bm, n // bn, nk),
        in_specs=in_specs,
        out_specs=pl.BlockSpec((bm, bn), lambda i, j, k: (i, j)),
        scratch_shapes=[pltpu.VMEM((bm, bn), F32)] if nk > 1 else [],
        compiler_params=_params("parallel", "parallel", "arbitrary"),
        name=name,
    )(*args)


def _ln_kernel(*refs, has_res):
    if has_res:
        x_ref, r_ref, g_ref, b_ref, o_ref, ob_ref = refs
        x = ALPHA * x_ref[...] + r_ref[...]
    else:
        x_ref, g_ref, b_ref, o_ref, ob_ref = refs
        x = x_ref[...]
    mu = jnp.mean(x, axis=-1, keepdims=True)
    xc = x - mu
    var = jnp.mean(xc * xc, axis=-1, keepdims=True)
    y = xc * lax.rsqrt(var + LN_EPS) * g_ref[...] + b_ref[...]
    o_ref[...] = y
    ob_ref[...] = y.astype(BF16)


def _layer_norm(x, g, b, res=None, *, rows=256):
    m, d = x.shape
    row_spec = pl.BlockSpec((rows, d), lambda i: (i, 0))
    vec_spec = pl.BlockSpec((1, d), lambda i: (0, 0))
    args = [x] + ([res] if res is not None else []) + [g.reshape(1, d), b.reshape(1, d)]
    in_specs = [row_spec] * (2 if res is not None else 1) + [vec_spec, vec_spec]
    return pl.pallas_call(
        functools.partial(_ln_kernel, has_res=res is not None),
        out_shape=(jax.ShapeDtypeStruct((m, d), F32), jax.ShapeDtypeStruct((m, d), BF16)),
        grid=(m // rows,),
        in_specs=in_specs,
        out_specs=(row_spec, row_spec),
        compiler_params=_params("parallel"),
        name="layer_norm",
    )(*args)


def _mlstm_kernel(q_ref, k_ref, v_ref, ao_ref, gif_ref, c0_ref, n0_ref, m0_ref, g_ref,
                  ha_ref, c_ref, n_ref, m_ref):
    L = CHUNK

    @pl.when(pl.program_id(1) == 0)
    def _():
        c_ref[...] = c0_ref[...]
        n_ref[...] = n0_ref[...]
        m_ref[...] = m0_ref[...]

    gif = gif_ref[...]
    logf = -(jnp.maximum(-gif, 0.0) + jnp.log1p(jnp.exp(-jnp.abs(gif))))
    row = lax.broadcasted_iota(jnp.int32, (L, L), 0)
    col = lax.broadcasted_iota(jnp.int32, (L, L), 1)
    causal = col <= row
    tril = jnp.where(causal, 1.0, 0.0).astype(BF16)
    b_all = sum(_dot(tril, t) for t in _split3(logf))
    sel_r = lax.broadcasted_iota(jnp.int32, (2 * MLSTM_HEADS, GATE_PAD), 0)
    sel_c = lax.broadcasted_iota(jnp.int32, (2 * MLSTM_HEADS, GATE_PAD), 1)
    sel = jnp.where(sel_r == sel_c, 1.0, 0.0).astype(BF16)

    def rows_of(x):
        return sum(lax.dot_general(sel, t, _NT, preferred_element_type=F32) for t in _split3(x))

    i_rows = rows_of(gif)
    b_rows = rows_of(b_all)

    for h in range(MLSTM_HEADS):
        qs = slice(h * MLSTM_DK, (h + 1) * MLSTM_DK)
        vs = slice(h * MLSTM_DV, (h + 1) * MLSTM_DV)
        q = q_ref[:, qs]
        k = k_ref[:, qs]
        v = v_ref[:, vs]
        i_col = gif[:, h:h + 1]
        b_col = b_all[:, MLSTM_HEADS + h:MLSTM_HEADS + h + 1]
        i_row = i_rows[h:h + 1, :]
        b_row = b_rows[MLSTM_HEADS + h:MLSTM_HEADS + h + 1, :]
        c_old = c_ref[0, h]
        n_old = n_ref[0, h:h + 1, :]
        m_old = m_ref[0, h:h + 1, 0:1]

        dmat = jnp.where(causal, b_col - b_row + i_row, -jnp.inf)
        m_inter = b_col + m_old
        m_t = jnp.maximum(m_inter, jnp.max(dmat, axis=-1, keepdims=True))
        s = lax.dot_general(q, k, _NT, preferred_element_type=F32) * jnp.exp(dmat - m_t)
        w_inter = jnp.exp(m_inter - m_t)
        kf = k.astype(F32)
        num = _dot(s.astype(BF16), v) + w_inter * _dot(q, c_old.astype(BF16))
        qn = jnp.sum(q.astype(F32) * n_old, axis=-1, keepdims=True)
        den = jnp.sum(s, axis=-1, keepdims=True) + w_inter * qn
        hh = num / jnp.maximum(jnp.abs(den), jnp.exp(-m_t))

        m_new = m_t[L - 1:L, :]
        b_last = b_col[L - 1:L, :]
        g_col = jnp.exp(b_last - b_col + i_col - m_new)
        decay = jnp.exp(b_last + m_old - m_new)
        kg = kf * g_col
        c_ref[0, h] = decay * c_old + lax.dot_general(kg.astype(BF16), v, _TN,
                                                      preferred_element_type=F32)
        n_ref[0, h:h + 1, :] = decay * n_old + jnp.sum(kg, axis=0, keepdims=True)
        m_ref[0, h:h + 1, :] = jnp.broadcast_to(m_new, (1, LANES))

        mu = jnp.mean(hh, axis=-1, keepdims=True)
        hc = hh - mu
        var = jnp.mean(hc * hc, axis=-1, keepdims=True)
        hn = hc * lax.rsqrt(var + HEAD_NORM_EPS) * g_ref[:, vs]
        ha_ref[:, vs] = (ao_ref[:, vs] * hn).astype(BF16)


def _mlstm(qkv, ao, gif, c0, n0, m0, norm_g, *, bsz, t, row_off):
    nc = t // CHUNK
    off = row_off // CHUNK
    rows = lambda b, c: off + b * nc + c
    state4 = pl.BlockSpec((1, MLSTM_HEADS, MLSTM_DK, MLSTM_DV), lambda b, c: (b, 0, 0, 0))
    state3 = pl.BlockSpec((1, MLSTM_HEADS, LANES), lambda b, c: (b, 0, 0))
    return pl.pallas_call(
        _mlstm_kernel,
        out_shape=(jax.ShapeDtypeStruct((bsz * t, MLSTM_W), BF16),
                   jax.ShapeDtypeStruct(c0.shape, F32),
                   jax.ShapeDtypeStruct(n0.shape, F32),
                   jax.ShapeDtypeStruct(m0.shape, F32)),
        grid=(bsz, nc),
        in_specs=[
            pl.BlockSpec((CHUNK, MLSTM_QK_W), lambda b, c: (rows(b, c), 0)),
            pl.BlockSpec((CHUNK, MLSTM_QK_W), lambda b, c: (rows(b, c), 1)),
            pl.BlockSpec((CHUNK, MLSTM_W), lambda b, c: (rows(b, c), 1)),
            pl.BlockSpec((CHUNK, MLSTM_W), lambda b, c: (rows(b, c), 0)),
            pl.BlockSpec((CHUNK, GATE_PAD), lambda b, c: (rows(b, c), 0)),
            state4, state3, state3,
            pl.BlockSpec((1, MLSTM_W), lambda b, c: (0, 0)),
        ],
        out_specs=(pl.BlockSpec((CHUNK, MLSTM_W), lambda b, c: (b * nc + c, 0)),
                   state4, state3, state3),
        compiler_params=_params("parallel", "arbitrary"),
        name="mlstm",
    )(qkv, qkv, qkv, ao, gif, c0, n0, m0, norm_g.reshape(1, MLSTM_W))


SB_SCALE = SB_DH ** -0.5


def _sb_block(q, kb, vb, carry, acc, *, diag):
    tq, tk = q.shape[0], kb.shape[0]
    z = lax.dot_general(q, kb, _NT, preferred_element_type=F32) * SB_SCALE
    log_keep = -(jnp.maximum(z, 0.0) + jnp.log1p(jnp.exp(-jnp.abs(z))))
    if diag:
        mask = (lax.broadcasted_iota(jnp.int32, (tq, tk), 1)
                < lax.broadcasted_iota(jnp.int32, (tq, tk), 0))
        log_keep = jnp.where(mask, log_keep, 0.0)
    lower = jnp.where(lax.broadcasted_iota(jnp.int32, (tk, tk), 0)
                      >= lax.broadcasted_iota(jnp.int32, (tk, tk), 1), 1.0, 0.0).astype(BF16)
    rc = sum(_dot(t, lower) for t in _split3(log_keep))
    attn = jnp.exp(z + rc + carry)
    if diag:
        attn = jnp.where(mask, attn, 0.0)
    acc = acc + _dot(attn.astype(BF16), vb)
    return carry + rc[:, 0:1], acc


def _sb_kernel(*refs, tq, n_past):
    if n_past:
        q_ref, k_ref, v_ref, kp_ref, vp_ref, o_ref = refs
    else:
        q_ref, k_ref, v_ref, o_ref = refs
    qi = pl.program_id(2)
    q = q_ref[...].astype(BF16)
    carry = jnp.zeros((tq, 1), F32)
    acc = jnp.zeros((tq, SB_DH), F32)

    start = pl.multiple_of(qi * tq, tq)
    carry, acc = _sb_block(q, k_ref[pl.ds(start, tq), :].astype(BF16),
                           v_ref[pl.ds(start, tq), :].astype(BF16), carry, acc, diag=True)

    def new_body(j, st):
        s0 = pl.multiple_of((qi - 1 - j) * tq, tq)
        return _sb_block(q, k_ref[pl.ds(s0, tq), :].astype(BF16),
                         v_ref[pl.ds(s0, tq), :].astype(BF16), *st, diag=False)

    carry, acc = lax.fori_loop(0, qi, new_body, (carry, acc))

    if n_past:
        def past_body(j, st):
            s0 = pl.multiple_of((n_past - 1 - j) * SB_TQ, SB_TQ)
            return _sb_block(q, kp_ref[0, pl.ds(s0, SB_TQ), :].astype(BF16),
                             vp_ref[0, pl.ds(s0, SB_TQ), :].astype(BF16), *st, diag=False)

        carry, acc = lax.fori_loop(0, n_past, past_body, (carry, acc))

    o_ref[...] = acc.astype(BF16)


def _stick_breaking(bc, past_k, past_v, *, bsz, t, row_off):
    tq = min(SB_TQ, t)
    nq = t // tq
    off_q = row_off // tq
    off_t = row_off // t
    n_past = 0 if past_k is None else past_k.shape[1] // SB_TQ
    in_specs = [
        pl.BlockSpec((tq, SB_DH), lambda b, h, i: (off_q + b * nq + i, h)),
        pl.BlockSpec((t, SB_DH), lambda b, h, i: (off_t + b, SB_HEADS + h)),
        pl.BlockSpec((t, SB_DH), lambda b, h, i: (off_t + b, 2 * SB_HEADS + h)),
    ]
    args = [bc, bc, bc]
    if n_past:
        past_spec = pl.BlockSpec((1, past_k.shape[1], SB_DH), lambda b, h, i: (b, 0, h))
        in_specs += [past_spec, past_spec]
        args += [past_k, past_v]
    return pl.pallas_call(
        functools.partial(_sb_kernel, tq=tq, n_past=n_past),
        out_shape=jax.ShapeDtypeStruct((bsz * t, SB_W), BF16),
        grid=(bsz, SB_HEADS, nq),
        in_specs=in_specs,
        out_specs=pl.BlockSpec((tq, SB_DH), lambda b, h, i: (b * nq + i, h)),
        compiler_params=_params("parallel", "parallel", "arbitrary"),
        name="stick_breaking",
    )(*args)


def _pool_kernel(u_ref, buf_ref, w_ref, s_ref, o_ref, ext_ref, *, tt, pos0):
    i = pl.program_id(1)

    @pl.when(i == 0)
    def _():
        ext_ref[0:POOL_HALO, :] = buf_ref[0]

    @pl.when(i > 0)
    def _():
        ext_ref[0:POOL_HALO, :] = ext_ref[tt:tt + POOL_HALO, :]

    ext_ref[POOL_HALO:POOL_HALO + tt, :] = u_ref[...]
    pos = pos0 + i * tt + lax.broadcasted_iota(jnp.int32, (tt, 1), 0)
    for gi, w in enumerate(POOL_WINDOWS):
        cs = slice(gi * POOL_G, (gi + 1) * POOL_G)
        cur = ext_ref[POOL_HALO:POOL_HALO + tt, cs]
        wsum = cur
        for j in range(1, w):
            wsum = wsum + ext_ref[POOL_HALO - j:POOL_HALO - j + tt, cs]
        cnt = jnp.minimum(pos + 1, w).astype(F32)
        d = (wsum / cnt - cur).astype(BF16)
        y = _dot(d, w_ref[gi]) * s_ref[:, cs]
        o_ref[:, cs] = y.astype(BF16)


def _pool(bc, buf, w_pool, scale, *, bsz, t, row_off, pos0):
    tt = min(256, t)
    nt = t // tt
    off = row_off // tt
    ucol = 3 * SB_W // POOL_W
    return pl.pallas_call(
        functools.partial(_pool_kernel, tt=tt, pos0=pos0),
        out_shape=jax.ShapeDtypeStruct((bsz * t, POOL_W), BF16),
        grid=(bsz, nt),
        in_specs=[
            pl.BlockSpec((tt, POOL_W), lambda b, i: (off + b * nt + i, ucol)),
            pl.BlockSpec((1, POOL_HALO, POOL_W), lambda b, i: (b, 0, 0)),
            pl.BlockSpec((len(POOL_WINDOWS), POOL_G, POOL_G), lambda b, i: (0, 0, 0)),
            pl.BlockSpec((1, POOL_W), lambda b, i: (0, 0)),
        ],
        out_specs=pl.BlockSpec((tt, POOL_W), lambda b, i: (b * nt + i, 0)),
        scratch_shapes=[pltpu.VMEM((POOL_HALO + tt, POOL_W), F32)],
        compiler_params=_params("parallel", "arbitrary"),
        name="pool_mix",
    )(bc, buf, w_pool, scale.reshape(1, POOL_W))


X_SCALE = X_DH ** -0.5


def _cross_kernel(q_ref, k_ref, v_ref, o_ref):
    for h in range(X_HEADS):
        cs = slice(h * X_DH, (h + 1) * X_DH)
        k = k_ref[0, :, cs].astype(BF16)
        v = v_ref[0, :, cs].astype(BF16)
        s = lax.dot_general(q_ref[:, cs], k, _NT, preferred_element_type=F32) * X_SCALE
        e = jnp.exp(s - jnp.max(s, axis=-1, keepdims=True))
        p = e / jnp.sum(e, axis=-1, keepdims=True)
        o_ref[:, cs] = _dot(p.astype(BF16), v).astype(BF16)


def _cross_attend(qx, mk, mv, *, bsz, t, row_off):
    tq = min(256, t)
    nq = t // tq
    off = row_off // tq
    mem_spec = pl.BlockSpec((1, mk.shape[1], X_W), lambda b, i: (b, 0, 0))
    return pl.pallas_call(
        _cross_kernel,
        out_shape=jax.ShapeDtypeStruct((bsz * t, X_W), BF16),
        grid=(bsz, nq),
        in_specs=[pl.BlockSpec((tq, X_W), lambda b, i: (off + b * nq + i, 0)),
                  mem_spec, mem_spec],
        out_specs=pl.BlockSpec((tq, X_W), lambda b, i: (b * nq + i, 0)),
        compiler_params=_params("parallel", "arbitrary"),
        name="cross_attend",
    )(qx, mk, mv)


def _merge_kernel(x_ref, wg0_ref, wg1_ref, wg2_ref, bg0_ref, bg1_ref, bg2_ref,
                  ha_ref, hb_ref, hc_ref, wa_ref, wb_ref, wc_ref, o_ref):
    x = x_ref[...]
    out = None
    for wg_ref, bg_ref, h_ref, wp_ref in ((wg0_ref, bg0_ref, ha_ref, wa_ref),
                                          (wg1_ref, bg1_ref, hb_ref, wb_ref),
                                          (wg2_ref, bg2_ref, hc_ref, wc_ref)):
        gate = jax.nn.sigmoid(_dot(x, wg_ref[...]) + bg_ref[...])
        term = gate * _dot(h_ref[...], wp_ref[...])
        out = term if out is None else out + term
    o_ref[...] = out.astype(BF16)


def _merge(xb, wg, bg, ha, hb, hc, w_pa, w_pb, w_pc, *, bm=512, bn=256):
    m = xb.shape[0]
    nb = D_MODEL // bn
    row = lambda width: pl.BlockSpec((bm, width), lambda i, j: (i, 0))
    colw = lambda kdim: pl.BlockSpec((kdim, bn), lambda i, j: (0, j))
    gate_w = [pl.BlockSpec((D_MODEL, bn), lambda i, j, g=g: (0, g * nb + j)) for g in range(N_BRANCH)]
    gate_b = [pl.BlockSpec((1, bn), lambda i, j, g=g: (0, g * nb + j)) for g in range(N_BRANCH)]
    return pl.pallas_call(
        _merge_kernel,
        out_shape=jax.ShapeDtypeStruct((m, D_MODEL), BF16),
        grid=(m // bm, nb),
        in_specs=[row(D_MODEL)] + gate_w + gate_b
                 + [row(MLSTM_W), row(SB_W), row(POOL_W), colw(MLSTM_W), colw(SB_W), colw(POOL_W)],
        out_specs=pl.BlockSpec((bm, bn), lambda i, j: (i, j)),
        compiler_params=_params("parallel", "arbitrary"),
        name="gated_merge",
    )(xb, wg, wg, wg, bg, bg, bg, ha, hb, hc, w_pa, w_pb, w_pc)


def _trunk_layer(x, xb, W, l, groups):
    w_in = W["w_in"][l]
    b_in = W["b_in"][l]
    bf = lambda a: a.astype(BF16)

    k_scale = jnp.concatenate([jnp.ones((MLSTM_QK_W,), F32),
                               jnp.full((MLSTM_QK_W,), MLSTM_DK ** -0.5, F32),
                               jnp.ones((MLSTM_W,), F32)])
    qkv = _matmul(xb, bf(w_in[:, :OFF_AO]), b_in[:OFF_AO], k_scale, out_dtype=BF16, name="proj_qkv")
    ao = _matmul(xb, bf(w_in[:, OFF_AO:OFF_AI]), b_in[OFF_AO:OFF_AI], act="sigmoid", name="proj_ogate")
    w_if = jnp.pad(bf(w_in[:, OFF_AI:OFF_BQ]), ((0, 0), (0, GATE_PAD - 2 * MLSTM_HEADS)))
    b_if = jnp.pad(b_in[OFF_AI:OFF_BQ], (0, GATE_PAD - 2 * MLSTM_HEADS))
    gif = _matmul(xb, w_if, b_if, name="proj_if")
    bc = _matmul(xb, bf(w_in[:, OFF_BQ:OFF_G]), b_in[OFF_BQ:OFF_G], name="proj_sb_pool")

    w_pool = bf(W["w_pool"][l])
    has, hbs, hcs, states = [], [], [], []
    for g in groups:
        dims = dict(bsz=g["bsz"], t=g["t"], row_off=g["row_off"])
        ha, c_new, n_new, m_new = _mlstm(qkv, ao, gif, g["c"][l], g["n"][l], g["m"][l],
                                         W["mlstm_norm_g"][l], **dims)
        hb = _stick_breaking(bc, *(g["sb_past"][l] if g["sb_past"] else (None, None)), **dims)
        hc = _pool(bc, g["pool_buf"][l], w_pool, W["pool_scale"][l], pos0=g["pos0"], **dims)
        has.append(ha); hbs.append(hb); hcs.append(hc)
        states.append((c_new, n_new, m_new))
    ha, hb, hc = (jnp.concatenate(v, axis=0) for v in (has, hbs, hcs))

    merged = _merge(xb, bf(w_in[:, OFF_G:]), b_in[OFF_G:].reshape(1, -1), ha, hb, hc,
                    bf(W["w_pa"][l]), bf(W["w_pb"][l]), bf(W["w_pc"][l]))
    y = _matmul(merged, bf(W["w_out"][l]), name="w_out")
    x, xb = _layer_norm(x, W["ln1_g"][l], W["ln1_b"][l], res=y)

    qx = _matmul(xb, bf(W["wq_x"][l]), out_dtype=BF16, name="wq_x")
    ox = jnp.concatenate([_cross_attend(qx, g["mk"][l], g["mv"][l], bsz=g["bsz"], t=g["t"],
                                        row_off=g["row_off"]) for g in groups], axis=0)
    y = _matmul(ox, bf(W["wo_x"][l]), name="wo_x")
    x, xb = _layer_norm(x, W["ln2_g"][l], W["ln2_b"][l], res=y)

    hid = _matmul(xb, bf(W["w_up"][l]), act="relu2", out_dtype=BF16, name="w_up")
    y = _matmul(hid, bf(W["w_down"][l]), bk=D_MODEL, name="w_down")
    x, xb = _layer_norm(x, W["ln3_g"][l], W["ln3_b"][l], res=y)
    return x, xb, bc, states


def kernel(x_prompt, x_sample, cache_sb_k, cache_sb_v, state_mlstm_c, state_mlstm_n, state_mlstm_m, state_pool, cache_mem_k, cache_mem_v, mem_prompt, ln_in_g, ln_in_b, w_in, b_in, mlstm_norm_g, w_pool, pool_scale, w_pa, w_pb, w_pc, w_out, ln1_g, ln1_b, wq_x, wk_x, wv_x, wo_x, ln2_g, ln2_b, w_up, w_down, ln3_g, ln3_b):
    W = dict(w_in=w_in, b_in=b_in, mlstm_norm_g=mlstm_norm_g, w_pool=w_pool, pool_scale=pool_scale,
             w_pa=w_pa, w_pb=w_pb, w_pc=w_pc, w_out=w_out, ln1_g=ln1_g, ln1_b=ln1_b, wq_x=wq_x,
             wo_x=wo_x, ln2_g=ln2_g, ln2_b=ln2_b, w_up=w_up, w_down=w_down, ln3_g=ln3_g, ln3_b=ln3_b)
    bp, tp, _ = x_prompt.shape
    bs, ts, _ = x_sample.shape
    rows_p, rows_s = bp * tp, bs * ts
    past = cache_sb_k.shape[2]
    mem_len = mem_prompt.shape[1]

    mem_b = mem_prompt.reshape(bp * mem_len, D_MODEL).astype(BF16)
    mk_p = [_matmul(mem_b, wk_x[l].astype(BF16), name="mem_k") for l in range(DEPTH)]
    mv_p = [_matmul(mem_b, wv_x[l].astype(BF16), name="mem_v") for l in range(DEPTH)]

    lane_m = lambda m: jnp.broadcast_to(m[..., None], m.shape + (LANES,))
    halo = lambda b: jnp.pad(b, ((0, 0), (0, 0), (POOL_HALO - POOL_BUF, 0), (0, 0)))
    groups = [
        dict(bsz=bp, t=tp, row_off=0, pos0=0,
             c=jnp.zeros((DEPTH, bp, MLSTM_HEADS, MLSTM_DK, MLSTM_DV), F32),
             n=jnp.zeros((DEPTH, bp, MLSTM_HEADS, MLSTM_DK), F32),
             m=jnp.zeros((DEPTH, bp, MLSTM_HEADS, LANES), F32),
             sb_past=None,
             pool_buf=jnp.zeros((DEPTH, bp, POOL_HALO, POOL_W), F32),
             mk=[a.reshape(bp, mem_len, X_W) for a in mk_p],
             mv=[a.reshape(bp, mem_len, X_W) for a in mv_p]),
        dict(bsz=bs, t=ts, row_off=rows_p, pos0=past,
             c=state_mlstm_c, n=state_mlstm_n, m=lane_m(state_mlstm_m),
             sb_past=[(cache_sb_k[l].reshape(bs, past, SB_W), cache_sb_v[l].reshape(bs, past, SB_W))
                      for l in range(DEPTH)],
             pool_buf=halo(state_pool),
             mk=cache_mem_k.reshape(DEPTH, bs, mem_len, X_W),
             mv=cache_mem_v.reshape(DEPTH, bs, mem_len, X_W)),
    ]

    x_all = jnp.concatenate([x_prompt.reshape(rows_p, D_MODEL), x_sample.reshape(rows_s, D_MODEL)], axis=0)
    x, xb = _layer_norm(x_all, ln_in_g, ln_in_b)

    per_layer = []
    for l in range(DEPTH):
        x, xb, bc, states = _trunk_layer(x, xb, W, l, groups)
        per_layer.append((bc, states))

    def group_out(gi, bsz, t, row_off):
        sl = slice(row_off, row_off + bsz * t)
        sbk = jnp.stack([bc[sl, SB_W:2 * SB_W].reshape(bsz, t, SB_HEADS, SB_DH) for bc, _ in per_layer])
        sbv = jnp.stack([bc[sl, 2 * SB_W:3 * SB_W].reshape(bsz, t, SB_HEADS, SB_DH) for bc, _ in per_layer])
        c = jnp.stack([st[gi][0] for _, st in per_layer])
        n = jnp.stack([st[gi][1] for _, st in per_layer])
        m = jnp.stack([st[gi][2][..., 0] for _, st in per_layer])
        pool = jnp.stack([bc[sl, 3 * SB_W:].reshape(bsz, t, POOL_W)[:, t - POOL_BUF:, :] for bc, _ in per_layer])
        return sbk, sbv, c, n, m, pool

    out_p = group_out(0, bp, tp, 0)
    out_s = group_out(1, bs, ts, rows_p)
    mem_k_p = jnp.stack([a.reshape(bp, mem_len, X_HEADS, X_DH) for a in mk_p])
    mem_v_p = jnp.stack([a.reshape(bp, mem_len, X_HEADS, X_DH) for a in mv_p])
    y_prompt = x[:rows_p].reshape(bp, tp, D_MODEL)
    y_sample = x[rows_p:].reshape(bs, ts, D_MODEL)
    return (y_prompt, y_sample) + out_p + (mem_k_p, mem_v_p) + out_s
```

```python
import functools

import jax
import jax.numpy as jnp
from jax import lax
from jax.experimental import pallas as pl
from jax.experimental.pallas import tpu as pltpu

F32 = jnp.float32
BF16 = jnp.bfloat16

D_MODEL = 4096
DEPTH = 2
CHUNK = 64
MLSTM_HEADS = 8
MLSTM_W = D_MODEL // 2
MLSTM_DV = MLSTM_W // MLSTM_HEADS
MLSTM_DK = MLSTM_DV // 2
MLSTM_QK_W = MLSTM_HEADS * MLSTM_DK
SB_HEADS = 8
SB_W = D_MODEL // 4
SB_DH = SB_W // SB_HEADS
POOL_WINDOWS = (2, 4, 8, 16)
POOL_W = D_MODEL // 4
POOL_G = POOL_W // len(POOL_WINDOWS)
POOL_BUF = max(POOL_WINDOWS) - 1
POOL_HALO = POOL_BUF + 1
X_HEADS = 4
X_W = D_MODEL // 4
X_DH = X_W // X_HEADS
D_FF = 4 * D_MODEL
N_BRANCH = 3
ALPHA = (2 * DEPTH) ** 0.25
LN_EPS = 1e-5
HEAD_NORM_EPS = 1e-6

OFF_AO = 2 * MLSTM_QK_W + MLSTM_W
OFF_AI = OFF_AO + MLSTM_W
OFF_BQ = OFF_AI + 2 * MLSTM_HEADS
OFF_G = OFF_BQ + 3 * SB_W + POOL_W
IN_W = OFF_G + N_BRANCH * D_MODEL

V7X_VMEM_LIMIT_BYTES = 56 * 1024 * 1024
LANES = 128
SUBLANES = 8
SB_TQ = 256
GATE_PAD = LANES

R_AO = OFF_AO
R_BQ = OFF_AI
R_BK = R_BQ + SB_W
R_BV = R_BK + SB_W
R_CU = R_BV + SB_W
R_G = R_CU + POOL_W
R_IF = R_G + N_BRANCH * D_MODEL
R_W = R_IF + GATE_PAD

_NT = (((1,), (1,)), ((), ()))
_TN = (((0,), (0,)), ((), ()))


def _params(*sem):
    return pltpu.CompilerParams(dimension_semantics=sem,
                                vmem_limit_bytes=V7X_VMEM_LIMIT_BYTES)


def _dot(a, b):
    return jnp.dot(a, b, preferred_element_type=F32)


def _split(x, terms):
    out = []
    for _ in range(terms - 1):
        h = x.astype(BF16)
        out.append(h)
        x = x - h.astype(F32)
    out.append(x.astype(BF16))
    return out


def _alias_prev(prev, n_in, n_out=1):
    if prev is None:
        return [], [], {}
    prev = list(prev) if isinstance(prev, (tuple, list)) else [prev]
    specs = [pl.BlockSpec(memory_space=pl.ANY)] * len(prev)
    return prev, specs, {n_in + i: i for i in range(len(prev))}


def _mm_kernel(*refs, act, has_bias, has_scale, nk, heads8):
    it = iter(refs)
    x_ref, w_ref = next(it), next(it)
    b_ref = next(it) if has_bias else None
    s_ref = next(it) if has_scale else None
    o_ref = next(it)
    acc_ref = next(it) if nk > 1 else None

    def epilogue(y):
        if has_bias:
            y = y + b_ref[...]
        if has_scale:
            y = y * s_ref[...]
        if act == "sigmoid":
            y = jax.nn.sigmoid(y)
        elif act == "relu2":
            y = jnp.square(jnp.maximum(y, 0.0))
        if heads8:
            rows = y.shape[0]
            for h in range(SUBLANES):
                o_ref[pl.ds(h, rows, stride=SUBLANES), :] = y[:, h * LANES:(h + 1) * LANES]
        else:
            o_ref[...] = y.astype(o_ref.dtype)

    part = _dot(x_ref[...], w_ref[...])
    if nk == 1:
        epilogue(part)
    else:
        k = pl.program_id(2)

        @pl.when(k == 0)
        def _():
            acc_ref[...] = part

        @pl.when(k > 0)
        def _():
            acc_ref[...] += part

        @pl.when(k == nk - 1)
        def _():
            epilogue(acc_ref[...])


def _matmul(x, w, l, *, n, col_off=0, bias=None, scale=None, act="none", out_dtype=F32,
            bm=1024, bn=1024, bk=None, heads8=False, name="matmul"):
    m, kdim = x.shape
    bm, bn = min(bm, m), min(bn, n)
    bk = kdim if bk is None else bk
    assert m % bm == 0 and n % bn == 0 and kdim % bk == 0 and col_off % bn == 0
    nk = kdim // bk
    cb = col_off // bn
    in_specs = [pl.BlockSpec((bm, bk), lambda i, j, k: (i, k)),
                pl.BlockSpec((None, bk, bn), lambda i, j, k: (l, k, cb + j))]
    args = [x, w]
    for v in (bias, scale):
        if v is not None:
            in_specs.append(pl.BlockSpec((1, bn), lambda i, j, k: (0, j)))
            args.append(v.reshape(1, n).astype(F32))
    if heads8:
        assert n == bn == SUBLANES * LANES and out_dtype == F32
        out_shape = jax.ShapeDtypeStruct((m * SUBLANES, LANES), F32)
        out_spec = pl.BlockSpec((bm * SUBLANES, LANES), lambda i, j, k: (i, 0))
    else:
        out_shape = jax.ShapeDtypeStruct((m, n), out_dtype)
        out_spec = pl.BlockSpec((bm, bn), lambda i, j, k: (i, j))
    kern = functools.partial(_mm_kernel, act=act, has_bias=bias is not None,
                             has_scale=scale is not None, nk=nk, heads8=heads8)
    return pl.pallas_call(
        kern,
        out_shape=out_shape,
        grid=(m // bm, n // bn, nk),
        in_specs=in_specs,
        out_specs=out_spec,
        scratch_shapes=[pltpu.VMEM((bm, bn), F32)] if nk > 1 else [],
        compiler_params=_params("parallel", "parallel", "arbitrary"),
        name=name,
    )(*args)


def _ln_kernel(*refs, has_res, n_prev, emit_bf16):
    it = iter(refs)
    x_ref = next(it)
    r_ref = next(it) if has_res else None
    g_ref, b_ref = next(it), next(it)
    for _ in range(n_prev):
        next(it)
    o_ref = next(it)
    x = x_ref[...]
    if has_res:
        x = ALPHA * x + r_ref[...]
    mu = jnp.mean(x, axis=-1, keepdims=True)
    xc = x - mu
    var = jnp.mean(xc * xc, axis=-1, keepdims=True)
    y = xc * lax.rsqrt(var + LN_EPS) * g_ref[...] + b_ref[...]
    o_ref[...] = y
    if emit_bf16:
        next(it)[...] = y.astype(BF16)


def _layer_norm(x, g, b, res=None, *, in_off=0, n_rows=None, out_off=0, out_rows=None,
                prev=None, emit_bf16=True, rows=256):
    d = x.shape[1]
    n_rows = x.shape[0] - in_off if n_rows is None else n_rows
    out_rows = n_rows if out_rows is None else out_rows
    ib, ob = in_off // rows, out_off // rows
    in_spec = pl.BlockSpec((rows, d), lambda i: (ib + i, 0))
    out_spec = pl.BlockSpec((rows, d), lambda i: (ob + i, 0))
    vec_spec = pl.BlockSpec((1, d), lambda i: (0, 0))
    args = [x] + ([res] if res is not None else []) + [g.reshape(1, d), b.reshape(1, d)]
    in_specs = [in_spec] * (2 if res is not None else 1) + [vec_spec, vec_spec]
    prev_args, prev_specs, aliases = _alias_prev(prev, len(args))
    out_shape = [jax.ShapeDtypeStruct((out_rows, d), F32)]
    if emit_bf16:
        out_shape.append(jax.ShapeDtypeStruct((out_rows, d), BF16))
    out = pl.pallas_call(
        functools.partial(_ln_kernel, has_res=res is not None, n_prev=len(prev_args),
                          emit_bf16=emit_bf16),
        out_shape=tuple(out_shape),
        grid=(n_rows // rows,),
        in_specs=in_specs + prev_specs,
        out_specs=tuple([out_spec] * len(out_shape)),
        input_output_aliases=aliases,
        compiler_params=_params("parallel"),
        name="layer_norm",
    )(*args, *prev_args)
    return out if emit_bf16 else out[0]


def _mlstm_kernel(q_ref, k_ref, v_ref, ao_ref, gif_ref, c0_ref, n0_ref, m0_ref, g_ref, *rest):
    ha_ref, c_ref, n_ref, m_ref = rest[-4:]
    L = CHUNK

    @pl.when(pl.program_id(1) == 0)
    def _():
        c_ref[...] = c0_ref[...]
        n_ref[...] = n0_ref[...]
        m_ref[...] = m0_ref[...]

    gif = gif_ref[...]
    logf = -(jnp.maximum(-gif, 0.0) + jnp.log1p(jnp.exp(-jnp.abs(gif))))
    row = lax.broadcasted_iota(jnp.int32, (L, L), 0)
    col = lax.broadcasted_iota(jnp.int32, (L, L), 1)
    causal = col <= row
    tril = jnp.where(causal, 1.0, 0.0).astype(BF16)
    b_all = sum(_dot(tril, t) for t in _split(logf, 3))
    sel_r = lax.broadcasted_iota(jnp.int32, (2 * MLSTM_HEADS, GATE_PAD), 0)
    sel_c = lax.broadcasted_iota(jnp.int32, (2 * MLSTM_HEADS, GATE_PAD), 1)
    sel = jnp.where(sel_r == sel_c, 1.0, 0.0).astype(BF16)

    def rows_of(x):
        return sum(lax.dot_general(sel, t, _NT, preferred_element_type=F32) for t in _split(x, 3))

    i_rows = rows_of(gif)
    b_rows = rows_of(b_all)

    for h in range(MLSTM_HEADS):
        qs = slice(h * MLSTM_DK, (h + 1) * MLSTM_DK)
        vs = slice(h * MLSTM_DV, (h + 1) * MLSTM_DV)
        q = q_ref[:, qs]
        k = k_ref[:, qs]
        v = v_ref[:, vs]
        i_col = gif[:, h:h + 1]
        b_col = b_all[:, MLSTM_HEADS + h:MLSTM_HEADS + h + 1]
        i_row = i_rows[h:h + 1, :]
        b_row = b_rows[MLSTM_HEADS + h:MLSTM_HEADS + h + 1, :]
        c_old = c_ref[0, h]
        n_old = n_ref[0, h:h + 1, :]
        m_old = m_ref[0, h:h + 1, 0:1]

        dmat = jnp.where(causal, b_col - b_row + i_row, -jnp.inf)
        m_inter = b_col + m_old
        m_t = jnp.maximum(m_inter, jnp.max(dmat, axis=-1, keepdims=True))
        s = lax.dot_general(q, k, _NT, preferred_element_type=F32) * jnp.exp(dmat - m_t)
        w_inter = jnp.exp(m_inter - m_t)
        kf = k.astype(F32)
        num = _dot(s.astype(BF16), v) + w_inter * _dot(q, c_old.astype(BF16))
        qn = jnp.sum(q.astype(F32) * n_old, axis=-1, keepdims=True)
        den = jnp.sum(s, axis=-1, keepdims=True) + w_inter * qn
        hh = num / jnp.maximum(jnp.abs(den), jnp.exp(-m_t))

        m_new = m_t[L - 1:L, :]
        b_last = b_col[L - 1:L, :]
        g_col = jnp.exp(b_last - b_col + i_col - m_new)
        decay = jnp.exp(b_last + m_old - m_new)
        kg = kf * g_col
        c_ref[0, h] = decay * c_old + lax.dot_general(kg.astype(BF16), v, _TN,
                                                      preferred_element_type=F32)
        n_ref[0, h:h + 1, :] = decay * n_old + jnp.sum(kg, axis=0, keepdims=True)
        m_ref[0, h:h + 1, :] = jnp.broadcast_to(m_new, (1, LANES))

        mu = jnp.mean(hh, axis=-1, keepdims=True)
        hc = hh - mu
        var = jnp.mean(hc * hc, axis=-1, keepdims=True)
        hn = hc * lax.rsqrt(var + HEAD_NORM_EPS) * g_ref[:, vs]
        ha_ref[:, vs] = (ao_ref[:, vs] * hn).astype(BF16)


def _mlstm(qkv, ao, gif, c0, n0, m0, norm_g, prev, *, bsz, t, row_off):
    nc = t // CHUNK
    off = row_off // CHUNK
    rows = lambda b, c: off + b * nc + c
    state4 = pl.BlockSpec((1, MLSTM_HEADS, MLSTM_DK, MLSTM_DV), lambda b, c: (b, 0, 0, 0))
    state3 = pl.BlockSpec((1, MLSTM_HEADS, LANES), lambda b, c: (b, 0, 0))
    args = [qkv, qkv, qkv, ao, gif, c0, n0, m0, norm_g.reshape(1, MLSTM_W)]
    prev_args, prev_specs, aliases = _alias_prev(prev, len(args))
    return pl.pallas_call(
        _mlstm_kernel,
        out_shape=(jax.ShapeDtypeStruct((qkv.shape[0], MLSTM_W), BF16),
                   jax.ShapeDtypeStruct(c0.shape, F32),
                   jax.ShapeDtypeStruct(n0.shape, F32),
                   jax.ShapeDtypeStruct(m0.shape, F32)),
        grid=(bsz, nc),
        in_specs=[
            pl.BlockSpec((CHUNK, MLSTM_QK_W), lambda b, c: (rows(b, c), 0)),
            pl.BlockSpec((CHUNK, MLSTM_QK_W), lambda b, c: (rows(b, c), 1)),
            pl.BlockSpec((CHUNK, MLSTM_W), lambda b, c: (rows(b, c), 1)),
            pl.BlockSpec((CHUNK, MLSTM_W), lambda b, c: (rows(b, c), 0)),
            pl.BlockSpec((CHUNK, GATE_PAD), lambda b, c: (rows(b, c), 0)),
            state4, state3, state3,
            pl.BlockSpec((1, MLSTM_W), lambda b, c: (0, 0)),
        ] + prev_specs,
        out_specs=(pl.BlockSpec((CHUNK, MLSTM_W), lambda b, c: (rows(b, c), 0)),
                   state4, state3, state3),
        input_output_aliases=aliases,
        compiler_params=_params("parallel", "arbitrary"),
        name="mlstm",
    )(*args, *prev_args)


SB_SCALE = SB_DH ** -0.5


def _lower_tri(n):
    return jnp.where(lax.broadcasted_iota(jnp.int32, (n, n), 0)
                     >= lax.broadcasted_iota(jnp.int32, (n, n), 1), 1.0, 0.0).astype(BF16)


def _sb_group(q_ref, load_k, load_v, heads, acc_ref, carry_ref, lower, *, diag):
    tq = q_ref.shape[0]
    tk = lower.shape[0]
    if diag:
        mask = (lax.broadcasted_iota(jnp.int32, (tq, tk), 1)
                < lax.broadcasted_iota(jnp.int32, (tq, tk), 0))
    zs, lks = [], []
    for h in heads:
        z = lax.dot_general(q_ref[:, h * SB_DH:(h + 1) * SB_DH], load_k(h), _NT,
                            preferred_element_type=F32) * SB_SCALE
        log_keep = -(jnp.maximum(z, 0.0) + jnp.log1p(jnp.exp(-jnp.abs(z))))
        if diag:
            log_keep = jnp.where(mask, log_keep, 0.0)
        zs.append(z)
        lks.append(log_keep)
    stacked = jnp.concatenate(lks, axis=0)
    rc_all = sum(_dot(t, lower) for t in _split(stacked, 2))
    for i, h in enumerate(heads):
        rc = rc_all[i * tq:(i + 1) * tq]
        carry = carry_ref[h]
        attn = jnp.exp(zs[i] + rc + carry)
        if diag:
            attn = jnp.where(mask, attn, 0.0)
        acc_ref[h] += _dot(attn.astype(BF16), load_v(h))
        carry_ref[h] = carry + rc[:, 0:1]


def _sb_block(q_ref, k_ref, v_ref, base, tk, acc_ref, carry_ref, lower, *, group, diag):
    def loader(ref):
        return lambda h: ref[pl.ds(base + h, tk, stride=SB_HEADS), :].astype(BF16)
    for g in range(0, SB_HEADS, group):
        _sb_group(q_ref, loader(k_ref), loader(v_ref), range(g, g + group),
                  acc_ref, carry_ref, lower, diag=diag)


def _sb_store(o_ref, acc_ref):
    for h in range(SB_HEADS):
        o_ref[:, h * SB_DH:(h + 1) * SB_DH] = acc_ref[h].astype(BF16)


def _sb_prompt_kernel(q_ref, k_ref, v_ref, *rest, tq, group):
    o_ref, acc_ref, carry_ref = rest[-3:]
    qi = pl.program_id(1)
    acc_ref[...] = jnp.zeros_like(acc_ref)
    carry_ref[...] = jnp.zeros_like(carry_ref)
    lower = _lower_tri(tq)
    blk_rows = tq * SB_HEADS

    _sb_block(q_ref, k_ref, v_ref, pl.multiple_of(qi * blk_rows, blk_rows), tq,
              acc_ref, carry_ref, lower, group=group, diag=True)

    def body(j, c):
        base = pl.multiple_of((qi - 1 - j) * blk_rows, blk_rows)
        _sb_block(q_ref, k_ref, v_ref, base, tq, acc_ref, carry_ref, lower,
                  group=group, diag=False)
        return c

    lax.fori_loop(0, qi, body, 0)
    _sb_store(o_ref, acc_ref)


def _sb_sample_kernel(q_ref, k_ref, v_ref, kp_ref, vp_ref, *rest, tq, n_past):
    o_ref, acc_ref, carry_ref = rest[-3:]
    s = pl.program_id(1)

    @pl.when(s == 0)
    def _():
        acc_ref[...] = jnp.zeros_like(acc_ref)
        carry_ref[...] = jnp.zeros_like(carry_ref)
        _sb_block(q_ref, k_ref, v_ref, 0, tq, acc_ref, carry_ref, _lower_tri(tq),
                  group=SB_HEADS, diag=True)

    @pl.when(s > 0)
    def _():
        _sb_block(q_ref, kp_ref, vp_ref, 0, SB_TQ, acc_ref, carry_ref, _lower_tri(SB_TQ),
                  group=SB_HEADS, diag=False)

    @pl.when(s == n_past)
    def _():
        _sb_store(o_ref, acc_ref)


def _stick_breaking(bq, k8, v8, past_k, past_v, l, prev, *, bsz, t, row_off):
    tq = min(SB_TQ, t)
    nq = t // tq
    off_q = row_off // tq
    off_t = row_off // t
    scratch = [pltpu.VMEM((SB_HEADS, tq, SB_DH), F32), pltpu.VMEM((SB_HEADS, tq, 1), F32)]
    out_shape = jax.ShapeDtypeStruct((bq.shape[0], SB_W), BF16)
    q_spec = pl.BlockSpec((tq, SB_W), lambda b, i: (off_q + b * nq + i, 0))
    if past_k is None:
        seq_spec = pl.BlockSpec((t * SB_HEADS, SB_DH), lambda b, i: (off_t + b, 0))
        args = [bq, k8, v8]
        prev_args, prev_specs, aliases = _alias_prev(prev, len(args))
        return pl.pallas_call(
            functools.partial(_sb_prompt_kernel, tq=tq, group=SB_HEADS // 2),
            out_shape=out_shape,
            grid=(bsz, nq),
            in_specs=[q_spec, seq_spec, seq_spec] + prev_specs,
            out_specs=q_spec,
            scratch_shapes=scratch,
            input_output_aliases=aliases,
            compiler_params=_params("parallel", "arbitrary"),
            name="stick_breaking_prompt",
        )(*args, *prev_args)
    assert nq == 1
    blk = SB_TQ * SB_HEADS
    n_past = past_k.shape[2] // blk
    q_spec = pl.BlockSpec((tq, SB_W), lambda b, s: (off_q + b, 0))
    new_spec = pl.BlockSpec((t * SB_HEADS, SB_DH), lambda b, s: (off_t + b, 0))
    past_spec = pl.BlockSpec((None, None, blk, SB_DH),
                             lambda b, s: (l, b, n_past - jnp.maximum(s, 1), 0))
    args = [bq, k8, v8, past_k, past_v]
    prev_args, prev_specs, aliases = _alias_prev(prev, len(args))
    return pl.pallas_call(
        functools.partial(_sb_sample_kernel, tq=tq, n_past=n_past),
        out_shape=out_shape,
        grid=(bsz, n_past + 1),
        in_specs=[q_spec, new_spec, new_spec, past_spec, past_spec] + prev_specs,
        out_specs=q_spec,
        scratch_shapes=scratch,
        input_output_aliases=aliases,
        compiler_params=_params("parallel", "arbitrary"),
        name="stick_breaking_sample",
    )(*args, *prev_args)


def _pool_kernel(u_ref, buf_ref, w_ref, s_ref, *rest, tt, pos0):
    o_ref, ext_ref = rest[-2:]
    i = pl.program_id(1)

    @pl.when(i == 0)
    def _():
        ext_ref[0:POOL_HALO, :] = buf_ref[0]

    @pl.when(i > 0)
    def _():
        ext_ref[0:POOL_HALO, :] = ext_ref[tt:tt + POOL_HALO, :]

    ext_ref[POOL_HALO:POOL_HALO + tt, :] = u_ref[...]
    pos = pos0 + i * tt + lax.broadcasted_iota(jnp.int32, (tt, 1), 0)
    for gi, w in enumerate(POOL_WINDOWS):
        cs = slice(gi * POOL_G, (gi + 1) * POOL_G)
        cur = ext_ref[POOL_HALO:POOL_HALO + tt, cs]
        wsum = cur
        for j in range(1, w):
            wsum = wsum + ext_ref[POOL_HALO - j:POOL_HALO - j + tt, cs]
        cnt = jnp.minimum(pos + 1, w).astype(F32)
        d = (wsum / cnt - cur).astype(BF16)
        y = _dot(d, w_ref[gi]) * s_ref[:, cs]
        o_ref[:, cs] = y.astype(BF16)


def _pool(u, buf, w_pool, scale, l, prev, *, bsz, t, row_off, pos0):
    tt = min(256, t)
    nt = t // tt
    off = row_off // tt
    row_spec = pl.BlockSpec((tt, POOL_W), lambda b, i: (off + b * nt + i, 0))
    args = [u, buf, w_pool, scale.reshape(1, POOL_W)]
    prev_args, prev_specs, aliases = _alias_prev(prev, len(args))
    return pl.pallas_call(
        functools.partial(_pool_kernel, tt=tt, pos0=pos0),
        out_shape=jax.ShapeDtypeStruct((u.shape[0], POOL_W), BF16),
        grid=(bsz, nt),
        in_specs=[
            row_spec,
            pl.BlockSpec((1, POOL_HALO, POOL_W), lambda b, i: (b, 0, 0)),
            pl.BlockSpec((None, len(POOL_WINDOWS), POOL_G, POOL_G), lambda b, i: (l, 0, 0, 0)),
            pl.BlockSpec((1, POOL_W), lambda b, i: (0, 0)),
        ] + prev_specs,
        out_specs=row_spec,
        scratch_shapes=[pltpu.VMEM((POOL_HALO + tt, POOL_W), F32)],
        input_output_aliases=aliases,
        compiler_params=_params("parallel", "arbitrary"),
        name="pool_mix",
    )(*args, *prev_args)


X_SCALE = X_DH ** -0.5


def _cross_kernel(q_ref, k_ref, v_ref, *rest):
    o_ref = rest[-1]
    for h in range(X_HEADS):
        cs = slice(h * X_DH, (h + 1) * X_DH)
        k = k_ref[0, :, cs].astype(BF16)
        v = v_ref[0, :, cs].astype(BF16)
        s = lax.dot_general(q_ref[:, cs], k, _NT, preferred_element_type=F32) * X_SCALE
        e = jnp.exp(s - jnp.max(s, axis=-1, keepdims=True))
        p = e / jnp.sum(e, axis=-1, keepdims=True)
        o_ref[:, cs] = _dot(p.astype(BF16), v).astype(BF16)


def _cross_attend(qx, mk, mv, prev, *, bsz, t, row_off):
    tq = min(256, t)
    nq = t // tq
    off = row_off // tq
    row_spec = pl.BlockSpec((tq, X_W), lambda b, i: (off + b * nq + i, 0))
    mem_spec = pl.BlockSpec((1, mk.shape[1], X_W), lambda b, i: (b, 0, 0))
    args = [qx, mk, mv]
    prev_args, prev_specs, aliases = _alias_prev(prev, len(args))
    return pl.pallas_call(
        _cross_kernel,
        out_shape=jax.ShapeDtypeStruct((qx.shape[0], X_W), BF16),
        grid=(bsz, nq),
        in_specs=[row_spec, mem_spec, mem_spec] + prev_specs,
        out_specs=row_spec,
        input_output_aliases=aliases,
        compiler_params=_params("parallel", "arbitrary"),
        name="cross_attend",
    )(*args, *prev_args)


def _merge_kernel(x_ref, wg0_ref, wg1_ref, wg2_ref, bg0_ref, bg1_ref, bg2_ref,
                  ha_ref, hb_ref, hc_ref, wa_ref, wb_ref, wc_ref, o_ref):
    x = x_ref[...]
    out = None
    for wg_ref, bg_ref, h_ref, wp_ref in ((wg0_ref, bg0_ref, ha_ref, wa_ref),
                                          (wg1_ref, bg1_ref, hb_ref, wb_ref),
                                          (wg2_ref, bg2_ref, hc_ref, wc_ref)):
        gate = jax.nn.sigmoid(_dot(x, wg_ref[...]) + bg_ref[...])
        term = gate * _dot(h_ref[...], wp_ref[...])
        out = term if out is None else out + term
    o_ref[...] = out.astype(BF16)


def _merge(xb, w_in_r, bg, ha, hb, hc, w_pa, w_pb, w_pc, l, *, bm=512, bn=256):
    m = xb.shape[0]
    nb = D_MODEL // bn
    gb = R_G // bn
    row = lambda width: pl.BlockSpec((bm, width), lambda i, j: (i, 0))
    colw = lambda kdim: pl.BlockSpec((None, kdim, bn), lambda i, j: (l, 0, j))
    gate_w = [pl.BlockSpec((None, D_MODEL, bn), lambda i, j, g=g: (l, 0, gb + g * nb + j))
              for g in range(N_BRANCH)]
    gate_b = [pl.BlockSpec((1, bn), lambda i, j, g=g: (0, g * nb + j)) for g in range(N_BRANCH)]
    return pl.pallas_call(
        _merge_kernel,
        out_shape=jax.ShapeDtypeStruct((m, D_MODEL), BF16),
        grid=(m // bm, nb),
        in_specs=[row(D_MODEL)] + gate_w + gate_b
                 + [row(MLSTM_W), row(SB_W), row(POOL_W), colw(MLSTM_W), colw(SB_W), colw(POOL_W)],
        out_specs=pl.BlockSpec((bm, bn), lambda i, j: (i, j)),
        compiler_params=_params("parallel", "arbitrary"),
        name="gated_merge",
    )(xb, w_in_r, w_in_r, w_in_r, bg, bg, bg, ha, hb, hc, w_pa, w_pb, w_pc)


def _trunk_layer(x, xb, W, l, groups, last):
    w_in_r, b_in = W["w_in_r"], W["b_in"][l]
    proj = functools.partial(_matmul, xb, w_in_r, l)

    k_scale = jnp.concatenate([jnp.ones((MLSTM_QK_W,), F32),
                               jnp.full((MLSTM_QK_W,), MLSTM_DK ** -0.5, F32),
                               jnp.ones((MLSTM_W,), F32)])
    qkv = proj(n=OFF_AO, bias=b_in[:OFF_AO], scale=k_scale, out_dtype=BF16, name="proj_qkv")
    ao = proj(n=MLSTM_W, col_off=R_AO, bias=b_in[OFF_AO:OFF_AI], act="sigmoid", name="proj_ogate")
    b_if = jnp.pad(b_in[OFF_AI:OFF_BQ], (0, GATE_PAD - 2 * MLSTM_HEADS))
    gif = proj(n=GATE_PAD, col_off=R_IF, bias=b_if, name="proj_if")
    sec = lambda i: b_in[OFF_BQ + i * SB_W:OFF_BQ + (i + 1) * SB_W]
    bq = proj(n=SB_W, col_off=R_BQ, bias=sec(0), out_dtype=BF16, name="proj_sb_q")
    k8 = proj(n=SB_W, col_off=R_BK, bias=sec(1), heads8=True, name="proj_sb_k")
    v8 = proj(n=SB_W, col_off=R_BV, bias=sec(2), heads8=True, name="proj_sb_v")
    u = proj(n=POOL_W, col_off=R_CU, bias=sec(3), name="proj_pool_u")

    ha = hb = hc = None
    states = []
    for g in groups:
        dims = dict(bsz=g["bsz"], t=g["t"], row_off=g["row_off"])
        ha, c_new, n_new, m_new = _mlstm(qkv, ao, gif, g["c"][l], g["n"][l], g["m"][l],
                                         W["mlstm_norm_g"][l], ha, **dims)
        hb = _stick_breaking(bq, k8, v8, g["sb_k"], g["sb_v"], l, hb, **dims)
        hc = _pool(u, g["pool_buf"][l], W["w_pool"], W["pool_scale"][l], l, hc,
                   pos0=g["pos0"], **dims)
        states.append((c_new, n_new, m_new))

    merged = _merge(xb, w_in_r, b_in[OFF_G:].reshape(1, -1), ha, hb, hc,
                    W["w_pa"], W["w_pb"], W["w_pc"], l)
    y = _matmul(merged, W["w_out"], l, n=D_MODEL, name="w_out")
    x, xb = _layer_norm(x, W["ln1_g"][l], W["ln1_b"][l], res=y)

    qx = _matmul(xb, W["wq_x"], l, n=X_W, out_dtype=BF16, name="wq_x")
    ox = None
    for g in groups:
        ox = _cross_attend(qx, g["mk"][l], g["mv"][l], ox, bsz=g["bsz"], t=g["t"],
                           row_off=g["row_off"])
    y = _matmul(ox, W["wo_x"], l, n=D_MODEL, name="wo_x")
    x, xb = _layer_norm(x, W["ln2_g"][l], W["ln2_b"][l], res=y)

    hid = _matmul(xb, W["w_up"], l, n=D_FF, act="relu2", out_dtype=BF16, name="w_up")
    y = _matmul(hid, W["w_down"], l, n=D_MODEL, bk=D_MODEL, name="w_down")
    if last:
        x = tuple(_layer_norm(x, W["ln3_g"][l], W["ln3_b"][l], res=y, in_off=g["row_off"],
                              n_rows=g["bsz"] * g["t"], emit_bf16=False) for g in groups)
        xb = None
    else:
        x, xb = _layer_norm(x, W["ln3_g"][l], W["ln3_b"][l], res=y)
    return x, xb, (k8, v8, u), states


def kernel(x_prompt, x_sample, cache_sb_k, cache_sb_v, state_mlstm_c, state_mlstm_n, state_mlstm_m, state_pool, cache_mem_k, cache_mem_v, mem_prompt, ln_in_g, ln_in_b, w_in, b_in, mlstm_norm_g, w_pool, pool_scale, w_pa, w_pb, w_pc, w_out, ln1_g, ln1_b, wq_x, wk_x, wv_x, wo_x, ln2_g, ln2_b, w_up, w_down, ln3_g, ln3_b):
    bf = lambda a: a.astype(BF16)
    w_in_r = jnp.concatenate(
        [bf(w_in[:, :, :OFF_AI]), bf(w_in[:, :, OFF_BQ:]),
         jnp.pad(bf(w_in[:, :, OFF_AI:OFF_BQ]), ((0, 0), (0, 0), (0, GATE_PAD - 2 * MLSTM_HEADS)))],
        axis=-1)
    W = dict(w_in_r=w_in_r, b_in=b_in, mlstm_norm_g=mlstm_norm_g, w_pool=bf(w_pool),
             pool_scale=pool_scale, w_pa=bf(w_pa), w_pb=bf(w_pb), w_pc=bf(w_pc), w_out=bf(w_out),
             ln1_g=ln1_g, ln1_b=ln1_b, wq_x=bf(wq_x), wo_x=bf(wo_x), ln2_g=ln2_g, ln2_b=ln2_b,
             w_up=bf(w_up), w_down=bf(w_down), ln3_g=ln3_g, ln3_b=ln3_b)
    bp, tp, _ = x_prompt.shape
    bs, ts, _ = x_sample.shape
    rows_p, rows_s = bp * tp, bs * ts
    rows = rows_p + rows_s
    past = cache_sb_k.shape[2]
    mem_len = mem_prompt.shape[1]

    mem_b = mem_prompt.reshape(bp * mem_len, D_MODEL).astype(BF16)
    wk_b, wv_b = bf(wk_x), bf(wv_x)
    mk_p = [_matmul(mem_b, wk_b, l, n=X_W, name="mem_k") for l in range(DEPTH)]
    mv_p = [_matmul(mem_b, wv_b, l, n=X_W, name="mem_v") for l in range(DEPTH)]

    lane_m = lambda m: jnp.broadcast_to(m[..., None], m.shape + (LANES,))
    halo = lambda b: jnp.pad(b, ((0, 0), (0, 0), (POOL_HALO - POOL_BUF, 0), (0, 0)))
    heads_rows = lambda c: c.reshape(DEPTH, bs, past * SB_HEADS, SB_DH)
    groups = [
        dict(bsz=bp, t=tp, row_off=0, pos0=0,
             c=jnp.zeros((DEPTH, bp, MLSTM_HEADS, MLSTM_DK, MLSTM_DV), F32),
             n=jnp.zeros((DEPTH, bp, MLSTM_HEADS, MLSTM_DK), F32),
             m=jnp.zeros((DEPTH, bp, MLSTM_HEADS, LANES), F32),
             sb_k=None, sb_v=None,
             pool_buf=jnp.zeros((DEPTH, bp, POOL_HALO, POOL_W), F32),
             mk=[a.reshape(bp, mem_len, X_W) for a in mk_p],
             mv=[a.reshape(bp, mem_len, X_W) for a in mv_p]),
        dict(bsz=bs, t=ts, row_off=rows_p, pos0=past,
             c=state_mlstm_c, n=state_mlstm_n, m=lane_m(state_mlstm_m),
             sb_k=heads_rows(cache_sb_k), sb_v=heads_rows(cache_sb_v),
             pool_buf=halo(state_pool),
             mk=cache_mem_k.reshape(DEPTH, bs, mem_len, X_W),
             mv=cache_mem_v.reshape(DEPTH, bs, mem_len, X_W)),
    ]

    xs = None
    for g, xin in zip(groups, (x_prompt, x_sample)):
        xs = _layer_norm(xin.reshape(g["bsz"] * g["t"], D_MODEL), ln_in_g, ln_in_b,
                         out_off=g["row_off"], out_rows=rows, prev=xs)
    x, xb = xs

    per_layer = []
    for l in range(DEPTH):
        x, xb, kvu, states = _trunk_layer(x, xb, W, l, groups, last=l == DEPTH - 1)
        per_layer.append((kvu, states))

    def group_out(gi, g):
        bsz, t, r0 = g["bsz"], g["t"], g["row_off"]
        heads = lambda a: a[r0 * SB_HEADS:(r0 + bsz * t) * SB_HEADS].reshape(bsz, t, SB_HEADS, SB_DH)
        sbk = jnp.stack([heads(kvu[0]) for kvu, _ in per_layer])
        sbv = jnp.stack([heads(kvu[1]) for kvu, _ in per_layer])
        c = jnp.stack([st[gi][0] for _, st in per_layer])
        n = jnp.stack([st[gi][1] for _, st in per_layer])
        m = jnp.stack([st[gi][2][..., 0] for _, st in per_layer])
        pool = jnp.stack([kvu[2][r0:r0 + bsz * t].reshape(bsz, t, POOL_W)[:, t - POOL_BUF:, :]
                          for kvu, _ in per_layer])
        return sbk, sbv, c, n, m, pool

    out_p = group_out(0, groups[0])
    out_s = group_out(1, groups[1])
    mem_k_p = jnp.stack([a.reshape(bp, mem_len, X_HEADS, X_DH) for a in mk_p])
    mem_v_p = jnp.stack([a.reshape(bp, mem_len, X_HEADS, X_DH) for a in mv_p])
    y_prompt = x[0].reshape(bp, tp, D_MODEL)
    y_sample = x[1].reshape(bs, ts, D_MODEL)
    return (y_prompt, y_sample) + out_p + (mem_k_p, mem_v_p) + out_s
```

```python
import functools

import jax
import jax.numpy as jnp
from jax import lax
from jax.experimental import pallas as pl
from jax.experimental.pallas import tpu as pltpu

F32 = jnp.float32
BF16 = jnp.bfloat16

D_MODEL = 4096
DEPTH = 2
MLSTM_CHUNK = 256
MLSTM_HEADS = 8
MLSTM_W = D_MODEL // 2
MLSTM_DV = MLSTM_W // MLSTM_HEADS
MLSTM_DK = MLSTM_DV // 2
MLSTM_QK_W = MLSTM_HEADS * MLSTM_DK
SB_HEADS = 8
SB_W = D_MODEL // 4
SB_DH = SB_W // SB_HEADS
POOL_WINDOWS = (2, 4, 8, 16)
POOL_W = D_MODEL // 4
POOL_G = POOL_W // len(POOL_WINDOWS)
POOL_BUF = max(POOL_WINDOWS) - 1
POOL_HALO = POOL_BUF + 1
X_HEADS = 4
X_W = D_MODEL // 4
X_DH = X_W // X_HEADS
D_FF = 4 * D_MODEL
N_BRANCH = 3
ALPHA = (2 * DEPTH) ** 0.25
LN_EPS = 1e-5
HEAD_NORM_EPS = 1e-6

OFF_AO = 2 * MLSTM_QK_W + MLSTM_W
OFF_AI = OFF_AO + MLSTM_W
OFF_BQ = OFF_AI + 2 * MLSTM_HEADS
OFF_G = OFF_BQ + 3 * SB_W + POOL_W
IN_W = OFF_G + N_BRANCH * D_MODEL

V7X_VMEM_LIMIT_BYTES = 56 * 1024 * 1024
LANES = 128
SUBLANES = 8
SB_TQ = 256
GATE_PAD = LANES

B_BK = SB_W
B_BV = 2 * SB_W
B_CU = 3 * SB_W
B_G = 3 * SB_W + POOL_W

_NT = (((1,), (1,)), ((), ()))
_TN = (((0,), (0,)), ((), ()))


def _params(*sem):
    return pltpu.CompilerParams(dimension_semantics=sem,
                                vmem_limit_bytes=V7X_VMEM_LIMIT_BYTES)


def _dot(a, b):
    return jnp.dot(a, b, preferred_element_type=F32)


def _split(x, terms):
    out = []
    for _ in range(terms - 1):
        h = x.astype(BF16)
        out.append(h)
        x = x - h.astype(F32)
    out.append(x.astype(BF16))
    return out


def _alias_prev(prev, n_in):
    if prev is None:
        return [], [], {}
    prev = list(prev) if isinstance(prev, (tuple, list)) else [prev]
    specs = [pl.BlockSpec(memory_space=pl.ANY)] * len(prev)
    return prev, specs, {n_in + i: i for i in range(len(prev))}


def _mm_kernel(*refs, act, has_bias, has_scale, nk):
    it = iter(refs)
    x_ref, w_ref = next(it), next(it)
    b_ref = next(it) if has_bias else None
    s_ref = next(it) if has_scale else None
    o_ref = next(it)
    acc_ref = next(it) if nk > 1 else None

    def epilogue(y):
        if has_bias:
            y = y + b_ref[...]
        if has_scale:
            y = y * s_ref[...]
        if act == "sigmoid":
            y = jax.nn.sigmoid(y)
        elif act == "relu2":
            y = jnp.square(jnp.maximum(y, 0.0))
        o_ref[...] = y.astype(o_ref.dtype)

    part = _dot(x_ref[...], w_ref[...].astype(BF16))
    if nk == 1:
        epilogue(part)
    else:
        k = pl.program_id(2)

        @pl.when(k == 0)
        def _():
            acc_ref[...] = part

        @pl.when(k > 0)
        def _():
            acc_ref[...] += part

        @pl.when(k == nk - 1)
        def _():
            epilogue(acc_ref[...])


def _matmul(x, w, l, *, n, col_off=0, bias=None, scale=None, act="none", out_dtype=F32,
            bm=1024, bn=1024, bk=None, name="matmul"):
    m, kdim = x.shape
    bm, bn = min(bm, m), min(bn, n)
    bk = kdim if bk is None else bk
    assert m % bm == 0 and n % bn == 0 and kdim % bk == 0 and col_off % bn == 0
    nk = kdim // bk
    cb = col_off // bn
    in_specs = [pl.BlockSpec((bm, bk), lambda i, j, k: (i, k)),
                pl.BlockSpec((None, bk, bn), lambda i, j, k: (l, k, cb + j))]
    args = [x, w]
    for v in (bias, scale):
        if v is not None:
            in_specs.append(pl.BlockSpec((1, bn), lambda i, j, k: (0, j)))
            args.append(v.reshape(1, n).astype(F32))
    kern = functools.partial(_mm_kernel, act=act, has_bias=bias is not None,
                             has_scale=scale is not None, nk=nk)
    return pl.pallas_call(
        kern,
        out_shape=jax.ShapeDtypeStruct((m, n), out_dtype),
        grid=(m // bm, n // bn, nk),
        in_specs=in_specs,
        out_specs=pl.BlockSpec((bm, bn), lambda i, j, k: (i, j)),
        scratch_shapes=[pltpu.VMEM((bm, bn), F32)] if nk > 1 else [],
        compiler_params=_params("parallel", "parallel", "arbitrary"),
        name=name,
    )(*args)


def _proj_heads_kernel(x_ref, w_ref, b_ref, *rest, n_first):
    first_ref, second_ref = rest[-2:]
    y = _dot(x_ref[...], w_ref[...]) + b_ref[...]
    rows = y.shape[0]

    def store(o_ref):
        for h in range(SUBLANES):
            o_ref[pl.ds(h, rows, stride=SUBLANES), :] = y[:, h * LANES:(h + 1) * LANES]

    i = pl.program_id(0)

    @pl.when(i < n_first)
    def _():
        store(first_ref)

    @pl.when(i >= n_first)
    def _():
        store(second_ref)


def _proj_heads(x, w, l, bias, col_off, group_rows, prev, *, bm=1024, name="proj_heads"):
    m, kdim = x.shape
    n = SUBLANES * LANES
    rows_a, rows_b = group_rows
    assert rows_a % bm == 0 and rows_b % bm == 0 and rows_a + rows_b == m and col_off % n == 0
    n_first = rows_a // bm
    cb = col_off // n
    args = [x, w, bias.reshape(1, n)]
    prev_args, prev_specs, aliases = _alias_prev(prev, len(args))
    blk = (None, bm * SUBLANES, LANES)
    return pl.pallas_call(
        functools.partial(_proj_heads_kernel, n_first=n_first),
        out_shape=tuple(jax.ShapeDtypeStruct((DEPTH, r * SUBLANES, LANES), F32) for r in group_rows),
        grid=(m // bm,),
        in_specs=[pl.BlockSpec((bm, kdim), lambda i: (i, 0)),
                  pl.BlockSpec((None, kdim, n), lambda i: (l, 0, cb)),
                  pl.BlockSpec((1, n), lambda i: (0, 0))] + prev_specs,
        out_specs=(pl.BlockSpec(blk, lambda i: (l, jnp.minimum(i, n_first - 1), 0)),
                   pl.BlockSpec(blk, lambda i: (l, jnp.maximum(i - n_first, 0), 0))),
        input_output_aliases=aliases,
        compiler_params=_params("arbitrary"),
        name=name,
    )(*args, *prev_args)


def _ln_kernel(*refs, has_res, n_prev, emit_bf16):
    it = iter(refs)
    x_ref = next(it)
    r_ref = next(it) if has_res else None
    g_ref, b_ref = next(it), next(it)
    for _ in range(n_prev):
        next(it)
    o_ref = next(it)
    x = x_ref[...]
    if has_res:
        x = ALPHA * x + r_ref[...]
    mu = jnp.mean(x, axis=-1, keepdims=True)
    xc = x - mu
    var = jnp.mean(xc * xc, axis=-1, keepdims=True)
    y = xc * lax.rsqrt(var + LN_EPS) * g_ref[...] + b_ref[...]
    o_ref[...] = y
    if emit_bf16:
        next(it)[...] = y.astype(BF16)


def _layer_norm(x, g, b, res=None, *, in_off=0, n_rows=None, out_off=0, out_rows=None,
                prev=None, emit_bf16=True, rows=256):
    d = x.shape[1]
    n_rows = x.shape[0] - in_off if n_rows is None else n_rows
    out_rows = n_rows if out_rows is None else out_rows
    ib, ob = in_off // rows, out_off // rows
    in_spec = pl.BlockSpec((rows, d), lambda i: (ib + i, 0))
    out_spec = pl.BlockSpec((rows, d), lambda i: (ob + i, 0))
    vec_spec = pl.BlockSpec((1, d), lambda i: (0, 0))
    args = [x] + ([res] if res is not None else []) + [g.reshape(1, d), b.reshape(1, d)]
    in_specs = [in_spec] * (2 if res is not None else 1) + [vec_spec, vec_spec]
    prev_args, prev_specs, aliases = _alias_prev(prev, len(args))
    out_shape = [jax.ShapeDtypeStruct((out_rows, d), F32)]
    if emit_bf16:
        out_shape.append(jax.ShapeDtypeStruct((out_rows, d), BF16))
    out = pl.pallas_call(
        functools.partial(_ln_kernel, has_res=res is not None, n_prev=len(prev_args),
                          emit_bf16=emit_bf16),
        out_shape=tuple(out_shape),
        grid=(n_rows // rows,),
        in_specs=in_specs + prev_specs,
        out_specs=tuple([out_spec] * len(out_shape)),
        input_output_aliases=aliases,
        compiler_params=_params("parallel"),
        name="layer_norm",
    )(*args, *prev_args)
    return out if emit_bf16 else out[0]


def _mlstm_kernel(q_ref, k_ref, v_ref, ao_ref, gif_ref, c0_ref, n0_ref, m0_ref, g_ref, *rest):
    ha_ref, c_ref, n_ref, m_ref = rest[-4:]
    L = q_ref.shape[0]

    @pl.when(pl.program_id(1) == 0)
    def _():
        c_ref[...] = c0_ref[...]
        n_ref[...] = n0_ref[...]
        m_ref[...] = m0_ref[...]

    gif = gif_ref[...]
    logf = -(jnp.maximum(-gif, 0.0) + jnp.log1p(jnp.exp(-jnp.abs(gif))))
    row = lax.broadcasted_iota(jnp.int32, (L, L), 0)
    col = lax.broadcasted_iota(jnp.int32, (L, L), 1)
    causal = col <= row
    tril = jnp.where(causal, 1.0, 0.0).astype(BF16)
    b_all = sum(_dot(tril, t) for t in _split(logf, 3))
    sel_r = lax.broadcasted_iota(jnp.int32, (2 * MLSTM_HEADS, GATE_PAD), 0)
    sel_c = lax.broadcasted_iota(jnp.int32, (2 * MLSTM_HEADS, GATE_PAD), 1)
    sel = jnp.where(sel_r == sel_c, 1.0, 0.0).astype(BF16)

    def rows_of(x):
        return sum(lax.dot_general(sel, t, _NT, preferred_element_type=F32) for t in _split(x, 3))

    i_rows = rows_of(gif)
    b_rows = rows_of(b_all)

    for h in range(MLSTM_HEADS):
        qs = slice(h * MLSTM_DK, (h + 1) * MLSTM_DK)
        vs = slice(h * MLSTM_DV, (h + 1) * MLSTM_DV)
        q = q_ref[:, qs]
        k = k_ref[:, qs]
        v = v_ref[:, vs]
        i_col = gif[:, h:h + 1]
        b_col = b_all[:, MLSTM_HEADS + h:MLSTM_HEADS + h + 1]
        i_row = i_rows[h:h + 1, :]
        b_row = b_rows[MLSTM_HEADS + h:MLSTM_HEADS + h + 1, :]
        c_old = c_ref[0, h]
        n_old = n_ref[0, h:h + 1, :]
        m_old = m_ref[0, h:h + 1, 0:1]

        dmat = jnp.where(causal, b_col - b_row + i_row, -jnp.inf)
        m_inter = b_col + m_old
        m_t = jnp.maximum(m_inter, jnp.max(dmat, axis=-1, keepdims=True))
        s = lax.dot_general(q, k, _NT, preferred_element_type=F32) * jnp.exp(dmat - m_t)
        w_inter = jnp.exp(m_inter - m_t)
        kf = k.astype(F32)
        num = _dot(s.astype(BF16), v) + w_inter * _dot(q, c_old.astype(BF16))
        qn = jnp.sum(q.astype(F32) * n_old, axis=-1, keepdims=True)
        den = jnp.sum(s, axis=-1, keepdims=True) + w_inter * qn
        hh = num / jnp.maximum(jnp.abs(den), jnp.exp(-m_t))

        m_new = m_t[L - 1:L, :]
        b_last = b_col[L - 1:L, :]
        g_col = jnp.exp(b_last - b_col + i_col - m_new)
        decay = jnp.exp(b_last + m_old - m_new)
        kg = kf * g_col
        c_ref[0, h] = decay * c_old + lax.dot_general(kg.astype(BF16), v, _TN,
                                                      preferred_element_type=F32)
        n_ref[0, h:h + 1, :] = decay * n_old + jnp.sum(kg, axis=0, keepdims=True)
        m_ref[0, h:h + 1, :] = jnp.broadcast_to(m_new, (1, LANES))

        mu = jnp.mean(hh, axis=-1, keepdims=True)
        hc = hh - mu
        var = jnp.mean(hc * hc, axis=-1, keepdims=True)
        hn = hc * lax.rsqrt(var + HEAD_NORM_EPS) * g_ref[:, vs]
        ha_ref[:, vs] = (ao_ref[:, vs] * hn).astype(BF16)


def _mlstm(qkv, ao, gif, c0, n0, m0, norm_g, l, prev, *, bsz, t, row_off):
    chunk = min(MLSTM_CHUNK, t)
    nc = t // chunk
    off = row_off // chunk
    rows = lambda b, c: off + b * nc + c
    state4 = pl.BlockSpec((1, MLSTM_HEADS, MLSTM_DK, MLSTM_DV), lambda b, c: (b, 0, 0, 0))
    state3 = pl.BlockSpec((1, MLSTM_HEADS, LANES), lambda b, c: (b, 0, 0))
    init4 = pl.BlockSpec((None, 1, MLSTM_HEADS, MLSTM_DK, MLSTM_DV), lambda b, c: (l, b, 0, 0, 0))
    init3 = pl.BlockSpec((None, 1, MLSTM_HEADS, LANES), lambda b, c: (l, b, 0, 0))
    args = [qkv, qkv, qkv, ao, gif, c0, n0, m0, norm_g.reshape(1, MLSTM_W)]
    prev_args, prev_specs, aliases = _alias_prev(prev, len(args))
    return pl.pallas_call(
        _mlstm_kernel,
        out_shape=(jax.ShapeDtypeStruct((qkv.shape[0], MLSTM_W), BF16),
                   jax.ShapeDtypeStruct(c0.shape[1:], F32),
                   jax.ShapeDtypeStruct(n0.shape[1:], F32),
                   jax.ShapeDtypeStruct(m0.shape[1:], F32)),
        grid=(bsz, nc),
        in_specs=[
            pl.BlockSpec((chunk, MLSTM_QK_W), lambda b, c: (rows(b, c), 0)),
            pl.BlockSpec((chunk, MLSTM_QK_W), lambda b, c: (rows(b, c), 1)),
            pl.BlockSpec((chunk, MLSTM_W), lambda b, c: (rows(b, c), 1)),
            pl.BlockSpec((chunk, MLSTM_W), lambda b, c: (rows(b, c), 0)),
            pl.BlockSpec((chunk, GATE_PAD), lambda b, c: (rows(b, c), 0)),
            init4, init3, init3,
            pl.BlockSpec((1, MLSTM_W), lambda b, c: (0, 0)),
        ] + prev_specs,
        out_specs=(pl.BlockSpec((chunk, MLSTM_W), lambda b, c: (rows(b, c), 0)),
                   state4, state3, state3),
        input_output_aliases=aliases,
        compiler_params=_params("parallel", "arbitrary"),
        name="mlstm",
    )(*args, *prev_args)


SB_SCALE = SB_DH ** -0.5


def _lower_tri(n):
    return jnp.where(lax.broadcasted_iota(jnp.int32, (n, n), 0)
                     >= lax.broadcasted_iota(jnp.int32, (n, n), 1), 1.0, 0.0).astype(BF16)


def _sb_group(q_ref, load_k, load_v, heads, acc_ref, carry_ref, lower, *, diag):
    tq = q_ref.shape[0]
    tk = lower.shape[0]
    if diag:
        mask = (lax.broadcasted_iota(jnp.int32, (tq, tk), 1)
                < lax.broadcasted_iota(jnp.int32, (tq, tk), 0))
    zs, lks = [], []
    for h in heads:
        z = lax.dot_general(q_ref[:, h * SB_DH:(h + 1) * SB_DH], load_k(h), _NT,
                            preferred_element_type=F32) * SB_SCALE
        log_keep = -(jnp.maximum(z, 0.0) + jnp.log(1.0 + jnp.exp(-jnp.abs(z))))
        if diag:
            log_keep = jnp.where(mask, log_keep, 0.0)
        zs.append(z)
        lks.append(log_keep)
    stacked = jnp.concatenate(lks, axis=0)
    rc_all = sum(_dot(t, lower) for t in _split(stacked, 2))
    for i, h in enumerate(heads):
        rc = rc_all[i * tq:(i + 1) * tq]
        carry = carry_ref[h]
        attn = jnp.exp(zs[i] + rc + carry)
        if diag:
            attn = jnp.where(mask, attn, 0.0)
        acc_ref[h] += _dot(attn.astype(BF16), load_v(h))
        carry_ref[h] = carry + rc[:, 0:1]


def _sb_block(q_ref, k_ref, v_ref, base, tk, acc_ref, carry_ref, lower, *, group, diag):
    def loader(ref):
        return lambda h: ref[pl.ds(base + h, tk, stride=SB_HEADS), :].astype(BF16)
    for g in range(0, SB_HEADS, group):
        _sb_group(q_ref, loader(k_ref), loader(v_ref), range(g, g + group),
                  acc_ref, carry_ref, lower, diag=diag)


def _sb_store(o_ref, acc_ref):
    for h in range(SB_HEADS):
        o_ref[:, h * SB_DH:(h + 1) * SB_DH] = acc_ref[h].astype(BF16)


def _sb_prompt_kernel(q_ref, k_ref, v_ref, *rest, tq, group):
    o_ref, acc_ref, carry_ref = rest[-3:]
    qi = pl.program_id(1)
    acc_ref[...] = jnp.zeros_like(acc_ref)
    carry_ref[...] = jnp.zeros_like(carry_ref)
    lower = _lower_tri(tq)
    blk_rows = tq * SB_HEADS

    _sb_block(q_ref, k_ref, v_ref, pl.multiple_of(qi * blk_rows, blk_rows), tq,
              acc_ref, carry_ref, lower, group=group, diag=True)

    def body(j, c):
        base = pl.multiple_of((qi - 1 - j) * blk_rows, blk_rows)
        _sb_block(q_ref, k_ref, v_ref, base, tq, acc_ref, carry_ref, lower,
                  group=group, diag=False)
        return c

    lax.fori_loop(0, qi, body, 0)
    _sb_store(o_ref, acc_ref)


def _sb_sample_kernel(q_ref, k_ref, v_ref, kp_ref, vp_ref, *rest, tq, n_past):
    o_ref, acc_ref, carry_ref = rest[-3:]
    s = pl.program_id(1)

    @pl.when(s == 0)
    def _():
        acc_ref[...] = jnp.zeros_like(acc_ref)
        carry_ref[...] = jnp.zeros_like(carry_ref)
        _sb_block(q_ref, k_ref, v_ref, 0, tq, acc_ref, carry_ref, _lower_tri(tq),
                  group=SB_HEADS, diag=True)

    @pl.when(s > 0)
    def _():
        _sb_block(q_ref, kp_ref, vp_ref, 0, SB_TQ, acc_ref, carry_ref, _lower_tri(SB_TQ),
                  group=SB_HEADS, diag=False)

    @pl.when(s == n_past)
    def _():
        _sb_store(o_ref, acc_ref)


def _stick_breaking(bq, k8, v8, past_k, past_v, l, prev, *, bsz, t, row_off):
    tq = min(SB_TQ, t)
    nq = t // tq
    off_q = row_off // tq
    scratch = [pltpu.VMEM((SB_HEADS, tq, SB_DH), F32), pltpu.VMEM((SB_HEADS, tq, 1), F32)]
    out_shape = jax.ShapeDtypeStruct((bq.shape[0], SB_W), BF16)
    q_spec = pl.BlockSpec((tq, SB_W), lambda b, i: (off_q + b * nq + i, 0))
    if past_k is None:
        seq_spec = pl.BlockSpec((None, t * SB_HEADS, SB_DH), lambda b, i: (l, b, 0))
        args = [bq, k8, v8]
        prev_args, prev_specs, aliases = _alias_prev(prev, len(args))
        return pl.pallas_call(
            functools.partial(_sb_prompt_kernel, tq=tq, group=SB_HEADS // 2),
            out_shape=out_shape,
            grid=(bsz, nq),
            in_specs=[q_spec, seq_spec, seq_spec] + prev_specs,
            out_specs=q_spec,
            scratch_shapes=scratch,
            input_output_aliases=aliases,
            compiler_params=_params("parallel", "arbitrary"),
            name="stick_breaking_prompt",
        )(*args, *prev_args)
    assert nq == 1
    blk = SB_TQ * SB_HEADS
    n_past = past_k.shape[2] // blk
    q_spec = pl.BlockSpec((tq, SB_W), lambda b, s: (off_q + b, 0))
    new_spec = pl.BlockSpec((None, t * SB_HEADS, SB_DH), lambda b, s: (l, b, 0))
    past_spec = pl.BlockSpec((None, None, blk, SB_DH),
                             lambda b, s: (l, b, n_past - jnp.maximum(s, 1), 0))
    args = [bq, k8, v8, past_k, past_v]
    prev_args, prev_specs, aliases = _alias_prev(prev, len(args))
    return pl.pallas_call(
        functools.partial(_sb_sample_kernel, tq=tq, n_past=n_past),
        out_shape=out_shape,
        grid=(bsz, n_past + 1),
        in_specs=[q_spec, new_spec, new_spec, past_spec, past_spec] + prev_specs,
        out_specs=q_spec,
        scratch_shapes=scratch,
        input_output_aliases=aliases,
        compiler_params=_params("parallel", "arbitrary"),
        name="stick_breaking_sample",
    )(*args, *prev_args)


def _pool_kernel(u_ref, buf_ref, w_ref, s_ref, *rest, tt, pos0):
    o_ref, ext_ref = rest[-2:]
    i = pl.program_id(1)

    @pl.when(i == 0)
    def _():
        ext_ref[0:POOL_HALO, :] = buf_ref[0]

    @pl.when(i > 0)
    def _():
        ext_ref[0:POOL_HALO, :] = ext_ref[tt:tt + POOL_HALO, :]

    ext_ref[POOL_HALO:POOL_HALO + tt, :] = u_ref[...]
    pos = pos0 + i * tt + lax.broadcasted_iota(jnp.int32, (tt, 1), 0)
    for gi, w in enumerate(POOL_WINDOWS):
        cs = slice(gi * POOL_G, (gi + 1) * POOL_G)
        cur = ext_ref[POOL_HALO:POOL_HALO + tt, cs]
        wsum = cur
        for j in range(1, w):
            wsum = wsum + ext_ref[POOL_HALO - j:POOL_HALO - j + tt, cs]
        cnt = jnp.minimum(pos + 1, w).astype(F32)
        d = (wsum / cnt - cur).astype(BF16)
        y = _dot(d, w_ref[gi]) * s_ref[:, cs]
        o_ref[:, cs] = y.astype(BF16)


def _pool(u, buf, w_pool, scale, l, prev, *, bsz, t, row_off, pos0):
    tt = min(256, t)
    nt = t // tt
    off = row_off // tt
    row_spec = pl.BlockSpec((tt, POOL_W), lambda b, i: (off + b * nt + i, 0))
    args = [u, buf, w_pool, scale.reshape(1, POOL_W)]
    prev_args, prev_specs, aliases = _alias_prev(prev, len(args))
    return pl.pallas_call(
        functools.partial(_pool_kernel, tt=tt, pos0=pos0),
        out_shape=jax.ShapeDtypeStruct((u.shape[0], POOL_W), BF16),
        grid=(bsz, nt),
        in_specs=[
            row_spec,
            pl.BlockSpec((1, POOL_HALO, POOL_W), lambda b, i: (b, 0, 0)),
            pl.BlockSpec((None, len(POOL_WINDOWS), POOL_G, POOL_G), lambda b, i: (l, 0, 0, 0)),
            pl.BlockSpec((1, POOL_W), lambda b, i: (0, 0)),
        ] + prev_specs,
        out_specs=row_spec,
        scratch_shapes=[pltpu.VMEM((POOL_HALO + tt, POOL_W), F32)],
        input_output_aliases=aliases,
        compiler_params=_params("parallel", "arbitrary"),
        name="pool_mix",
    )(*args, *prev_args)


X_SCALE = X_DH ** -0.5


def _cross_kernel(q_ref, k_ref, v_ref, *rest):
    o_ref = rest[-1]
    for h in range(X_HEADS):
        cs = slice(h * X_DH, (h + 1) * X_DH)
        k = k_ref[0, :, cs].astype(BF16)
        v = v_ref[0, :, cs].astype(BF16)
        s = lax.dot_general(q_ref[:, cs], k, _NT, preferred_element_type=F32) * X_SCALE
        e = jnp.exp(s - jnp.max(s, axis=-1, keepdims=True))
        p = e / jnp.sum(e, axis=-1, keepdims=True)
        o_ref[:, cs] = _dot(p.astype(BF16), v).astype(BF16)


def _cross_attend(qx, mk, mv, l, prev, *, bsz, t, row_off):
    tq = min(256, t)
    nq = t // tq
    off = row_off // tq
    row_spec = pl.BlockSpec((tq, X_W), lambda b, i: (off + b * nq + i, 0))
    mem_spec = pl.BlockSpec((None, 1, mk.shape[2], X_W), lambda b, i: (l, b, 0, 0))
    args = [qx, mk, mv]
    prev_args, prev_specs, aliases = _alias_prev(prev, len(args))
    return pl.pallas_call(
        _cross_kernel,
        out_shape=jax.ShapeDtypeStruct((qx.shape[0], X_W), BF16),
        grid=(bsz, nq),
        in_specs=[row_spec, mem_spec, mem_spec] + prev_specs,
        out_specs=row_spec,
        input_output_aliases=aliases,
        compiler_params=_params("parallel", "arbitrary"),
        name="cross_attend",
    )(*args, *prev_args)


def _merge_kernel(x_ref, wg0_ref, wg1_ref, wg2_ref, bg0_ref, bg1_ref, bg2_ref,
                  ha_ref, hb_ref, hc_ref, wa_ref, wb_ref, wc_ref, o_ref):
    x = x_ref[...]
    out = None
    for wg_ref, bg_ref, h_ref, wp_ref in ((wg0_ref, bg0_ref, ha_ref, wa_ref),
                                          (wg1_ref, bg1_ref, hb_ref, wb_ref),
                                          (wg2_ref, bg2_ref, hc_ref, wc_ref)):
        gate = jax.nn.sigmoid(_dot(x, wg_ref[...]) + bg_ref[...])
        term = gate * _dot(h_ref[...], wp_ref[...])
        out = term if out is None else out + term
    o_ref[...] = out.astype(BF16)


def _merge(xb, w_b, bg, ha, hb, hc, w_pa, w_pb, w_pc, l, *, bm=512, bn=256):
    m = xb.shape[0]
    nb = D_MODEL // bn
    gb = B_G // bn
    row = lambda width: pl.BlockSpec((bm, width), lambda i, j: (i, 0))
    colw = lambda kdim: pl.BlockSpec((None, kdim, bn), lambda i, j: (l, 0, j))
    gate_w = [pl.BlockSpec((None, D_MODEL, bn), lambda i, j, g=g: (l, 0, gb + g * nb + j))
              for g in range(N_BRANCH)]
    gate_b = [pl.BlockSpec((1, bn), lambda i, j, g=g: (0, g * nb + j)) for g in range(N_BRANCH)]
    return pl.pallas_call(
        _merge_kernel,
        out_shape=jax.ShapeDtypeStruct((m, D_MODEL), BF16),
        grid=(m // bm, nb),
        in_specs=[row(D_MODEL)] + gate_w + gate_b
                 + [row(MLSTM_W), row(SB_W), row(POOL_W), colw(MLSTM_W), colw(SB_W), colw(POOL_W)],
        out_specs=pl.BlockSpec((bm, bn), lambda i, j: (i, j)),
        compiler_params=_params("parallel", "arbitrary"),
        name="gated_merge",
    )(xb, w_b, w_b, w_b, bg, bg, bg, ha, hb, hc, w_pa, w_pb, w_pc)


def _trunk_layer(x, xb, W, l, groups, kv_prev, last):
    b_in = W["b_in"][l]
    proj_a = functools.partial(_matmul, xb, W["w_a"], l)
    proj_b = functools.partial(_matmul, xb, W["w_b"], l)
    group_rows = tuple(g["bsz"] * g["t"] for g in groups)

    k_scale = jnp.concatenate([jnp.ones((MLSTM_QK_W,), F32),
                               jnp.full((MLSTM_QK_W,), MLSTM_DK ** -0.5, F32),
                               jnp.ones((MLSTM_W,), F32)])
    qkv = proj_a(n=OFF_AO, bias=b_in[:OFF_AO], scale=k_scale, out_dtype=BF16, name="proj_qkv")
    ao = proj_a(n=MLSTM_W, col_off=OFF_AO, bias=b_in[OFF_AO:OFF_AI], act="sigmoid",
                name="proj_ogate")
    b_if = jnp.pad(b_in[OFF_AI:OFF_BQ], (0, GATE_PAD - 2 * MLSTM_HEADS))
    gif = _matmul(xb, W["w_if"], l, n=GATE_PAD, bias=b_if, name="proj_if")
    sec = lambda i: b_in[OFF_BQ + i * SB_W:OFF_BQ + (i + 1) * SB_W]
    bq = proj_b(n=SB_W, bias=sec(0), out_dtype=BF16, name="proj_sb_q")
    k8 = _proj_heads(xb, W["w_b"], l, sec(1), B_BK, group_rows, kv_prev and kv_prev[0],
                     name="proj_sb_k")
    v8 = _proj_heads(xb, W["w_b"], l, sec(2), B_BV, group_rows, kv_prev and kv_prev[1],
                     name="proj_sb_v")
    u = proj_b(n=POOL_W, col_off=B_CU, bias=sec(3), name="proj_pool_u")

    ha = hb = hc = None
    states = []
    for gi, g in enumerate(groups):
        dims = dict(bsz=g["bsz"], t=g["t"], row_off=g["row_off"])
        ha, c_new, n_new, m_new = _mlstm(qkv, ao, gif, g["c"], g["n"], g["m"],
                                         W["mlstm_norm_g"][l], l, ha, **dims)
        hb = _stick_breaking(bq, k8[gi], v8[gi], g["sb_k"], g["sb_v"], l, hb, **dims)
        hc = _pool(u, g["pool_buf"][l], W["w_pool"], W["pool_scale"][l], l, hc,
                   pos0=g["pos0"], **dims)
        states.append((c_new, n_new, m_new))

    merged = _merge(xb, W["w_b"], b_in[OFF_G:].reshape(1, -1), ha, hb, hc,
                    W["w_pa"], W["w_pb"], W["w_pc"], l)
    y = _matmul(merged, W["w_out"], l, n=D_MODEL, bn=512, name="w_out")
    x, xb = _layer_norm(x, W["ln1_g"][l], W["ln1_b"][l], res=y)

    qx = _matmul(xb, W["wq_x"], l, n=X_W, out_dtype=BF16, name="wq_x")
    ox = None
    for g in groups:
        ox = _cross_attend(qx, g["mk"], g["mv"], l, ox, bsz=g["bsz"], t=g["t"],
                           row_off=g["row_off"])
    y = _matmul(ox, W["wo_x"], l, n=D_MODEL, name="wo_x")
    x, xb = _layer_norm(x, W["ln2_g"][l], W["ln2_b"][l], res=y)

    hid = _matmul(xb, W["w_up"], l, n=D_FF, act="relu2", out_dtype=BF16, bn=512, name="w_up")
    y = _matmul(hid, W["w_down"], l, n=D_MODEL, bk=D_MODEL, name="w_down")
    if last:
        x = tuple(_layer_norm(x, W["ln3_g"][l], W["ln3_b"][l], res=y, in_off=g["row_off"],
                              n_rows=g["bsz"] * g["t"], emit_bf16=False) for g in groups)
        xb = None
    else:
        x, xb = _layer_norm(x, W["ln3_g"][l], W["ln3_b"][l], res=y)
    return x, xb, (k8, v8), u, states


def kernel(x_prompt, x_sample, cache_sb_k, cache_sb_v, state_mlstm_c, state_mlstm_n, state_mlstm_m, state_pool, cache_mem_k, cache_mem_v, mem_prompt, ln_in_g, ln_in_b, w_in, b_in, mlstm_norm_g, w_pool, pool_scale, w_pa, w_pb, w_pc, w_out, ln1_g, ln1_b, wq_x, wk_x, wv_x, wo_x, ln2_g, ln2_b, w_up, w_down, ln3_g, ln3_b):
    bf = lambda a: a.astype(BF16)
    W = dict(w_a=bf(w_in[:, :, :OFF_AI]), w_b=bf(w_in[:, :, OFF_BQ:]),
             w_if=jnp.pad(bf(w_in[:, :, OFF_AI:OFF_BQ]),
                          ((0, 0), (0, 0), (0, GATE_PAD - 2 * MLSTM_HEADS))),
             b_in=b_in, mlstm_norm_g=mlstm_norm_g, w_pool=bf(w_pool),
             pool_scale=pool_scale, w_pa=bf(w_pa), w_pb=bf(w_pb), w_pc=bf(w_pc), w_out=w_out,
             ln1_g=ln1_g, ln1_b=ln1_b, wq_x=bf(wq_x), wo_x=bf(wo_x), ln2_g=ln2_g, ln2_b=ln2_b,
             w_up=w_up, w_down=bf(w_down), ln3_g=ln3_g, ln3_b=ln3_b)
    bp, tp, _ = x_prompt.shape
    bs, ts, _ = x_sample.shape
    rows_p, rows_s = bp * tp, bs * ts
    rows = rows_p + rows_s
    past = cache_sb_k.shape[2]
    mem_len = mem_prompt.shape[1]

    mem_b = mem_prompt.reshape(bp * mem_len, D_MODEL).astype(BF16)
    wk_b, wv_b = bf(wk_x), bf(wv_x)
    mk_p = jnp.stack([_matmul(mem_b, wk_b, l, n=X_W, name="mem_k") for l in range(DEPTH)])
    mv_p = jnp.stack([_matmul(mem_b, wv_b, l, n=X_W, name="mem_v") for l in range(DEPTH)])

    lane_m = lambda m: jnp.broadcast_to(m[..., None], m.shape + (LANES,))
    halo = lambda b: jnp.pad(b, ((0, 0), (0, 0), (POOL_HALO - POOL_BUF, 0), (0, 0)))
    heads_rows = lambda c: c.reshape(DEPTH, bs, past * SB_HEADS, SB_DH)
    groups = [
        dict(bsz=bp, t=tp, row_off=0, pos0=0,
             c=jnp.zeros((DEPTH, bp, MLSTM_HEADS, MLSTM_DK, MLSTM_DV), F32),
             n=jnp.zeros((DEPTH, bp, MLSTM_HEADS, MLSTM_DK), F32),
             m=jnp.zeros((DEPTH, bp, MLSTM_HEADS, LANES), F32),
             sb_k=None, sb_v=None,
             pool_buf=jnp.zeros((DEPTH, bp, POOL_HALO, POOL_W), F32),
             mk=mk_p.reshape(DEPTH, bp, mem_len, X_W),
             mv=mv_p.reshape(DEPTH, bp, mem_len, X_W)),
        dict(bsz=bs, t=ts, row_off=rows_p, pos0=past,
             c=state_mlstm_c, n=state_mlstm_n, m=lane_m(state_mlstm_m),
             sb_k=heads_rows(cache_sb_k), sb_v=heads_rows(cache_sb_v),
             pool_buf=halo(state_pool),
             mk=cache_mem_k.reshape(DEPTH, bs, mem_len, X_W),
             mv=cache_mem_v.reshape(DEPTH, bs, mem_len, X_W)),
    ]

    xs = None
    for g, xin in zip(groups, (x_prompt, x_sample)):
        xs = _layer_norm(xin.reshape(g["bsz"] * g["t"], D_MODEL), ln_in_g, ln_in_b,
                         out_off=g["row_off"], out_rows=rows, prev=xs)
    x, xb = xs

    kv = None
    per_layer = []
    for l in range(DEPTH):
        x, xb, kv, u, states = _trunk_layer(x, xb, W, l, groups, kv, last=l == DEPTH - 1)
        per_layer.append((u, states))

    def group_out(gi, g):
        bsz, t, r0 = g["bsz"], g["t"], g["row_off"]
        sbk = kv[0][gi].reshape(DEPTH, bsz, t, SB_HEADS, SB_DH)
        sbv = kv[1][gi].reshape(DEPTH, bsz, t, SB_HEADS, SB_DH)
        c = jnp.stack([st[gi][0] for _, st in per_layer])
        n = jnp.stack([st[gi][1] for _, st in per_layer])
        m = jnp.stack([st[gi][2][..., 0] for _, st in per_layer])
        pool = jnp.stack([u[r0:r0 + bsz * t].reshape(bsz, t, POOL_W)[:, t - POOL_BUF:, :]
                          for u, _ in per_layer])
        return sbk, sbv, c, n, m, pool

    out_p = group_out(0, groups[0])
    out_s = group_out(1, groups[1])
    mem_k_p = mk_p.reshape(DEPTH, bp, mem_len, X_HEADS, X_DH)
    mem_v_p = mv_p.reshape(DEPTH, bp, mem_len, X_HEADS, X_DH)
    y_prompt = x[0].reshape(bp, tp, D_MODEL)
    y_sample = x[1].reshape(bs, ts, D_MODEL)
    return (y_prompt, y_sample) + out_p + (mem_k_p, mem_v_p) + out_s
```

```python
import functools

import jax
import jax.numpy as jnp
from jax import lax
from jax.experimental import pallas as pl
from jax.experimental.pallas import tpu as pltpu

F32 = jnp.float32
BF16 = jnp.bfloat16

D_MODEL = 4096
DEPTH = 2
MLSTM_CHUNK = 256
MLSTM_HEADS = 8
MLSTM_W = D_MODEL // 2
MLSTM_DV = MLSTM_W // MLSTM_HEADS
MLSTM_DK = MLSTM_DV // 2
MLSTM_QK_W = MLSTM_HEADS * MLSTM_DK
SB_HEADS = 8
SB_W = D_MODEL // 4
SB_DH = SB_W // SB_HEADS
POOL_WINDOWS = (2, 4, 8, 16)
POOL_W = D_MODEL // 4
POOL_G = POOL_W // len(POOL_WINDOWS)
POOL_BUF = max(POOL_WINDOWS) - 1
POOL_HALO = POOL_BUF + 1
X_HEADS = 4
X_W = D_MODEL // 4
X_DH = X_W // X_HEADS
D_FF = 4 * D_MODEL
N_BRANCH = 3
ALPHA = (2 * DEPTH) ** 0.25
LN_EPS = 1e-5
HEAD_NORM_EPS = 1e-6

OFF_AO = 2 * MLSTM_QK_W + MLSTM_W
OFF_AI = OFF_AO + MLSTM_W
OFF_BQ = OFF_AI + 2 * MLSTM_HEADS
OFF_G = OFF_BQ + 3 * SB_W + POOL_W
IN_W = OFF_G + N_BRANCH * D_MODEL

V7X_VMEM_LIMIT_BYTES = 56 * 1024 * 1024
LANES = 128
SUBLANES = 8
SB_TQ = 256
GATE_PAD = LANES

B_BK = SB_W
B_BV = 2 * SB_W
B_CU = 3 * SB_W
B_G = 3 * SB_W + POOL_W

_NT = (((1,), (1,)), ((), ()))
_TN = (((0,), (0,)), ((), ()))


def _params(*sem):
    return pltpu.CompilerParams(dimension_semantics=sem,
                                vmem_limit_bytes=V7X_VMEM_LIMIT_BYTES)


def _dot(a, b):
    return jnp.dot(a, b, preferred_element_type=F32)


def _split(x, terms):
    out = []
    for _ in range(terms - 1):
        h = x.astype(BF16)
        out.append(h)
        x = x - h.astype(F32)
    out.append(x.astype(BF16))
    return out


def _alias_prev(prev, n_in):
    if prev is None:
        return [], [], {}
    prev = list(prev) if isinstance(prev, (tuple, list)) else [prev]
    specs = [pl.BlockSpec(memory_space=pl.ANY)] * len(prev)
    return prev, specs, {n_in + i: i for i in range(len(prev))}


def _mm_kernel(*refs, act, has_bias, has_scale, has_res, nk, side_slabs, wt):
    it = iter(refs)
    x_ref, w_ref = next(it), next(it)
    b_ref = next(it) if has_bias else None
    s_ref = next(it) if has_scale else None
    r_ref = next(it) if has_res else None
    side_in_ref = next(it) if side_slabs else None
    o_ref = next(it)
    side_out_ref = next(it) if side_slabs else None
    acc_ref = next(it) if nk > 1 else None

    if side_slabs:
        step = pl.program_id(0) * pl.num_programs(1) + pl.program_id(1)

        @pl.when(step < side_slabs)
        def _():
            side_out_ref[...] = side_in_ref[...].astype(BF16)

    def epilogue(y):
        if has_bias:
            y = y + b_ref[...]
        if has_scale:
            y = y * s_ref[...]
        if act == "sigmoid":
            y = jax.nn.sigmoid(y)
        elif act == "relu2":
            y = jnp.square(jnp.maximum(y, 0.0))
        if has_res:
            y = ALPHA * r_ref[...] + y
        o_ref[...] = y.astype(o_ref.dtype)

    w = w_ref[...].astype(BF16)
    part = lax.dot_general(x_ref[...], w, _NT if wt else (((1,), (0,)), ((), ())),
                           preferred_element_type=F32)
    if nk == 1:
        epilogue(part)
    else:
        k = pl.program_id(2)

        @pl.when(k == 0)
        def _():
            acc_ref[...] = part

        @pl.when(k > 0)
        def _():
            acc_ref[...] += part

        @pl.when(k == nk - 1)
        def _():
            epilogue(acc_ref[...])


SIDE_SLAB_ROWS = 64


def _matmul(x, w, l, *, n, col_off=0, bias=None, scale=None, res=None, act="none",
            out_dtype=F32, bm=1024, bn=1024, bk=None, side_cast=None, wt=False, name="matmul"):
    m, kdim = x.shape
    bm, bn = min(bm, m), min(bn, n)
    bk = kdim if bk is None else bk
    assert m % bm == 0 and n % bn == 0 and kdim % bk == 0 and col_off % bn == 0
    nk = kdim // bk
    cb = col_off // bn
    nj = n // bn
    w_spec = (pl.BlockSpec((None, bn, bk), lambda i, j, k: (l, cb + j, k)) if wt
              else pl.BlockSpec((None, bk, bn), lambda i, j, k: (l, k, cb + j)))
    in_specs = [pl.BlockSpec((bm, bk), lambda i, j, k: (i, k)), w_spec]
    args = [x, w]
    for v in (bias, scale):
        if v is not None:
            in_specs.append(pl.BlockSpec((1, bn), lambda i, j, k: (0, j)))
            args.append(v.reshape(1, n).astype(F32))
    if res is not None:
        in_specs.append(pl.BlockSpec((bm, bn), lambda i, j, k: (i, j)))
        args.append(res)
    out_shape = [jax.ShapeDtypeStruct((m, n), out_dtype)]
    out_specs = [pl.BlockSpec((bm, bn), lambda i, j, k: (i, j))]
    side_slabs = 0
    if side_cast is not None:
        _, sr, sc = side_cast.shape
        side_slabs = sr // SIDE_SLAB_ROWS
        assert nk == 1 and sr % SIDE_SLAB_ROWS == 0 and side_slabs <= (m // bm) * nj
        slab = lambda i, j, k: jnp.minimum(i * nj + j, side_slabs - 1)
        in_specs.append(pl.BlockSpec((None, SIDE_SLAB_ROWS, sc), lambda i, j, k: (l, slab(i, j, k), 0)))
        args.append(side_cast)
        out_shape.append(jax.ShapeDtypeStruct((1, sr, sc), BF16))
        out_specs.append(pl.BlockSpec((None, SIDE_SLAB_ROWS, sc), lambda i, j, k: (0, slab(i, j, k), 0)))
    kern = functools.partial(_mm_kernel, act=act, has_bias=bias is not None,
                             has_scale=scale is not None, has_res=res is not None, nk=nk,
                             side_slabs=side_slabs, wt=wt)
    out = pl.pallas_call(
        kern,
        out_shape=tuple(out_shape),
        grid=(m // bm, nj, nk),
        in_specs=in_specs,
        out_specs=tuple(out_specs),
        scratch_shapes=[pltpu.VMEM((bm, bn), F32)] if nk > 1 else [],
        compiler_params=(_params("arbitrary", "arbitrary", "arbitrary") if side_slabs
                         else _params("parallel", "parallel", "arbitrary")),
        name=name,
    )(*args)
    return out if side_slabs else out[0]


def _proj_heads_kernel(x_ref, w_ref, b_ref, *rest, n_first):
    first_ref, second_ref = rest[-2:]
    y = lax.dot_general(x_ref[...], w_ref[...], _NT, preferred_element_type=F32) + b_ref[...]
    rows = y.shape[0]

    def store(o_ref):
        for h in range(SUBLANES):
            o_ref[pl.ds(h, rows, stride=SUBLANES), :] = y[:, h * LANES:(h + 1) * LANES]

    i = pl.program_id(0)

    @pl.when(i < n_first)
    def _():
        store(first_ref)

    @pl.when(i >= n_first)
    def _():
        store(second_ref)


def _proj_heads(x, w, l, bias, col_off, group_rows, prev, *, bm=1024, name="proj_heads"):
    m, kdim = x.shape
    n = SUBLANES * LANES
    rows_a, rows_b = group_rows
    assert rows_a % bm == 0 and rows_b % bm == 0 and rows_a + rows_b == m and col_off % n == 0
    n_first = rows_a // bm
    cb = col_off // n
    args = [x, w, bias.reshape(1, n)]
    prev_args, prev_specs, aliases = _alias_prev(prev, len(args))
    blk = (None, bm * SUBLANES, LANES)
    return pl.pallas_call(
        functools.partial(_proj_heads_kernel, n_first=n_first),
        out_shape=tuple(jax.ShapeDtypeStruct((DEPTH, r * SUBLANES, LANES), F32) for r in group_rows),
        grid=(m // bm,),
        in_specs=[pl.BlockSpec((bm, kdim), lambda i: (i, 0)),
                  pl.BlockSpec((None, n, kdim), lambda i: (l, cb, 0)),
                  pl.BlockSpec((1, n), lambda i: (0, 0))] + prev_specs,
        out_specs=(pl.BlockSpec(blk, lambda i: (l, jnp.minimum(i, n_first - 1), 0)),
                   pl.BlockSpec(blk, lambda i: (l, jnp.maximum(i - n_first, 0), 0))),
        input_output_aliases=aliases,
        compiler_params=_params("arbitrary"),
        name=name,
    )(*args, *prev_args)


def _ln_kernel(*refs, has_res, n_prev, emit_bf16):
    it = iter(refs)
    x_ref = next(it)
    r_ref = next(it) if has_res else None
    g_ref, b_ref = next(it), next(it)
    for _ in range(n_prev):
        next(it)
    o_ref = next(it)
    x = x_ref[...]
    if has_res:
        x = ALPHA * x + r_ref[...]
    mu = jnp.mean(x, axis=-1, keepdims=True)
    xc = x - mu
    var = jnp.mean(xc * xc, axis=-1, keepdims=True)
    y = xc * lax.rsqrt(var + LN_EPS) * g_ref[...] + b_ref[...]
    o_ref[...] = y
    if emit_bf16:
        next(it)[...] = y.astype(BF16)


def _layer_norm(x, g, b, res=None, *, in_off=0, n_rows=None, out_off=0, out_rows=None,
                prev=None, emit_bf16=True, rows=256):
    d = x.shape[1]
    n_rows = x.shape[0] - in_off if n_rows is None else n_rows
    out_rows = n_rows if out_rows is None else out_rows
    ib, ob = in_off // rows, out_off // rows
    in_spec = pl.BlockSpec((rows, d), lambda i: (ib + i, 0))
    out_spec = pl.BlockSpec((rows, d), lambda i: (ob + i, 0))
    vec_spec = pl.BlockSpec((1, d), lambda i: (0, 0))
    args = [x] + ([res] if res is not None else []) + [g.reshape(1, d), b.reshape(1, d)]
    in_specs = [in_spec] * (2 if res is not None else 1) + [vec_spec, vec_spec]
    prev_args, prev_specs, aliases = _alias_prev(prev, len(args))
    out_shape = [jax.ShapeDtypeStruct((out_rows, d), F32)]
    if emit_bf16:
        out_shape.append(jax.ShapeDtypeStruct((out_rows, d), BF16))
    out = pl.pallas_call(
        functools.partial(_ln_kernel, has_res=res is not None, n_prev=len(prev_args),
                          emit_bf16=emit_bf16),
        out_shape=tuple(out_shape),
        grid=(n_rows // rows,),
        in_specs=in_specs + prev_specs,
        out_specs=tuple([out_spec] * len(out_shape)),
        input_output_aliases=aliases,
        compiler_params=_params("parallel"),
        name="layer_norm",
    )(*args, *prev_args)
    return out if emit_bf16 else out[0]


def _mlstm_kernel(q_ref, k_ref, v_ref, ao_ref, gif_ref, c0_ref, n0_ref, m0_ref, g_ref, *rest):
    ha_ref, c_ref, n_ref, m_ref = rest[-4:]
    L = q_ref.shape[0]

    @pl.when(pl.program_id(1) == 0)
    def _():
        c_ref[...] = c0_ref[...]
        n_ref[...] = n0_ref[...]
        m_ref[...] = m0_ref[...]

    gif = gif_ref[...]
    logf = -(jnp.maximum(-gif, 0.0) + jnp.log1p(jnp.exp(-jnp.abs(gif))))
    row = lax.broadcasted_iota(jnp.int32, (L, L), 0)
    col = lax.broadcasted_iota(jnp.int32, (L, L), 1)
    causal = col <= row
    tril = jnp.where(causal, 1.0, 0.0).astype(BF16)
    b_all = sum(_dot(tril, t) for t in _split(logf, 3))
    sel_r = lax.broadcasted_iota(jnp.int32, (2 * MLSTM_HEADS, GATE_PAD), 0)
    sel_c = lax.broadcasted_iota(jnp.int32, (2 * MLSTM_HEADS, GATE_PAD), 1)
    sel = jnp.where(sel_r == sel_c, 1.0, 0.0).astype(BF16)

    def rows_of(x):
        return sum(lax.dot_general(sel, t, _NT, preferred_element_type=F32) for t in _split(x, 3))

    i_rows = rows_of(gif)
    b_rows = rows_of(b_all)

    for h in range(MLSTM_HEADS):
        qs = slice(h * MLSTM_DK, (h + 1) * MLSTM_DK)
        vs = slice(h * MLSTM_DV, (h + 1) * MLSTM_DV)
        q = q_ref[:, qs]
        k = k_ref[:, qs]
        v = v_ref[:, vs]
        i_col = gif[:, h:h + 1]
        b_col = b_all[:, MLSTM_HEADS + h:MLSTM_HEADS + h + 1]
        i_row = i_rows[h:h + 1, :]
        b_row = b_rows[MLSTM_HEADS + h:MLSTM_HEADS + h + 1, :]
        c_old = c_ref[0, h]
        n_old = n_ref[0, h:h + 1, :]
        m_old = m_ref[0, h:h + 1, 0:1]

        dmat = jnp.where(causal, b_col - b_row + i_row, -jnp.inf)
        m_inter = b_col + m_old
        m_t = jnp.maximum(m_inter, jnp.max(dmat, axis=-1, keepdims=True))
        s = lax.dot_general(q, k, _NT, preferred_element_type=F32) * jnp.exp(dmat - m_t)
        w_inter = jnp.exp(m_inter - m_t)
        kf = k.astype(F32)
        num = _dot(s.astype(BF16), v) + w_inter * _dot(q, c_old.astype(BF16))
        qn = jnp.sum(q.astype(F32) * n_old, axis=-1, keepdims=True)
        den = jnp.sum(s, axis=-1, keepdims=True) + w_inter * qn
        hh = num / jnp.maximum(jnp.abs(den), jnp.exp(-m_t))

        m_new = m_t[L - 1:L, :]
        b_last = b_col[L - 1:L, :]
        g_col = jnp.exp(b_last - b_col + i_col - m_new)
        decay = jnp.exp(b_last + m_old - m_new)
        kg = kf * g_col
        c_ref[0, h] = decay * c_old + lax.dot_general(kg.astype(BF16), v, _TN,
                                                      preferred_element_type=F32)
        n_ref[0, h:h + 1, :] = decay * n_old + jnp.sum(kg, axis=0, keepdims=True)
        m_ref[0, h:h + 1, :] = jnp.broadcast_to(m_new, (1, LANES))

        mu = jnp.mean(hh, axis=-1, keepdims=True)
        hc = hh - mu
        var = jnp.mean(hc * hc, axis=-1, keepdims=True)
        hn = hc * lax.rsqrt(var + HEAD_NORM_EPS) * g_ref[:, vs]
        ha_ref[:, vs] = (ao_ref[:, vs] * hn).astype(BF16)


def _mlstm(qkv, ao, gif, c0, n0, m0, norm_g, l, prev, *, bsz, t, row_off):
    chunk = min(MLSTM_CHUNK, t)
    nc = t // chunk
    off = row_off // chunk
    rows = lambda b, c: off + b * nc + c
    state4 = pl.BlockSpec((1, MLSTM_HEADS, MLSTM_DK, MLSTM_DV), lambda b, c: (b, 0, 0, 0))
    state3 = pl.BlockSpec((1, MLSTM_HEADS, LANES), lambda b, c: (b, 0, 0))
    init4 = pl.BlockSpec((None, 1, MLSTM_HEADS, MLSTM_DK, MLSTM_DV), lambda b, c: (l, b, 0, 0, 0))
    init3 = pl.BlockSpec((None, 1, MLSTM_HEADS, LANES), lambda b, c: (l, b, 0, 0))
    args = [qkv, qkv, qkv, ao, gif, c0, n0, m0, norm_g.reshape(1, MLSTM_W)]
    prev_args, prev_specs, aliases = _alias_prev(prev, len(args))
    return pl.pallas_call(
        _mlstm_kernel,
        out_shape=(jax.ShapeDtypeStruct((qkv.shape[0], MLSTM_W), BF16),
                   jax.ShapeDtypeStruct(c0.shape[1:], F32),
                   jax.ShapeDtypeStruct(n0.shape[1:], F32),
                   jax.ShapeDtypeStruct(m0.shape[1:], F32)),
        grid=(bsz, nc),
        in_specs=[
            pl.BlockSpec((chunk, MLSTM_QK_W), lambda b, c: (rows(b, c), 0)),
            pl.BlockSpec((chunk, MLSTM_QK_W), lambda b, c: (rows(b, c), 1)),
            pl.BlockSpec((chunk, MLSTM_W), lambda b, c: (rows(b, c), 1)),
            pl.BlockSpec((chunk, MLSTM_W), lambda b, c: (rows(b, c), 0)),
            pl.BlockSpec((chunk, GATE_PAD), lambda b, c: (rows(b, c), 0)),
            init4, init3, init3,
            pl.BlockSpec((1, MLSTM_W), lambda b, c: (0, 0)),
        ] + prev_specs,
        out_specs=(pl.BlockSpec((chunk, MLSTM_W), lambda b, c: (rows(b, c), 0)),
                   state4, state3, state3),
        input_output_aliases=aliases,
        compiler_params=_params("parallel", "arbitrary"),
        name="mlstm",
    )(*args, *prev_args)


SB_SCALE = SB_DH ** -0.5


def _lower_tri(n):
    return jnp.where(lax.broadcasted_iota(jnp.int32, (n, n), 0)
                     >= lax.broadcasted_iota(jnp.int32, (n, n), 1), 1.0, 0.0).astype(BF16)


def _sb_group(q_ref, load_k, load_v, heads, acc_ref, carry_ref, lower, *, diag):
    tq = q_ref.shape[0]
    tk = lower.shape[0]
    if diag:
        mask = (lax.broadcasted_iota(jnp.int32, (tq, tk), 1)
                < lax.broadcasted_iota(jnp.int32, (tq, tk), 0))
    zs, lks = [], []
    for h in heads:
        z = lax.dot_general(q_ref[:, h * SB_DH:(h + 1) * SB_DH], load_k(h), _NT,
                            preferred_element_type=F32) * SB_SCALE
        log_keep = -(jnp.maximum(z, 0.0) + jnp.log(1.0 + jnp.exp(-jnp.abs(z))))
        if diag:
            log_keep = jnp.where(mask, log_keep, 0.0)
        zs.append(z)
        lks.append(log_keep)
    stacked = jnp.concatenate(lks, axis=0)
    rc_all = sum(_dot(t, lower) for t in _split(stacked, 2))
    for i, h in enumerate(heads):
        rc = rc_all[i * tq:(i + 1) * tq]
        carry = carry_ref[h]
        attn = jnp.exp(zs[i] + rc + carry)
        if diag:
            attn = jnp.where(mask, attn, 0.0)
        acc_ref[h] += _dot(attn.astype(BF16), load_v(h))
        carry_ref[h] = carry + rc[:, 0:1]


def _sb_block(q_ref, k_ref, v_ref, base, tk, acc_ref, carry_ref, lower, *, group, diag):
    def loader(ref):
        return lambda h: ref[pl.ds(base + h, tk, stride=SB_HEADS), :].astype(BF16)
    for g in range(0, SB_HEADS, group):
        _sb_group(q_ref, loader(k_ref), loader(v_ref), range(g, g + group),
                  acc_ref, carry_ref, lower, diag=diag)


def _sb_store(o_ref, acc_ref):
    for h in range(SB_HEADS):
        o_ref[:, h * SB_DH:(h + 1) * SB_DH] = acc_ref[h].astype(BF16)


def _sb_prompt_kernel(q_ref, k_ref, v_ref, *rest, tq, group):
    o_ref, acc_ref, carry_ref = rest[-3:]
    qi = pl.program_id(1)
    acc_ref[...] = jnp.zeros_like(acc_ref)
    carry_ref[...] = jnp.zeros_like(carry_ref)
    lower = _lower_tri(tq)
    blk_rows = tq * SB_HEADS

    _sb_block(q_ref, k_ref, v_ref, pl.multiple_of(qi * blk_rows, blk_rows), tq,
              acc_ref, carry_ref, lower, group=group, diag=True)

    def body(j, c):
        base = pl.multiple_of((qi - 1 - j) * blk_rows, blk_rows)
        _sb_block(q_ref, k_ref, v_ref, base, tq, acc_ref, carry_ref, lower,
                  group=group, diag=False)
        return c

    lax.fori_loop(0, qi, body, 0)
    _sb_store(o_ref, acc_ref)


def _sb_sample_kernel(q_ref, k_ref, v_ref, kp_ref, vp_ref, *rest, tq, n_past):
    o_ref, acc_ref, carry_ref = rest[-3:]
    s = pl.program_id(1)

    @pl.when(s == 0)
    def _():
        acc_ref[...] = jnp.zeros_like(acc_ref)
        carry_ref[...] = jnp.zeros_like(carry_ref)
        _sb_block(q_ref, k_ref, v_ref, 0, tq, acc_ref, carry_ref, _lower_tri(tq),
                  group=SB_HEADS, diag=True)

    @pl.when(s > 0)
    def _():
        _sb_block(q_ref, kp_ref, vp_ref, 0, SB_TQ, acc_ref, carry_ref, _lower_tri(SB_TQ),
                  group=SB_HEADS, diag=False)

    @pl.when(s == n_past)
    def _():
        _sb_store(o_ref, acc_ref)


def _stick_breaking(bq, k8, v8, past_k, past_v, l, prev, *, bsz, t, row_off):
    tq = min(SB_TQ, t)
    nq = t // tq
    off_q = row_off // tq
    scratch = [pltpu.VMEM((SB_HEADS, tq, SB_DH), F32), pltpu.VMEM((SB_HEADS, tq, 1), F32)]
    out_shape = jax.ShapeDtypeStruct((bq.shape[0], SB_W), BF16)
    q_spec = pl.BlockSpec((tq, SB_W), lambda b, i: (off_q + b * nq + i, 0))
    if past_k is None:
        seq_spec = pl.BlockSpec((None, t * SB_HEADS, SB_DH), lambda b, i: (l, b, 0))
        args = [bq, k8, v8]
        prev_args, prev_specs, aliases = _alias_prev(prev, len(args))
        return pl.pallas_call(
            functools.partial(_sb_prompt_kernel, tq=tq, group=SB_HEADS // 2),
            out_shape=out_shape,
            grid=(bsz, nq),
            in_specs=[q_spec, seq_spec, seq_spec] + prev_specs,
            out_specs=q_spec,
            scratch_shapes=scratch,
            input_output_aliases=aliases,
            compiler_params=_params("parallel", "arbitrary"),
            name="stick_breaking_prompt",
        )(*args, *prev_args)
    assert nq == 1
    blk = SB_TQ * SB_HEADS
    n_past = past_k.shape[2] // blk
    q_spec = pl.BlockSpec((tq, SB_W), lambda b, s: (off_q + b, 0))
    new_spec = pl.BlockSpec((None, t * SB_HEADS, SB_DH), lambda b, s: (l, b, 0))
    past_spec = pl.BlockSpec((None, None, blk, SB_DH),
                             lambda b, s: (l, b, n_past - jnp.maximum(s, 1), 0))
    args = [bq, k8, v8, past_k, past_v]
    prev_args, prev_specs, aliases = _alias_prev(prev, len(args))
    return pl.pallas_call(
        functools.partial(_sb_sample_kernel, tq=tq, n_past=n_past),
        out_shape=out_shape,
        grid=(bsz, n_past + 1),
        in_specs=[q_spec, new_spec, new_spec, past_spec, past_spec] + prev_specs,
        out_specs=q_spec,
        scratch_shapes=scratch,
        input_output_aliases=aliases,
        compiler_params=_params("parallel", "arbitrary"),
        name="stick_breaking_sample",
    )(*args, *prev_args)


def _pool_kernel(u_ref, buf_ref, w_ref, s_ref, *rest, tt, pos0):
    o_ref, ext_ref = rest[-2:]
    i = pl.program_id(1)

    @pl.when(i == 0)
    def _():
        ext_ref[0:POOL_HALO, :] = buf_ref[0]

    @pl.when(i > 0)
    def _():
        ext_ref[0:POOL_HALO, :] = ext_ref[tt:tt + POOL_HALO, :]

    ext_ref[POOL_HALO:POOL_HALO + tt, :] = u_ref[...]
    pos = pos0 + i * tt + lax.broadcasted_iota(jnp.int32, (tt, 1), 0)
    for gi, w in enumerate(POOL_WINDOWS):
        cs = slice(gi * POOL_G, (gi + 1) * POOL_G)
        cur = ext_ref[POOL_HALO:POOL_HALO + tt, cs]
        wsum = cur
        for j in range(1, w):
            wsum = wsum + ext_ref[POOL_HALO - j:POOL_HALO - j + tt, cs]
        cnt = jnp.minimum(pos + 1, w).astype(F32)
        d = (wsum / cnt - cur).astype(BF16)
        y = _dot(d, w_ref[gi]) * s_ref[:, cs]
        o_ref[:, cs] = y.astype(BF16)


def _pool(u, buf, w_pool, scale, l, prev, *, bsz, t, row_off, pos0):
    tt = min(256, t)
    nt = t // tt
    off = row_off // tt
    row_spec = pl.BlockSpec((tt, POOL_W), lambda b, i: (off + b * nt + i, 0))
    args = [u, buf, w_pool, scale.reshape(1, POOL_W)]
    prev_args, prev_specs, aliases = _alias_prev(prev, len(args))
    return pl.pallas_call(
        functools.partial(_pool_kernel, tt=tt, pos0=pos0),
        out_shape=jax.ShapeDtypeStruct((u.shape[0], POOL_W), BF16),
        grid=(bsz, nt),
        in_specs=[
            row_spec,
            pl.BlockSpec((1, POOL_HALO, POOL_W), lambda b, i: (b, 0, 0)),
            pl.BlockSpec((None, len(POOL_WINDOWS), POOL_G, POOL_G), lambda b, i: (l, 0, 0, 0)),
            pl.BlockSpec((1, POOL_W), lambda b, i: (0, 0)),
        ] + prev_specs,
        out_specs=row_spec,
        scratch_shapes=[pltpu.VMEM((POOL_HALO + tt, POOL_W), F32)],
        input_output_aliases=aliases,
        compiler_params=_params("parallel", "arbitrary"),
        name="pool_mix",
    )(*args, *prev_args)


X_SCALE = X_DH ** -0.5


def _cross_kernel(q_ref, k_ref, v_ref, *rest):
    o_ref = rest[-1]
    for h in range(X_HEADS):
        cs = slice(h * X_DH, (h + 1) * X_DH)
        k = k_ref[0, :, cs].astype(BF16)
        v = v_ref[0, :, cs].astype(BF16)
        s = lax.dot_general(q_ref[:, cs], k, _NT, preferred_element_type=F32) * X_SCALE
        e = jnp.exp(s - jnp.max(s, axis=-1, keepdims=True))
        p = e / jnp.sum(e, axis=-1, keepdims=True)
        o_ref[:, cs] = _dot(p.astype(BF16), v).astype(BF16)


def _cross_attend(qx, mk, mv, l, prev, *, bsz, t, row_off):
    tq = min(256, t)
    nq = t // tq
    off = row_off // tq
    row_spec = pl.BlockSpec((tq, X_W), lambda b, i: (off + b * nq + i, 0))
    mem_spec = pl.BlockSpec((None, 1, mk.shape[2], X_W), lambda b, i: (l, b, 0, 0))
    args = [qx, mk, mv]
    prev_args, prev_specs, aliases = _alias_prev(prev, len(args))
    return pl.pallas_call(
        _cross_kernel,
        out_shape=jax.ShapeDtypeStruct((qx.shape[0], X_W), BF16),
        grid=(bsz, nq),
        in_specs=[row_spec, mem_spec, mem_spec] + prev_specs,
        out_specs=row_spec,
        input_output_aliases=aliases,
        compiler_params=_params("parallel", "arbitrary"),
        name="cross_attend",
    )(*args, *prev_args)


def _merge_kernel(x_ref, wg0_ref, wg1_ref, wg2_ref, bg0_ref, bg1_ref, bg2_ref,
                  ha_ref, hb_ref, hc_ref, wa_ref, wb_ref, wc_ref, o_ref):
    x = x_ref[...]
    out = None
    for wg_ref, bg_ref, h_ref, wp_ref in ((wg0_ref, bg0_ref, ha_ref, wa_ref),
                                          (wg1_ref, bg1_ref, hb_ref, wb_ref),
                                          (wg2_ref, bg2_ref, hc_ref, wc_ref)):
        gate = jax.nn.sigmoid(lax.dot_general(x, wg_ref[...], _NT, preferred_element_type=F32)
                              + bg_ref[...])
        term = gate * _dot(h_ref[...], wp_ref[...])
        out = term if out is None else out + term
    o_ref[...] = out.astype(BF16)


def _merge(xb, wt_b, bg, ha, hb, hc, w_pa, w_pb, w_pc, l, *, bm=512, bn=256):
    m = xb.shape[0]
    nb = D_MODEL // bn
    gb = B_G // bn
    row = lambda width: pl.BlockSpec((bm, width), lambda i, j: (i, 0))
    colw = lambda kdim: pl.BlockSpec((None, kdim, bn), lambda i, j: (l, 0, j))
    gate_w = [pl.BlockSpec((None, bn, D_MODEL), lambda i, j, g=g: (l, gb + g * nb + j, 0))
              for g in range(N_BRANCH)]
    gate_b = [pl.BlockSpec((1, bn), lambda i, j, g=g: (0, g * nb + j)) for g in range(N_BRANCH)]
    return pl.pallas_call(
        _merge_kernel,
        out_shape=jax.ShapeDtypeStruct((m, D_MODEL), BF16),
        grid=(m // bm, nb),
        in_specs=[row(D_MODEL)] + gate_w + gate_b
                 + [row(MLSTM_W), row(SB_W), row(POOL_W), colw(MLSTM_W), colw(SB_W), colw(POOL_W)],
        out_specs=pl.BlockSpec((bm, bn), lambda i, j: (i, j)),
        compiler_params=_params("parallel", "arbitrary"),
        name="gated_merge",
    )(xb, wt_b, wt_b, wt_b, bg, bg, bg, ha, hb, hc, w_pa, w_pb, w_pc)


def _trunk_layer(x, xb, W, l, groups, kv_prev, last):
    b_in = W["b_in"][l]
    proj_a = functools.partial(_matmul, xb, W["wt_in"], l, wt=True, bn=512)
    proj_b = functools.partial(_matmul, xb, W["wt_b"], l, wt=True)
    group_rows = tuple(g["bsz"] * g["t"] for g in groups)

    k_scale = jnp.concatenate([jnp.ones((MLSTM_QK_W,), F32),
                               jnp.full((MLSTM_QK_W,), MLSTM_DK ** -0.5, F32),
                               jnp.ones((MLSTM_W,), F32)])
    qkv = proj_a(n=OFF_AO, bias=b_in[:OFF_AO], scale=k_scale, out_dtype=BF16, name="proj_qkv")
    ao = proj_a(n=MLSTM_W, col_off=OFF_AO, bias=b_in[OFF_AO:OFF_AI], act="sigmoid",
                name="proj_ogate")
    b_if = jnp.pad(b_in[OFF_AI:OFF_BQ], (0, GATE_PAD - 2 * MLSTM_HEADS))
    gif = proj_a(n=GATE_PAD, col_off=OFF_AI, bias=b_if, bn=GATE_PAD, name="proj_if")
    sec = lambda i: b_in[OFF_BQ + i * SB_W:OFF_BQ + (i + 1) * SB_W]
    bq = proj_b(n=SB_W, bias=sec(0), out_dtype=BF16, name="proj_sb_q")
    k8 = _proj_heads(xb, W["wt_b"], l, sec(1), B_BK, group_rows, kv_prev and kv_prev[0],
                     name="proj_sb_k")
    v8 = _proj_heads(xb, W["wt_b"], l, sec(2), B_BV, group_rows, kv_prev and kv_prev[1],
                     name="proj_sb_v")
    u = proj_b(n=POOL_W, col_off=B_CU, bias=sec(3), name="proj_pool_u")

    ha = hb = hc = None
    states = []
    for gi, g in enumerate(groups):
        dims = dict(bsz=g["bsz"], t=g["t"], row_off=g["row_off"])
        ha, c_new, n_new, m_new = _mlstm(qkv, ao, gif, g["c"], g["n"], g["m"],
                                         W["mlstm_norm_g"][l], l, ha, **dims)
        hb = _stick_breaking(bq, k8[gi], v8[gi], g["sb_k"], g["sb_v"], l, hb, **dims)
        hc = _pool(u, g["pool_buf"][l], W["w_pool"], W["pool_scale"][l], l, hc,
                   pos0=g["pos0"], **dims)
        states.append((c_new, n_new, m_new))

    merged = _merge(xb, W["wt_b"], b_in[OFF_G:].reshape(1, -1), ha, hb, hc,
                    W["w_pa"], W["w_pb"], W["w_pc"], l)
    y = _matmul(merged, W["w_out"], l, n=D_MODEL, res=x, name="w_out")
    x, xb = _layer_norm(y, W["ln1_g"][l], W["ln1_b"][l])

    qx = _matmul(xb, W["wq_x"], l, n=X_W, out_dtype=BF16, name="wq_x")
    ox = None
    for g in groups:
        ox = _cross_attend(qx, g["mk"], g["mv"], l, ox, bsz=g["bsz"], t=g["t"],
                           row_off=g["row_off"])
    y = _matmul(ox, W["wo_x"], l, n=D_MODEL, res=x, name="wo_x")
    x, xb = _layer_norm(y, W["ln2_g"][l], W["ln2_b"][l])

    hid, w_down = _matmul(xb, W["w_up"], l, n=D_FF, act="relu2", out_dtype=BF16, bn=512,
                          side_cast=W["w_down"], name="w_up")
    y = _matmul(hid, w_down, 0, n=D_MODEL, bk=D_MODEL, name="w_down")
    if last:
        x = tuple(_layer_norm(x, W["ln3_g"][l], W["ln3_b"][l], res=y, in_off=g["row_off"],
                              n_rows=g["bsz"] * g["t"], emit_bf16=False) for g in groups)
        xb = None
    else:
        x, xb = _layer_norm(x, W["ln3_g"][l], W["ln3_b"][l], res=y)
    return x, xb, (k8, v8), u, states


def kernel(x_prompt, x_sample, cache_sb_k, cache_sb_v, state_mlstm_c, state_mlstm_n, state_mlstm_m, state_pool, cache_mem_k, cache_mem_v, mem_prompt, ln_in_g, ln_in_b, w_in, b_in, mlstm_norm_g, w_pool, pool_scale, w_pa, w_pb, w_pc, w_out, ln1_g, ln1_b, wq_x, wk_x, wv_x, wo_x, ln2_g, ln2_b, w_up, w_down, ln3_g, ln3_b):
    bf = lambda a: a.astype(BF16)
    wt_in = jnp.swapaxes(w_in, 1, 2)
    W = dict(wt_in=wt_in, wt_b=bf(wt_in[:, OFF_BQ:, :]),
             b_in=b_in, mlstm_norm_g=mlstm_norm_g, w_pool=bf(w_pool),
             pool_scale=pool_scale, w_pa=bf(w_pa), w_pb=bf(w_pb), w_pc=bf(w_pc), w_out=bf(w_out),
             ln1_g=ln1_g, ln1_b=ln1_b, wq_x=bf(wq_x), wo_x=bf(wo_x), ln2_g=ln2_g, ln2_b=ln2_b,
             w_up=w_up, w_down=w_down, ln3_g=ln3_g, ln3_b=ln3_b)
    bp, tp, _ = x_prompt.shape
    bs, ts, _ = x_sample.shape
    rows_p, rows_s = bp * tp, bs * ts
    rows = rows_p + rows_s
    past = cache_sb_k.shape[2]
    mem_len = mem_prompt.shape[1]

    mem_b = mem_prompt.reshape(bp * mem_len, D_MODEL).astype(BF16)
    wk_b, wv_b = bf(wk_x), bf(wv_x)
    mem4 = lambda a, b: a.reshape(DEPTH, b, mem_len, X_W)
    mk_p = mem4(jnp.stack([_matmul(mem_b, wk_b, l, n=X_W, name="mem_k") for l in range(DEPTH)]), bp)
    mv_p = mem4(jnp.stack([_matmul(mem_b, wv_b, l, n=X_W, name="mem_v") for l in range(DEPTH)]), bp)

    lane_m = lambda m: jnp.broadcast_to(m[..., None], m.shape + (LANES,))
    halo = lambda b: jnp.pad(b, ((0, 0), (0, 0), (POOL_HALO - POOL_BUF, 0), (0, 0)))
    heads_rows = lambda c: c.reshape(DEPTH, bs, past * SB_HEADS, SB_DH)
    groups = [
        dict(bsz=bp, t=tp, row_off=0, pos0=0,
             c=jnp.zeros((DEPTH, bp, MLSTM_HEADS, MLSTM_DK, MLSTM_DV), F32),
             n=jnp.zeros((DEPTH, bp, MLSTM_HEADS, MLSTM_DK), F32),
             m=jnp.zeros((DEPTH, bp, MLSTM_HEADS, LANES), F32),
             sb_k=None, sb_v=None,
             pool_buf=jnp.zeros((DEPTH, bp, POOL_HALO, POOL_W), F32),
             mk=mk_p, mv=mv_p),
        dict(bsz=bs, t=ts, row_off=rows_p, pos0=past,
             c=state_mlstm_c, n=state_mlstm_n, m=lane_m(state_mlstm_m),
             sb_k=heads_rows(cache_sb_k), sb_v=heads_rows(cache_sb_v),
             pool_buf=halo(state_pool),
             mk=mem4(cache_mem_k, bs), mv=mem4(cache_mem_v, bs)),
    ]

    xs = None
    for g, xin in zip(groups, (x_prompt, x_sample)):
        xs = _layer_norm(xin.reshape(g["bsz"] * g["t"], D_MODEL), ln_in_g, ln_in_b,
                         out_off=g["row_off"], out_rows=rows, prev=xs)
    x, xb = xs

    kv = None
    per_layer = []
    for l in range(DEPTH):
        x, xb, kv, u, states = _trunk_layer(x, xb, W, l, groups, kv, last=l == DEPTH - 1)
        per_layer.append((u, states))

    def group_out(gi, g):
        bsz, t, r0 = g["bsz"], g["t"], g["row_off"]
        sbk = kv[0][gi].reshape(DEPTH, bsz, t, SB_HEADS, SB_DH)
        sbv = kv[1][gi].reshape(DEPTH, bsz, t, SB_HEADS, SB_DH)
        c = jnp.stack([st[gi][0] for _, st in per_layer])
        n = jnp.stack([st[gi][1] for _, st in per_layer])
        m = jnp.stack([st[gi][2][..., 0] for _, st in per_layer])
        pool = jnp.stack([u[r0:r0 + bsz * t].reshape(bsz, t, POOL_W)[:, t - POOL_BUF:, :]
                          for u, _ in per_layer])
        return sbk, sbv, c, n, m, pool

    out_p = group_out(0, groups[0])
    out_s = group_out(1, groups[1])
    y_prompt = x[0].reshape(bp, tp, D_MODEL)
    y_sample = x[1].reshape(bs, ts, D_MODEL)
    mem5 = lambda a: a.reshape(DEPTH, bp, mem_len, X_HEADS, X_DH)
    return (y_prompt, y_sample) + out_p + (mem5(mk_p), mem5(mv_p)) + out_s
```

```python
import functools

import jax
import jax.numpy as jnp
from jax import lax
from jax.experimental import pallas as pl
from jax.experimental.pallas import tpu as pltpu

F32 = jnp.float32
BF16 = jnp.bfloat16

D_MODEL = 4096
DEPTH = 2
MLSTM_CHUNK = 256
MLSTM_HEADS = 8
MLSTM_W = D_MODEL // 2
MLSTM_DV = MLSTM_W // MLSTM_HEADS
MLSTM_DK = MLSTM_DV // 2
MLSTM_QK_W = MLSTM_HEADS * MLSTM_DK
SB_HEADS = 8
SB_W = D_MODEL // 4
SB_DH = SB_W // SB_HEADS
POOL_WINDOWS = (2, 4, 8, 16)
POOL_W = D_MODEL // 4
POOL_G = POOL_W // len(POOL_WINDOWS)
POOL_BUF = max(POOL_WINDOWS) - 1
POOL_HALO = POOL_BUF + 1
X_HEADS = 4
X_W = D_MODEL // 4
X_DH = X_W // X_HEADS
D_FF = 4 * D_MODEL
N_BRANCH = 3
ALPHA = (2 * DEPTH) ** 0.25
LN_EPS = 1e-5
HEAD_NORM_EPS = 1e-6

OFF_AO = 2 * MLSTM_QK_W + MLSTM_W
OFF_AI = OFF_AO + MLSTM_W
OFF_BQ = OFF_AI + 2 * MLSTM_HEADS
OFF_G = OFF_BQ + 3 * SB_W + POOL_W
IN_W = OFF_G + N_BRANCH * D_MODEL

V7X_VMEM_LIMIT_BYTES = 56 * 1024 * 1024
LANES = 128
SUBLANES = 8
SB_TQ = 256
GATE_PAD = LANES

B_BK = SB_W
B_BV = 2 * SB_W
B_CU = 3 * SB_W
B_G = 3 * SB_W + POOL_W

_NT = (((1,), (1,)), ((), ()))
_TN = (((0,), (0,)), ((), ()))


def _params(*sem):
    return pltpu.CompilerParams(dimension_semantics=sem,
                                vmem_limit_bytes=V7X_VMEM_LIMIT_BYTES)


def _dot(a, b):
    return jnp.dot(a, b, preferred_element_type=F32)


def _split(x, terms):
    out = []
    for _ in range(terms - 1):
        h = x.astype(BF16)
        out.append(h)
        x = x - h.astype(F32)
    out.append(x.astype(BF16))
    return out


def _alias_prev(prev, n_in):
    if prev is None:
        return [], [], {}
    prev = list(prev) if isinstance(prev, (tuple, list)) else [prev]
    specs = [pl.BlockSpec(memory_space=pl.ANY)] * len(prev)
    return prev, specs, {n_in + i: i for i in range(len(prev))}


def _mm_kernel(*refs, act, has_bias, has_scale, has_res, nk, side_slabs, wt):
    it = iter(refs)
    x_ref, w_ref = next(it), next(it)
    b_ref = next(it) if has_bias else None
    s_ref = next(it) if has_scale else None
    r_ref = next(it) if has_res else None
    side_in_ref = next(it) if side_slabs else None
    o_ref = next(it)
    side_out_ref = next(it) if side_slabs else None
    acc_ref = next(it) if nk > 1 else None

    if side_slabs:
        step = pl.program_id(0) * pl.num_programs(1) + pl.program_id(1)

        @pl.when(step < side_slabs)
        def _():
            side_out_ref[...] = side_in_ref[0].astype(BF16)

    def epilogue(y):
        if has_bias:
            y = y + b_ref[...]
        if has_scale:
            y = y * s_ref[...]
        if act == "sigmoid":
            y = jax.nn.sigmoid(y)
        elif act == "relu2":
            y = jnp.square(jnp.maximum(y, 0.0))
        if has_res:
            y = ALPHA * r_ref[...] + y
        o_ref[...] = y.astype(o_ref.dtype)

    w = w_ref[...].astype(BF16)
    part = lax.dot_general(x_ref[...], w, _NT if wt else (((1,), (0,)), ((), ())),
                           preferred_element_type=F32)
    if nk == 1:
        epilogue(part)
    else:
        k = pl.program_id(2)

        @pl.when(k == 0)
        def _():
            acc_ref[...] = part

        @pl.when(k > 0)
        def _():
            acc_ref[...] += part

        @pl.when(k == nk - 1)
        def _():
            epilogue(acc_ref[...])


def _matmul(x, w, l, *, n, col_off=0, bias=None, scale=None, res=None, act="none",
            out_dtype=F32, bm=1024, bn=1024, bk=None, side_cast=None, wt=False, name="matmul"):
    m, kdim = x.shape
    bm, bn = min(bm, m), min(bn, n)
    bk = kdim if bk is None else bk
    assert m % bm == 0 and n % bn == 0 and kdim % bk == 0 and col_off % bn == 0
    nk = kdim // bk
    cb = col_off // bn
    nj = n // bn
    w_spec = (pl.BlockSpec((None, bn, bk), lambda i, j, k: (l, cb + j, k)) if wt
              else pl.BlockSpec((None, bk, bn), lambda i, j, k: (l, k, cb + j)))
    in_specs = [pl.BlockSpec((bm, bk), lambda i, j, k: (i, k)), w_spec]
    args = [x, w]
    for v in (bias, scale):
        if v is not None:
            in_specs.append(pl.BlockSpec((1, bn), lambda i, j, k: (0, j)))
            args.append(v.reshape(1, n).astype(F32))
    if res is not None:
        in_specs.append(pl.BlockSpec((bm, bn), lambda i, j, k: (i, j)))
        args.append(res)
    out_shape = [jax.ShapeDtypeStruct((m, n), out_dtype)]
    out_specs = [pl.BlockSpec((bm, bn), lambda i, j, k: (i, j))]
    side_slabs = 0
    if side_cast is not None:
        side_w, row_off, sr, slab_rows = side_cast
        sc = side_w.shape[2]
        side_slabs = sr // slab_rows
        assert nk == 1 and sr % slab_rows == 0 and side_slabs <= (m // bm) * nj
        slab = lambda i, j, k: jnp.minimum(i * nj + j, side_slabs - 1)
        in_specs.append(pl.BlockSpec((pl.Element(1), pl.Element(slab_rows), pl.Element(sc)),
                                     lambda i, j, k: (l, pl.multiple_of(
                                         row_off + slab(i, j, k) * slab_rows, SUBLANES), 0)))
        args.append(side_w)
        out_shape.append(jax.ShapeDtypeStruct((1, sr, sc), BF16))
        out_specs.append(pl.BlockSpec((None, slab_rows, sc), lambda i, j, k: (0, slab(i, j, k), 0)))
    kern = functools.partial(_mm_kernel, act=act, has_bias=bias is not None,
                             has_scale=scale is not None, has_res=res is not None, nk=nk,
                             side_slabs=side_slabs, wt=wt)
    out = pl.pallas_call(
        kern,
        out_shape=tuple(out_shape),
        grid=(m // bm, nj, nk),
        in_specs=in_specs,
        out_specs=tuple(out_specs),
        scratch_shapes=[pltpu.VMEM((bm, bn), F32)] if nk > 1 else [],
        compiler_params=(_params("arbitrary", "arbitrary", "arbitrary") if side_slabs
                         else _params("parallel", "parallel", "arbitrary")),
        name=name,
    )(*args)
    return out if side_slabs else out[0]


def _proj_heads_kernel(x_ref, w_ref, b_ref, *rest, n_first):
    first_ref, second_ref = rest[-2:]
    y = lax.dot_general(x_ref[...], w_ref[...], _NT, preferred_element_type=F32) + b_ref[...]
    rows = y.shape[0]

    def store(o_ref):
        for h in range(SUBLANES):
            o_ref[pl.ds(h, rows, stride=SUBLANES), :] = y[:, h * LANES:(h + 1) * LANES]

    i = pl.program_id(0)

    @pl.when(i < n_first)
    def _():
        store(first_ref)

    @pl.when(i >= n_first)
    def _():
        store(second_ref)


def _proj_heads(x, w, l, bias, col_off, group_rows, prev, *, bm=1024, name="proj_heads"):
    m, kdim = x.shape
    n = SUBLANES * LANES
    rows_a, rows_b = group_rows
    assert rows_a % bm == 0 and rows_b % bm == 0 and rows_a + rows_b == m and col_off % n == 0
    n_first = rows_a // bm
    cb = col_off // n
    args = [x, w, bias.reshape(1, n)]
    prev_args, prev_specs, aliases = _alias_prev(prev, len(args))
    blk = (None, bm * SUBLANES, LANES)
    return pl.pallas_call(
        functools.partial(_proj_heads_kernel, n_first=n_first),
        out_shape=tuple(jax.ShapeDtypeStruct((DEPTH, r * SUBLANES, LANES), F32) for r in group_rows),
        grid=(m // bm,),
        in_specs=[pl.BlockSpec((bm, kdim), lambda i: (i, 0)),
                  pl.BlockSpec((None, n, kdim), lambda i: (0, cb, 0)),
                  pl.BlockSpec((1, n), lambda i: (0, 0))] + prev_specs,
        out_specs=(pl.BlockSpec(blk, lambda i: (l, jnp.minimum(i, n_first - 1), 0)),
                   pl.BlockSpec(blk, lambda i: (l, jnp.maximum(i - n_first, 0), 0))),
        input_output_aliases=aliases,
        compiler_params=_params("arbitrary"),
        name=name,
    )(*args, *prev_args)


def _ln_kernel(*refs, has_res, n_prev, emit_bf16):
    it = iter(refs)
    x_ref = next(it)
    r_ref = next(it) if has_res else None
    g_ref, b_ref = next(it), next(it)
    for _ in range(n_prev):
        next(it)
    o_ref = next(it)
    x = x_ref[...]
    if has_res:
        x = ALPHA * x + r_ref[...]
    mu = jnp.mean(x, axis=-1, keepdims=True)
    xc = x - mu
    var = jnp.mean(xc * xc, axis=-1, keepdims=True)
    y = xc * lax.rsqrt(var + LN_EPS) * g_ref[...] + b_ref[...]
    o_ref[...] = y
    if emit_bf16:
        next(it)[...] = y.astype(BF16)


def _layer_norm(x, g, b, res=None, *, in_off=0, n_rows=None, out_off=0, out_rows=None,
                prev=None, emit_bf16=True, rows=256):
    d = x.shape[1]
    n_rows = x.shape[0] - in_off if n_rows is None else n_rows
    out_rows = n_rows if out_rows is None else out_rows
    ib, ob = in_off // rows, out_off // rows
    in_spec = pl.BlockSpec((rows, d), lambda i: (ib + i, 0))
    out_spec = pl.BlockSpec((rows, d), lambda i: (ob + i, 0))
    vec_spec = pl.BlockSpec((1, d), lambda i: (0, 0))
    args = [x] + ([res] if res is not None else []) + [g.reshape(1, d), b.reshape(1, d)]
    in_specs = [in_spec] * (2 if res is not None else 1) + [vec_spec, vec_spec]
    prev_args, prev_specs, aliases = _alias_prev(prev, len(args))
    out_shape = [jax.ShapeDtypeStruct((out_rows, d), F32)]
    if emit_bf16:
        out_shape.append(jax.ShapeDtypeStruct((out_rows, d), BF16))
    out = pl.pallas_call(
        functools.partial(_ln_kernel, has_res=res is not None, n_prev=len(prev_args),
                          emit_bf16=emit_bf16),
        out_shape=tuple(out_shape),
        grid=(n_rows // rows,),
        in_specs=in_specs + prev_specs,
        out_specs=tuple([out_spec] * len(out_shape)),
        input_output_aliases=aliases,
        compiler_params=_params("parallel"),
        name="layer_norm",
    )(*args, *prev_args)
    return out if emit_bf16 else out[0]


def _mlstm_kernel(q_ref, k_ref, v_ref, ao_ref, gif_ref, c0_ref, n0_ref, m0_ref, g_ref, *rest):
    ha_ref, c_ref, n_ref, m_ref = rest[-4:]
    L = q_ref.shape[0]

    @pl.when(pl.program_id(1) == 0)
    def _():
        c_ref[...] = c0_ref[...]
        n_ref[...] = n0_ref[...]
        m_ref[...] = m0_ref[...]

    gif = gif_ref[...]
    logf = -(jnp.maximum(-gif, 0.0) + jnp.log1p(jnp.exp(-jnp.abs(gif))))
    row = lax.broadcasted_iota(jnp.int32, (L, L), 0)
    col = lax.broadcasted_iota(jnp.int32, (L, L), 1)
    causal = col <= row
    tril = jnp.where(causal, 1.0, 0.0).astype(BF16)
    b_all = sum(_dot(tril, t) for t in _split(logf, 3))
    sel_r = lax.broadcasted_iota(jnp.int32, (2 * MLSTM_HEADS, GATE_PAD), 0)
    sel_c = lax.broadcasted_iota(jnp.int32, (2 * MLSTM_HEADS, GATE_PAD), 1)
    sel = jnp.where(sel_r == sel_c, 1.0, 0.0).astype(BF16)

    def rows_of(x):
        return sum(lax.dot_general(sel, t, _NT, preferred_element_type=F32) for t in _split(x, 3))

    i_rows = rows_of(gif)
    b_rows = rows_of(b_all)

    for h in range(MLSTM_HEADS):
        qs = slice(h * MLSTM_DK, (h + 1) * MLSTM_DK)
        vs = slice(h * MLSTM_DV, (h + 1) * MLSTM_DV)
        q = q_ref[:, qs]
        k = k_ref[:, qs]
        v = v_ref[:, vs]
        i_col = gif[:, h:h + 1]
        b_col = b_all[:, MLSTM_HEADS + h:MLSTM_HEADS + h + 1]
        i_row = i_rows[h:h + 1, :]
        b_row = b_rows[MLSTM_HEADS + h:MLSTM_HEADS + h + 1, :]
        c_old = c_ref[0, h]
        n_old = n_ref[0, h:h + 1, :]
        m_old = m_ref[0, h:h + 1, 0:1]

        dmat = jnp.where(causal, b_col - b_row + i_row, -jnp.inf)
        m_inter = b_col + m_old
        m_t = jnp.maximum(m_inter, jnp.max(dmat, axis=-1, keepdims=True))
        s = lax.dot_general(q, k, _NT, preferred_element_type=F32) * jnp.exp(dmat - m_t)
        w_inter = jnp.exp(m_inter - m_t)
        kf = k.astype(F32)
        num = _dot(s.astype(BF16), v) + w_inter * _dot(q, c_old.astype(BF16))
        qn = jnp.sum(q.astype(F32) * n_old, axis=-1, keepdims=True)
        den = jnp.sum(s, axis=-1, keepdims=True) + w_inter * qn
        hh = num / jnp.maximum(jnp.abs(den), jnp.exp(-m_t))

        m_new = m_t[L - 1:L, :]
        b_last = b_col[L - 1:L, :]
        g_col = jnp.exp(b_last - b_col + i_col - m_new)
        decay = jnp.exp(b_last + m_old - m_new)
        kg = kf * g_col
        c_ref[0, h] = decay * c_old + lax.dot_general(kg.astype(BF16), v, _TN,
                                                      preferred_element_type=F32)
        n_ref[0, h:h + 1, :] = decay * n_old + jnp.sum(kg, axis=0, keepdims=True)
        m_ref[0, h:h + 1, :] = jnp.broadcast_to(m_new, (1, LANES))

        mu = jnp.mean(hh, axis=-1, keepdims=True)
        hc = hh - mu
        var = jnp.mean(hc * hc, axis=-1, keepdims=True)
        hn = hc * lax.rsqrt(var + HEAD_NORM_EPS) * g_ref[:, vs]
        ha_ref[:, vs] = (ao_ref[:, vs] * hn).astype(BF16)


def _mlstm(qkv, ao, gif, c0, n0, m0, norm_g, l, prev, *, bsz, t, row_off):
    chunk = min(MLSTM_CHUNK, t)
    nc = t // chunk
    off = row_off // chunk
    rows = lambda b, c: off + b * nc + c
    state4 = pl.BlockSpec((1, MLSTM_HEADS, MLSTM_DK, MLSTM_DV), lambda b, c: (b, 0, 0, 0))
    state3 = pl.BlockSpec((1, MLSTM_HEADS, LANES), lambda b, c: (b, 0, 0))
    init4 = pl.BlockSpec((None, 1, MLSTM_HEADS, MLSTM_DK, MLSTM_DV), lambda b, c: (l, b, 0, 0, 0))
    init3 = pl.BlockSpec((None, 1, MLSTM_HEADS, LANES), lambda b, c: (l, b, 0, 0))
    args = [qkv, qkv, qkv, ao, gif, c0, n0, m0, norm_g.reshape(1, MLSTM_W)]
    prev_args, prev_specs, aliases = _alias_prev(prev, len(args))
    return pl.pallas_call(
        _mlstm_kernel,
        out_shape=(jax.ShapeDtypeStruct((qkv.shape[0], MLSTM_W), BF16),
                   jax.ShapeDtypeStruct(c0.shape[1:], F32),
                   jax.ShapeDtypeStruct(n0.shape[1:], F32),
                   jax.ShapeDtypeStruct(m0.shape[1:], F32)),
        grid=(bsz, nc),
        in_specs=[
            pl.BlockSpec((chunk, MLSTM_QK_W), lambda b, c: (rows(b, c), 0)),
            pl.BlockSpec((chunk, MLSTM_QK_W), lambda b, c: (rows(b, c), 1)),
            pl.BlockSpec((chunk, MLSTM_W), lambda b, c: (rows(b, c), 1)),
            pl.BlockSpec((chunk, MLSTM_W), lambda b, c: (rows(b, c), 0)),
            pl.BlockSpec((chunk, GATE_PAD), lambda b, c: (rows(b, c), 0)),
            init4, init3, init3,
            pl.BlockSpec((1, MLSTM_W), lambda b, c: (0, 0)),
        ] + prev_specs,
        out_specs=(pl.BlockSpec((chunk, MLSTM_W), lambda b, c: (rows(b, c), 0)),
                   state4, state3, state3),
        input_output_aliases=aliases,
        compiler_params=_params("parallel", "arbitrary"),
        name="mlstm",
    )(*args, *prev_args)


SB_SCALE = SB_DH ** -0.5
SB_DEAD_CARRY = -110.0


def _lower_tri(n):
    return jnp.where(lax.broadcasted_iota(jnp.int32, (n, n), 0)
                     >= lax.broadcasted_iota(jnp.int32, (n, n), 1), 1.0, 0.0).astype(BF16)


def _sb_group(q_ref, load_k, load_v, heads, acc_ref, carry_ref, lower, *, diag):
    tq = q_ref.shape[0]
    tk = lower.shape[0]
    if diag:
        mask = (lax.broadcasted_iota(jnp.int32, (tq, tk), 1)
                < lax.broadcasted_iota(jnp.int32, (tq, tk), 0))
    zs, lks = [], []
    for h in heads:
        z = lax.dot_general(q_ref[:, h * SB_DH:(h + 1) * SB_DH], load_k(h), _NT,
                            preferred_element_type=F32) * SB_SCALE
        log_keep = -(jnp.maximum(z, 0.0) + jnp.log(1.0 + jnp.exp(-jnp.abs(z))))
        if diag:
            log_keep = jnp.where(mask, log_keep, 0.0)
        zs.append(z)
        lks.append(log_keep)
    stacked = jnp.concatenate(lks, axis=0)
    rc_all = sum(_dot(t, lower) for t in _split(stacked, 2))
    for i, h in enumerate(heads):
        rc = rc_all[i * tq:(i + 1) * tq]
        carry = carry_ref[h]
        attn = jnp.exp(zs[i] + rc + carry)
        if diag:
            attn = jnp.where(mask, attn, 0.0)
        acc_ref[h] += _dot(attn.astype(BF16), load_v(h))
        carry_ref[h] = carry + rc[:, 0:1]


def _sb_block(q_ref, k_ref, v_ref, base, tk, acc_ref, carry_ref, lower, *, group, diag):
    def loader(ref):
        return lambda h: ref[pl.ds(base + h, tk, stride=SB_HEADS), :].astype(BF16)
    for g in range(0, SB_HEADS, group):
        _sb_group(q_ref, loader(k_ref), loader(v_ref), range(g, g + group),
                  acc_ref, carry_ref, lower, diag=diag)


def _sb_alive(carry_ref):
    return jnp.max(carry_ref[...]) > SB_DEAD_CARRY


def _sb_store(o_ref, acc_ref):
    for h in range(SB_HEADS):
        o_ref[:, h * SB_DH:(h + 1) * SB_DH] = acc_ref[h].astype(BF16)


def _sb_prompt_kernel(q_ref, k_ref, v_ref, *rest, tq, group):
    o_ref, acc_ref, carry_ref = rest[-3:]
    qi = pl.program_id(1)
    acc_ref[...] = jnp.zeros_like(acc_ref)
    carry_ref[...] = jnp.zeros_like(carry_ref)
    lower = _lower_tri(tq)
    blk_rows = tq * SB_HEADS

    _sb_block(q_ref, k_ref, v_ref, pl.multiple_of(qi * blk_rows, blk_rows), tq,
              acc_ref, carry_ref, lower, group=group, diag=True)

    def body(j, c):
        @pl.when(_sb_alive(carry_ref))
        def _():
            base = pl.multiple_of((qi - 1 - j) * blk_rows, blk_rows)
            _sb_block(q_ref, k_ref, v_ref, base, tq, acc_ref, carry_ref, lower,
                      group=group, diag=False)
        return c

    lax.fori_loop(0, qi, body, 0)
    _sb_store(o_ref, acc_ref)


def _sb_sample_kernel(q_ref, k_ref, v_ref, kp_ref, vp_ref, *rest, tq, n_past):
    o_ref, acc_ref, carry_ref = rest[-3:]
    s = pl.program_id(1)

    @pl.when(s == 0)
    def _():
        acc_ref[...] = jnp.zeros_like(acc_ref)
        carry_ref[...] = jnp.zeros_like(carry_ref)
        _sb_block(q_ref, k_ref, v_ref, 0, tq, acc_ref, carry_ref, _lower_tri(tq),
                  group=SB_HEADS, diag=True)

    @pl.when(jnp.logical_and(s > 0, _sb_alive(carry_ref)))
    def _():
        _sb_block(q_ref, kp_ref, vp_ref, 0, SB_TQ, acc_ref, carry_ref, _lower_tri(SB_TQ),
                  group=SB_HEADS, diag=False)

    @pl.when(s == n_past)
    def _():
        _sb_store(o_ref, acc_ref)


def _stick_breaking(bq, k8, v8, past_k, past_v, l, prev, *, bsz, t, row_off):
    tq = min(SB_TQ, t)
    nq = t // tq
    off_q = row_off // tq
    scratch = [pltpu.VMEM((SB_HEADS, tq, SB_DH), F32), pltpu.VMEM((SB_HEADS, tq, 1), F32)]
    out_shape = jax.ShapeDtypeStruct((bq.shape[0], SB_W), BF16)
    q_spec = pl.BlockSpec((tq, SB_W), lambda b, i: (off_q + b * nq + i, 0))
    if past_k is None:
        seq_spec = pl.BlockSpec((None, t * SB_HEADS, SB_DH), lambda b, i: (l, b, 0))
        args = [bq, k8, v8]
        prev_args, prev_specs, aliases = _alias_prev(prev, len(args))
        return pl.pallas_call(
            functools.partial(_sb_prompt_kernel, tq=tq, group=SB_HEADS // 2),
            out_shape=out_shape,
            grid=(bsz, nq),
            in_specs=[q_spec, seq_spec, seq_spec] + prev_specs,
            out_specs=q_spec,
            scratch_shapes=scratch,
            input_output_aliases=aliases,
            compiler_params=_params("parallel", "arbitrary"),
            name="stick_breaking_prompt",
        )(*args, *prev_args)
    assert nq == 1
    blk = SB_TQ * SB_HEADS
    n_past = past_k.shape[2] // blk
    q_spec = pl.BlockSpec((tq, SB_W), lambda b, s: (off_q + b, 0))
    new_spec = pl.BlockSpec((None, t * SB_HEADS, SB_DH), lambda b, s: (l, b, 0))
    past_spec = pl.BlockSpec((None, None, blk, SB_DH),
                             lambda b, s: (l, b, n_past - jnp.maximum(s, 1), 0))
    args = [bq, k8, v8, past_k, past_v]
    prev_args, prev_specs, aliases = _alias_prev(prev, len(args))
    return pl.pallas_call(
        functools.partial(_sb_sample_kernel, tq=tq, n_past=n_past),
        out_shape=out_shape,
        grid=(bsz, n_past + 1),
        in_specs=[q_spec, new_spec, new_spec, past_spec, past_spec] + prev_specs,
        out_specs=q_spec,
        scratch_shapes=scratch,
        input_output_aliases=aliases,
        compiler_params=_params("parallel", "arbitrary"),
        name="stick_breaking_sample",
    )(*args, *prev_args)


def _pool_kernel(u_ref, buf_ref, w_ref, s_ref, *rest, tt, pos0):
    o_ref, ext_ref = rest[-2:]
    i = pl.program_id(1)

    @pl.when(i == 0)
    def _():
        ext_ref[0:POOL_HALO, :] = buf_ref[0]

    @pl.when(i > 0)
    def _():
        ext_ref[0:POOL_HALO, :] = ext_ref[tt:tt + POOL_HALO, :]

    ext_ref[POOL_HALO:POOL_HALO + tt, :] = u_ref[...]
    pos = pos0 + i * tt + lax.broadcasted_iota(jnp.int32, (tt, 1), 0)
    for gi, w in enumerate(POOL_WINDOWS):
        cs = slice(gi * POOL_G, (gi + 1) * POOL_G)
        cur = ext_ref[POOL_HALO:POOL_HALO + tt, cs]
        wsum = cur
        for j in range(1, w):
            wsum = wsum + ext_ref[POOL_HALO - j:POOL_HALO - j + tt, cs]
        cnt = jnp.minimum(pos + 1, w).astype(F32)
        d = (wsum / cnt - cur).astype(BF16)
        y = _dot(d, w_ref[gi]) * s_ref[:, cs]
        o_ref[:, cs] = y.astype(BF16)


def _pool(u, buf, w_pool, scale, l, prev, *, bsz, t, row_off, pos0):
    tt = min(256, t)
    nt = t // tt
    off = row_off // tt
    row_spec = pl.BlockSpec((tt, POOL_W), lambda b, i: (off + b * nt + i, 0))
    args = [u, buf, w_pool, scale.reshape(1, POOL_W)]
    prev_args, prev_specs, aliases = _alias_prev(prev, len(args))
    return pl.pallas_call(
        functools.partial(_pool_kernel, tt=tt, pos0=pos0),
        out_shape=jax.ShapeDtypeStruct((u.shape[0], POOL_W), BF16),
        grid=(bsz, nt),
        in_specs=[
            row_spec,
            pl.BlockSpec((1, POOL_HALO, POOL_W), lambda b, i: (b, 0, 0)),
            pl.BlockSpec((None, len(POOL_WINDOWS), POOL_G, POOL_G), lambda b, i: (l, 0, 0, 0)),
            pl.BlockSpec((1, POOL_W), lambda b, i: (0, 0)),
        ] + prev_specs,
        out_specs=row_spec,
        scratch_shapes=[pltpu.VMEM((POOL_HALO + tt, POOL_W), F32)],
        input_output_aliases=aliases,
        compiler_params=_params("parallel", "arbitrary"),
        name="pool_mix",
    )(*args, *prev_args)


X_SCALE = X_DH ** -0.5


def _cross_kernel(q_ref, k_ref, v_ref, *rest):
    o_ref = rest[-1]
    for h in range(X_HEADS):
        cs = slice(h * X_DH, (h + 1) * X_DH)
        k = k_ref[0, :, cs].astype(BF16)
        v = v_ref[0, :, cs].astype(BF16)
        s = lax.dot_general(q_ref[:, cs], k, _NT, preferred_element_type=F32) * X_SCALE
        e = jnp.exp(s - jnp.max(s, axis=-1, keepdims=True))
        p = e / jnp.sum(e, axis=-1, keepdims=True)
        o_ref[:, cs] = _dot(p.astype(BF16), v).astype(BF16)


def _cross_attend(qx, mk, mv, l, prev, *, bsz, t, row_off):
    tq = min(256, t)
    nq = t // tq
    off = row_off // tq
    row_spec = pl.BlockSpec((tq, X_W), lambda b, i: (off + b * nq + i, 0))
    mem_spec = pl.BlockSpec((None, 1, mk.shape[2], X_W), lambda b, i: (l, b, 0, 0))
    args = [qx, mk, mv]
    prev_args, prev_specs, aliases = _alias_prev(prev, len(args))
    return pl.pallas_call(
        _cross_kernel,
        out_shape=jax.ShapeDtypeStruct((qx.shape[0], X_W), BF16),
        grid=(bsz, nq),
        in_specs=[row_spec, mem_spec, mem_spec] + prev_specs,
        out_specs=row_spec,
        input_output_aliases=aliases,
        compiler_params=_params("parallel", "arbitrary"),
        name="cross_attend",
    )(*args, *prev_args)


def _merge_kernel(x_ref, wg0_ref, wg1_ref, wg2_ref, bg0_ref, bg1_ref, bg2_ref,
                  ha_ref, hb_ref, hc_ref, wa_ref, wb_ref, wc_ref, o_ref):
    x = x_ref[...]
    out = None
    for wg_ref, bg_ref, h_ref, wp_ref in ((wg0_ref, bg0_ref, ha_ref, wa_ref),
                                          (wg1_ref, bg1_ref, hb_ref, wb_ref),
                                          (wg2_ref, bg2_ref, hc_ref, wc_ref)):
        gate = jax.nn.sigmoid(lax.dot_general(x, wg_ref[...], _NT, preferred_element_type=F32)
                              + bg_ref[...])
        term = gate * _dot(h_ref[...], wp_ref[...])
        out = term if out is None else out + term
    o_ref[...] = out.astype(BF16)


def _merge(xb, wt_b, bg, ha, hb, hc, w_pa, w_pb, w_pc, l, *, bm=512, bn=256):
    m = xb.shape[0]
    nb = D_MODEL // bn
    gb = B_G // bn
    row = lambda width: pl.BlockSpec((bm, width), lambda i, j: (i, 0))
    colw = lambda kdim: pl.BlockSpec((None, kdim, bn), lambda i, j: (l, 0, j))
    gate_w = [pl.BlockSpec((None, bn, D_MODEL), lambda i, j, g=g: (0, gb + g * nb + j, 0))
              for g in range(N_BRANCH)]
    gate_b = [pl.BlockSpec((1, bn), lambda i, j, g=g: (0, g * nb + j)) for g in range(N_BRANCH)]
    return pl.pallas_call(
        _merge_kernel,
        out_shape=jax.ShapeDtypeStruct((m, D_MODEL), BF16),
        grid=(m // bm, nb),
        in_specs=[row(D_MODEL)] + gate_w + gate_b
                 + [row(MLSTM_W), row(SB_W), row(POOL_W), colw(MLSTM_W), colw(SB_W), colw(POOL_W)],
        out_specs=pl.BlockSpec((bm, bn), lambda i, j: (i, j)),
        compiler_params=_params("parallel", "arbitrary"),
        name="gated_merge",
    )(xb, wt_b, wt_b, wt_b, bg, bg, bg, ha, hb, hc, w_pa, w_pb, w_pc)


def _trunk_layer(x, xb, W, l, groups, kv_prev, last):
    b_in = W["b_in"][l]
    proj_a = functools.partial(_matmul, xb, W["wt_in"], l, wt=True, bn=512)
    group_rows = tuple(g["bsz"] * g["t"] for g in groups)

    k_scale = jnp.concatenate([jnp.ones((MLSTM_QK_W,), F32),
                               jnp.full((MLSTM_QK_W,), MLSTM_DK ** -0.5, F32),
                               jnp.ones((MLSTM_W,), F32)])
    qkv, wt_b = proj_a(n=OFF_AO, bias=b_in[:OFF_AO], scale=k_scale, out_dtype=BF16,
                       side_cast=(W["wt_in"], OFF_BQ, IN_W - OFF_BQ, 256), name="proj_qkv")
    proj_b = functools.partial(_matmul, xb, wt_b, 0, wt=True)
    ao, w_out = proj_a(n=MLSTM_W, col_off=OFF_AO, bias=b_in[OFF_AO:OFF_AI], act="sigmoid",
                       side_cast=(W["w_out"], 0, D_MODEL, 128), name="proj_ogate")
    b_if = jnp.pad(b_in[OFF_AI:OFF_BQ], (0, GATE_PAD - 2 * MLSTM_HEADS))
    gif = proj_a(n=GATE_PAD, col_off=OFF_AI, bias=b_if, bn=GATE_PAD, name="proj_if")
    sec = lambda i: b_in[OFF_BQ + i * SB_W:OFF_BQ + (i + 1) * SB_W]
    bq = proj_b(n=SB_W, bias=sec(0), out_dtype=BF16, name="proj_sb_q")
    k8 = _proj_heads(xb, wt_b, l, sec(1), B_BK, group_rows, kv_prev and kv_prev[0],
                     name="proj_sb_k")
    v8 = _proj_heads(xb, wt_b, l, sec(2), B_BV, group_rows, kv_prev and kv_prev[1],
                     name="proj_sb_v")
    u = proj_b(n=POOL_W, col_off=B_CU, bias=sec(3), name="proj_pool_u")

    ha = hb = hc = None
    states = []
    for gi, g in enumerate(groups):
        dims = dict(bsz=g["bsz"], t=g["t"], row_off=g["row_off"])
        ha, c_new, n_new, m_new = _mlstm(qkv, ao, gif, g["c"], g["n"], g["m"],
                                         W["mlstm_norm_g"][l], l, ha, **dims)
        hb = _stick_breaking(bq, k8[gi], v8[gi], g["sb_k"], g["sb_v"], l, hb, **dims)
        hc = _pool(u, g["pool_buf"][l], W["w_pool"], W["pool_scale"][l], l, hc,
                   pos0=g["pos0"], **dims)
        states.append((c_new, n_new, m_new))

    merged = _merge(xb, wt_b, b_in[OFF_G:].reshape(1, -1), ha, hb, hc,
                    W["w_pa"], W["w_pb"], W["w_pc"], l)
    y = _matmul(merged, w_out, 0, n=D_MODEL, res=x, name="w_out")
    x, xb = _layer_norm(y, W["ln1_g"][l], W["ln1_b"][l])

    qx = _matmul(xb, W["wq_x"], l, n=X_W, out_dtype=BF16, name="wq_x")
    ox = None
    for g in groups:
        ox = _cross_attend(qx, g["mk"], g["mv"], l, ox, bsz=g["bsz"], t=g["t"],
                           row_off=g["row_off"])
    y = _matmul(ox, W["wo_x"], l, n=D_MODEL, res=x, name="wo_x")
    x, xb = _layer_norm(y, W["ln2_g"][l], W["ln2_b"][l])

    hid, w_down = _matmul(xb, W["w_up"], l, n=D_FF, act="relu2", out_dtype=BF16, bn=512,
                          side_cast=(W["w_down"], 0, D_FF, 64), name="w_up")
    y = _matmul(hid, w_down, 0, n=D_MODEL, bk=D_MODEL, name="w_down")
    if last:
        x = tuple(_layer_norm(x, W["ln3_g"][l], W["ln3_b"][l], res=y, in_off=g["row_off"],
                              n_rows=g["bsz"] * g["t"], emit_bf16=False) for g in groups)
        xb = None
    else:
        x, xb = _layer_norm(x, W["ln3_g"][l], W["ln3_b"][l], res=y)
    return x, xb, (k8, v8), u, states


def kernel(x_prompt, x_sample, cache_sb_k, cache_sb_v, state_mlstm_c, state_mlstm_n, state_mlstm_m, state_pool, cache_mem_k, cache_mem_v, mem_prompt, ln_in_g, ln_in_b, w_in, b_in, mlstm_norm_g, w_pool, pool_scale, w_pa, w_pb, w_pc, w_out, ln1_g, ln1_b, wq_x, wk_x, wv_x, wo_x, ln2_g, ln2_b, w_up, w_down, ln3_g, ln3_b):
    bf = lambda a: a.astype(BF16)
    wt_in = jnp.swapaxes(w_in, 1, 2)
    W = dict(wt_in=wt_in,
             b_in=b_in, mlstm_norm_g=mlstm_norm_g, w_pool=bf(w_pool),
             pool_scale=pool_scale, w_pa=bf(w_pa), w_pb=bf(w_pb), w_pc=bf(w_pc), w_out=w_out,
             ln1_g=ln1_g, ln1_b=ln1_b, wq_x=bf(wq_x), wo_x=bf(wo_x), ln2_g=ln2_g, ln2_b=ln2_b,
             w_up=w_up, w_down=w_down, ln3_g=ln3_g, ln3_b=ln3_b)
    bp, tp, _ = x_prompt.shape
    bs, ts, _ = x_sample.shape
    rows_p, rows_s = bp * tp, bs * ts
    rows = rows_p + rows_s
    past = cache_sb_k.shape[2]
    mem_len = mem_prompt.shape[1]

    mem_b = mem_prompt.reshape(bp * mem_len, D_MODEL).astype(BF16)
    wk_b, wv_b = bf(wk_x), bf(wv_x)
    mem4 = lambda a, b: a.reshape(DEPTH, b, mem_len, X_W)
    mk_p = mem4(jnp.stack([_matmul(mem_b, wk_b, l, n=X_W, name="mem_k") for l in range(DEPTH)]), bp)
    mv_p = mem4(jnp.stack([_matmul(mem_b, wv_b, l, n=X_W, name="mem_v") for l in range(DEPTH)]), bp)

    lane_m = lambda m: jnp.broadcast_to(m[..., None], m.shape + (LANES,))
    halo = lambda b: jnp.pad(b, ((0, 0), (0, 0), (POOL_HALO - POOL_BUF, 0), (0, 0)))
    heads_rows = lambda c: c.reshape(DEPTH, bs, past * SB_HEADS, SB_DH)
    groups = [
        dict(bsz=bp, t=tp, row_off=0, pos0=0,
             c=jnp.zeros((DEPTH, bp, MLSTM_HEADS, MLSTM_DK, MLSTM_DV), F32),
             n=jnp.zeros((DEPTH, bp, MLSTM_HEADS, MLSTM_DK), F32),
             m=jnp.zeros((DEPTH, bp, MLSTM_HEADS, LANES), F32),
             sb_k=None, sb_v=None,
             pool_buf=jnp.zeros((DEPTH, bp, POOL_HALO, POOL_W), F32),
             mk=mk_p, mv=mv_p),
        dict(bsz=bs, t=ts, row_off=rows_p, pos0=past,
             c=state_mlstm_c, n=state_mlstm_n, m=lane_m(state_mlstm_m),
             sb_k=heads_rows(cache_sb_k), sb_v=heads_rows(cache_sb_v),
             pool_buf=halo(state_pool),
             mk=mem4(cache_mem_k, bs), mv=mem4(cache_mem_v, bs)),
    ]

    xs = None
    for g, xin in zip(groups, (x_prompt, x_sample)):
        xs = _layer_norm(xin.reshape(g["bsz"] * g["t"], D_MODEL), ln_in_g, ln_in_b,
                         out_off=g["row_off"], out_rows=rows, prev=xs)
    x, xb = xs

    kv = None
    per_layer = []
    for l in range(DEPTH):
        x, xb, kv, u, states = _trunk_layer(x, xb, W, l, groups, kv, last=l == DEPTH - 1)
        per_layer.append((u, states))

    def group_out(gi, g):
        bsz, t, r0 = g["bsz"], g["t"], g["row_off"]
        sbk = kv[0][gi].reshape(DEPTH, bsz, t, SB_HEADS, SB_DH)
        sbv = kv[1][gi].reshape(DEPTH, bsz, t, SB_HEADS, SB_DH)
        c = jnp.stack([st[gi][0] for _, st in per_layer])
        n = jnp.stack([st[gi][1] for _, st in per_layer])
        m = jnp.stack([st[gi][2][..., 0] for _, st in per_layer])
        pool = jnp.stack([u[r0:r0 + bsz * t].reshape(bsz, t, POOL_W)[:, t - POOL_BUF:, :]
                          for u, _ in per_layer])
        return sbk, sbv, c, n, m, pool

    out_p = group_out(0, groups[0])
    out_s = group_out(1, groups[1])
    y_prompt = x[0].reshape(bp, tp, D_MODEL)
    y_sample = x[1].reshape(bs, ts, D_MODEL)
    mem5 = lambda a: a.reshape(DEPTH, bp, mem_len, X_HEADS, X_DH)
    return (y_prompt, y_sample) + out_p + (mem5(mk_p), mem5(mv_p)) + out_s
```

```python
import functools

import jax
import jax.numpy as jnp
from jax import lax
from jax.experimental import pallas as pl
from jax.experimental.pallas import tpu as pltpu

F32 = jnp.float32
BF16 = jnp.bfloat16

D_MODEL = 4096
DEPTH = 2
MLSTM_CHUNK = 256
MLSTM_HEADS = 8
MLSTM_W = D_MODEL // 2
MLSTM_DV = MLSTM_W // MLSTM_HEADS
MLSTM_DK = MLSTM_DV // 2
MLSTM_QK_W = MLSTM_HEADS * MLSTM_DK
SB_HEADS = 8
SB_W = D_MODEL // 4
SB_DH = SB_W // SB_HEADS
POOL_WINDOWS = (2, 4, 8, 16)
POOL_W = D_MODEL // 4
POOL_G = POOL_W // len(POOL_WINDOWS)
POOL_BUF = max(POOL_WINDOWS) - 1
POOL_HALO = POOL_BUF + 1
X_HEADS = 4
X_W = D_MODEL // 4
X_DH = X_W // X_HEADS
D_FF = 4 * D_MODEL
N_BRANCH = 3
ALPHA = (2 * DEPTH) ** 0.25
LN_EPS = 1e-5
HEAD_NORM_EPS = 1e-6

OFF_AO = 2 * MLSTM_QK_W + MLSTM_W
OFF_AI = OFF_AO + MLSTM_W
OFF_BQ = OFF_AI + 2 * MLSTM_HEADS
OFF_G = OFF_BQ + 3 * SB_W + POOL_W
IN_W = OFF_G + N_BRANCH * D_MODEL

V7X_VMEM_LIMIT_BYTES = 56 * 1024 * 1024
LANES = 128
SUBLANES = 8
SB_TQ = 256
GATE_PAD = LANES

B_BK = SB_W
B_BV = 2 * SB_W
B_CU = 3 * SB_W
B_G = 3 * SB_W + POOL_W

_NT = (((1,), (1,)), ((), ()))
_TN = (((0,), (0,)), ((), ()))


def _params(*sem):
    return pltpu.CompilerParams(dimension_semantics=sem,
                                vmem_limit_bytes=V7X_VMEM_LIMIT_BYTES)


def _dot(a, b):
    return jnp.dot(a, b, preferred_element_type=F32)


def _split(x, terms):
    out = []
    for _ in range(terms - 1):
        h = x.astype(BF16)
        out.append(h)
        x = x - h.astype(F32)
    out.append(x.astype(BF16))
    return out


def _alias_prev(prev, n_in):
    if prev is None:
        return [], [], {}
    prev = list(prev) if isinstance(prev, (tuple, list)) else [prev]
    specs = [pl.BlockSpec(memory_space=pl.ANY)] * len(prev)
    return prev, specs, {n_in + i: i for i in range(len(prev))}


def _mm_kernel(*refs, act, has_bias, has_scale, has_res, nk, side_slabs, wt):
    it = iter(refs)
    x_ref, w_ref = next(it), next(it)
    b_ref = next(it) if has_bias else None
    s_ref = next(it) if has_scale else None
    r_ref = next(it) if has_res else None
    side_in_refs = [next(it) for _ in side_slabs]
    o_ref = next(it)
    side_out_refs = [next(it) for _ in side_slabs]
    acc_ref = next(it) if nk > 1 else None

    if side_slabs:
        step = pl.program_id(0) * pl.num_programs(1) + pl.program_id(1)
        for n_slabs, in_ref, out_ref in zip(side_slabs, side_in_refs, side_out_refs):
            @pl.when(step < n_slabs)
            def _(in_ref=in_ref, out_ref=out_ref):
                out_ref[...] = in_ref[0].astype(BF16)

    def epilogue(y):
        if has_bias:
            y = y + b_ref[...]
        if has_scale:
            y = y * s_ref[...]
        if act == "sigmoid":
            y = jax.nn.sigmoid(y)
        elif act == "relu2":
            y = jnp.square(jnp.maximum(y, 0.0))
        if has_res:
            y = ALPHA * r_ref[...] + y
        o_ref[...] = y.astype(o_ref.dtype)

    w = w_ref[...].astype(BF16)
    part = lax.dot_general(x_ref[...], w, _NT if wt else (((1,), (0,)), ((), ())),
                           preferred_element_type=F32)
    if nk == 1:
        epilogue(part)
    else:
        k = pl.program_id(2)

        @pl.when(k == 0)
        def _():
            acc_ref[...] = part

        @pl.when(k > 0)
        def _():
            acc_ref[...] += part

        @pl.when(k == nk - 1)
        def _():
            epilogue(acc_ref[...])


def _matmul(x, w, l, *, n, col_off=0, bias=None, scale=None, res=None, act="none",
            out_dtype=F32, bm=1024, bn=1024, bk=None, side_cast=(), wt=False, name="matmul"):
    m, kdim = x.shape
    bm, bn = min(bm, m), min(bn, n)
    bk = kdim if bk is None else bk
    assert m % bm == 0 and n % bn == 0 and kdim % bk == 0 and col_off % bn == 0
    nk = kdim // bk
    cb = col_off // bn
    nj = n // bn
    w_spec = (pl.BlockSpec((None, bn, bk), lambda i, j, k: (l, cb + j, k)) if wt
              else pl.BlockSpec((None, bk, bn), lambda i, j, k: (l, k, cb + j)))
    in_specs = [pl.BlockSpec((bm, bk), lambda i, j, k: (i, k)), w_spec]
    args = [x, w]
    for v in (bias, scale):
        if v is not None:
            in_specs.append(pl.BlockSpec((1, bn), lambda i, j, k: (0, j)))
            args.append(v.reshape(1, n).astype(F32))
    if res is not None:
        in_specs.append(pl.BlockSpec((bm, bn), lambda i, j, k: (i, j)))
        args.append(res)
    out_shape = [jax.ShapeDtypeStruct((m, n), out_dtype)]
    out_specs = [pl.BlockSpec((bm, bn), lambda i, j, k: (i, j))]
    side_slabs = []
    for side_w, side_l, row_off, sr, slab_rows in side_cast:
        sc = side_w.shape[2]
        n_slabs = sr // slab_rows
        assert nk == 1 and sr % slab_rows == 0 and n_slabs <= (m // bm) * nj
        slab = lambda i, j, k, n_slabs=n_slabs: jnp.minimum(i * nj + j, n_slabs - 1)
        in_specs.append(pl.BlockSpec(
            (pl.Element(1), pl.Element(slab_rows), pl.Element(sc)),
            lambda i, j, k, slab=slab, side_l=side_l, row_off=row_off, slab_rows=slab_rows: (
                side_l, pl.multiple_of(row_off + slab(i, j, k) * slab_rows, SUBLANES), 0)))
        args.append(side_w)
        out_shape.append(jax.ShapeDtypeStruct((1, sr, sc), BF16))
        out_specs.append(pl.BlockSpec((None, slab_rows, sc),
                                      lambda i, j, k, slab=slab: (0, slab(i, j, k), 0)))
        side_slabs.append(n_slabs)
    kern = functools.partial(_mm_kernel, act=act, has_bias=bias is not None,
                             has_scale=scale is not None, has_res=res is not None, nk=nk,
                             side_slabs=tuple(side_slabs), wt=wt)
    out = pl.pallas_call(
        kern,
        out_shape=tuple(out_shape),
        grid=(m // bm, nj, nk),
        in_specs=in_specs,
        out_specs=tuple(out_specs),
        scratch_shapes=[pltpu.VMEM((bm, bn), F32)] if nk > 1 else [],
        compiler_params=(_params("arbitrary", "arbitrary", "arbitrary") if side_slabs
                         else _params("parallel", "parallel", "arbitrary")),
        name=name,
    )(*args)
    return out if side_slabs else out[0]


def _proj_heads_kernel(x_ref, w_ref, b_ref, *rest, n_first):
    first_ref, second_ref = rest[-2:]
    y = lax.dot_general(x_ref[...], w_ref[...], _NT, preferred_element_type=F32) + b_ref[...]
    rows = y.shape[0]

    def store(o_ref):
        for h in range(SUBLANES):
            o_ref[pl.ds(h, rows, stride=SUBLANES), :] = y[:, h * LANES:(h + 1) * LANES]

    i = pl.program_id(0)

    @pl.when(i < n_first)
    def _():
        store(first_ref)

    @pl.when(i >= n_first)
    def _():
        store(second_ref)


def _proj_heads(x, w, l, bias, col_off, group_rows, prev, *, bm=1024, name="proj_heads"):
    m, kdim = x.shape
    n = SUBLANES * LANES
    rows_a, rows_b = group_rows
    assert rows_a % bm == 0 and rows_b % bm == 0 and rows_a + rows_b == m and col_off % n == 0
    n_first = rows_a // bm
    cb = col_off // n
    args = [x, w, bias.reshape(1, n)]
    prev_args, prev_specs, aliases = _alias_prev(prev, len(args))
    blk = (None, bm * SUBLANES, LANES)
    return pl.pallas_call(
        functools.partial(_proj_heads_kernel, n_first=n_first),
        out_shape=tuple(jax.ShapeDtypeStruct((DEPTH, r * SUBLANES, LANES), F32) for r in group_rows),
        grid=(m // bm,),
        in_specs=[pl.BlockSpec((bm, kdim), lambda i: (i, 0)),
                  pl.BlockSpec((None, n, kdim), lambda i: (0, cb, 0)),
                  pl.BlockSpec((1, n), lambda i: (0, 0))] + prev_specs,
        out_specs=(pl.BlockSpec(blk, lambda i: (l, jnp.minimum(i, n_first - 1), 0)),
                   pl.BlockSpec(blk, lambda i: (l, jnp.maximum(i - n_first, 0), 0))),
        input_output_aliases=aliases,
        compiler_params=_params("arbitrary"),
        name=name,
    )(*args, *prev_args)


def _ln_kernel(*refs, has_res, n_prev, emit_bf16, n_blocks):
    it = iter(refs)
    x_ref = next(it)
    r_ref = next(it) if has_res else None
    g_ref, b_ref = next(it), next(it)
    for _ in range(n_prev):
        next(it)
    o_ref = next(it)
    x = x_ref[...]
    if has_res:
        x = ALPHA * x + r_ref[...]
    mu = jnp.mean(x, axis=-1, keepdims=True)
    xc = x - mu
    var = jnp.mean(xc * xc, axis=-1, keepdims=True)
    y = xc * lax.rsqrt(var + LN_EPS) * g_ref[...] + b_ref[...]
    if n_blocks is not None:
        y = jnp.where(pl.program_id(0) < n_blocks, y, 0.0)
    o_ref[...] = y
    if emit_bf16:
        next(it)[...] = y.astype(BF16)


def _layer_norm(x, g, b, res=None, *, in_off=0, n_rows=None, out_off=0, out_rows=None,
                prev=None, emit_bf16=True, rows=256):
    d = x.shape[1]
    n_rows = x.shape[0] - in_off if n_rows is None else n_rows
    out_rows = n_rows if out_rows is None else out_rows
    ib, ob = in_off // rows, out_off // rows
    n_blocks = n_rows // rows
    fill = prev is None and out_rows > n_rows
    assert not fill or out_off == 0
    in_spec = pl.BlockSpec((rows, d), lambda i: (ib + jnp.minimum(i, n_blocks - 1), 0))
    out_spec = pl.BlockSpec((rows, d), lambda i: (ob + i, 0))
    vec_spec = pl.BlockSpec((1, d), lambda i: (0, 0))
    args = [x] + ([res] if res is not None else []) + [g.reshape(1, d), b.reshape(1, d)]
    in_specs = [in_spec] * (2 if res is not None else 1) + [vec_spec, vec_spec]
    prev_args, prev_specs, aliases = _alias_prev(prev, len(args))
    out_shape = [jax.ShapeDtypeStruct((out_rows, d), F32)]
    if emit_bf16:
        out_shape.append(jax.ShapeDtypeStruct((out_rows, d), BF16))
    out = pl.pallas_call(
        functools.partial(_ln_kernel, has_res=res is not None, n_prev=len(prev_args),
                          emit_bf16=emit_bf16, n_blocks=n_blocks if fill else None),
        out_shape=tuple(out_shape),
        grid=(out_rows // rows if fill else n_blocks,),
        in_specs=in_specs + prev_specs,
        out_specs=tuple([out_spec] * len(out_shape)),
        input_output_aliases=aliases,
        compiler_params=_params("parallel"),
        name="layer_norm",
    )(*args, *prev_args)
    return out if emit_bf16 else out[0]


def _mlstm_kernel(q_ref, k_ref, v_ref, ao_ref, gif_ref, c0_ref, n0_ref, m0_ref, g_ref, *rest):
    ha_ref, c_ref, n_ref, m_ref = rest[-4:]
    L = q_ref.shape[0]

    @pl.when(pl.program_id(1) == 0)
    def _():
        c_ref[...] = c0_ref[...]
        n_ref[...] = n0_ref[...]
        m_ref[...] = m0_ref[...]

    gif = gif_ref[...]
    logf = -(jnp.maximum(-gif, 0.0) + jnp.log1p(jnp.exp(-jnp.abs(gif))))
    row = lax.broadcasted_iota(jnp.int32, (L, L), 0)
    col = lax.broadcasted_iota(jnp.int32, (L, L), 1)
    causal = col <= row
    tril = jnp.where(causal, 1.0, 0.0).astype(BF16)
    b_all = sum(_dot(tril, t) for t in _split(logf, 3))
    sel_r = lax.broadcasted_iota(jnp.int32, (2 * MLSTM_HEADS, GATE_PAD), 0)
    sel_c = lax.broadcasted_iota(jnp.int32, (2 * MLSTM_HEADS, GATE_PAD), 1)
    sel = jnp.where(sel_r == sel_c, 1.0, 0.0).astype(BF16)

    def rows_of(x):
        return sum(lax.dot_general(sel, t, _NT, preferred_element_type=F32) for t in _split(x, 3))

    i_rows = rows_of(gif)
    b_rows = rows_of(b_all)

    for h in range(MLSTM_HEADS):
        qs = slice(h * MLSTM_DK, (h + 1) * MLSTM_DK)
        vs = slice(h * MLSTM_DV, (h + 1) * MLSTM_DV)
        q = q_ref[:, qs]
        k = k_ref[:, qs]
        v = v_ref[:, vs]
        i_col = gif[:, h:h + 1]
        b_col = b_all[:, MLSTM_HEADS + h:MLSTM_HEADS + h + 1]
        i_row = i_rows[h:h + 1, :]
        b_row = b_rows[MLSTM_HEADS + h:MLSTM_HEADS + h + 1, :]
        c_old = c_ref[0, h]
        n_old = n_ref[0, h:h + 1, :]
        m_old = m_ref[0, h:h + 1, 0:1]

        dmat = jnp.where(causal, b_col - b_row + i_row, -jnp.inf)
        m_inter = b_col + m_old
        m_t = jnp.maximum(m_inter, jnp.max(dmat, axis=-1, keepdims=True))
        s = lax.dot_general(q, k, _NT, preferred_element_type=F32) * jnp.exp(dmat - m_t)
        w_inter = jnp.exp(m_inter - m_t)
        kf = k.astype(F32)
        num = _dot(s.astype(BF16), v) + w_inter * _dot(q, c_old.astype(BF16))
        qn = jnp.sum(q.astype(F32) * n_old, axis=-1, keepdims=True)
        den = jnp.sum(s, axis=-1, keepdims=True) + w_inter * qn
        hh = num / jnp.maximum(jnp.abs(den), jnp.exp(-m_t))

        m_new = m_t[L - 1:L, :]
        b_last = b_col[L - 1:L, :]
        g_col = jnp.exp(b_last - b_col + i_col - m_new)
        decay = jnp.exp(b_last + m_old - m_new)
        kg = kf * g_col
        c_ref[0, h] = decay * c_old + lax.dot_general(kg.astype(BF16), v, _TN,
                                                      preferred_element_type=F32)
        n_ref[0, h:h + 1, :] = decay * n_old + jnp.sum(kg, axis=0, keepdims=True)
        m_ref[0, h:h + 1, :] = jnp.broadcast_to(m_new, (1, LANES))

        mu = jnp.mean(hh, axis=-1, keepdims=True)
        hc = hh - mu
        var = jnp.mean(hc * hc, axis=-1, keepdims=True)
        hn = hc * lax.rsqrt(var + HEAD_NORM_EPS) * g_ref[:, vs]
        ha_ref[:, vs] = (ao_ref[:, vs] * hn).astype(BF16)


def _mlstm(qkv, ao, gif, c0, n0, m0, norm_g, l, prev, *, bsz, t, row_off):
    chunk = min(MLSTM_CHUNK, t)
    nc = t // chunk
    off = row_off // chunk
    rows = lambda b, c: off + b * nc + c
    state4 = pl.BlockSpec((1, MLSTM_HEADS, MLSTM_DK, MLSTM_DV), lambda b, c: (b, 0, 0, 0))
    state3 = pl.BlockSpec((1, MLSTM_HEADS, LANES), lambda b, c: (b, 0, 0))
    init4 = pl.BlockSpec((None, 1, MLSTM_HEADS, MLSTM_DK, MLSTM_DV), lambda b, c: (l, b, 0, 0, 0))
    init3 = pl.BlockSpec((None, 1, MLSTM_HEADS, LANES), lambda b, c: (l, b, 0, 0))
    args = [qkv, qkv, qkv, ao, gif, c0, n0, m0, norm_g.reshape(1, MLSTM_W)]
    prev_args, prev_specs, aliases = _alias_prev(prev, len(args))
    return pl.pallas_call(
        _mlstm_kernel,
        out_shape=(jax.ShapeDtypeStruct((qkv.shape[0], MLSTM_W), BF16),
                   jax.ShapeDtypeStruct(c0.shape[1:], F32),
                   jax.ShapeDtypeStruct(n0.shape[1:], F32),
                   jax.ShapeDtypeStruct(m0.shape[1:], F32)),
        grid=(bsz, nc),
        in_specs=[
            pl.BlockSpec((chunk, MLSTM_QK_W), lambda b, c: (rows(b, c), 0)),
            pl.BlockSpec((chunk, MLSTM_QK_W), lambda b, c: (rows(b, c), 1)),
            pl.BlockSpec((chunk, MLSTM_W), lambda b, c: (rows(b, c), 1)),
            pl.BlockSpec((chunk, MLSTM_W), lambda b, c: (rows(b, c), 0)),
            pl.BlockSpec((chunk, GATE_PAD), lambda b, c: (rows(b, c), 0)),
            init4, init3, init3,
            pl.BlockSpec((1, MLSTM_W), lambda b, c: (0, 0)),
        ] + prev_specs,
        out_specs=(pl.BlockSpec((chunk, MLSTM_W), lambda b, c: (rows(b, c), 0)),
                   state4, state3, state3),
        input_output_aliases=aliases,
        compiler_params=_params("parallel", "arbitrary"),
        name="mlstm",
    )(*args, *prev_args)


SB_SCALE = SB_DH ** -0.5
SB_DEAD_CARRY = -110.0


def _lower_tri(n):
    return jnp.where(lax.broadcasted_iota(jnp.int32, (n, n), 0)
                     >= lax.broadcasted_iota(jnp.int32, (n, n), 1), 1.0, 0.0).astype(BF16)


def _sb_group(q_ref, load_k, load_v, heads, acc_ref, carry_ref, lower, *, diag):
    tq = q_ref.shape[0]
    tk = lower.shape[0]
    if diag:
        mask = (lax.broadcasted_iota(jnp.int32, (tq, tk), 1)
                < lax.broadcasted_iota(jnp.int32, (tq, tk), 0))
    zs, lks = [], []
    for h in heads:
        z = lax.dot_general(q_ref[:, h * SB_DH:(h + 1) * SB_DH], load_k(h), _NT,
                            preferred_element_type=F32) * SB_SCALE
        log_keep = -(jnp.maximum(z, 0.0) + jnp.log(1.0 + jnp.exp(-jnp.abs(z))))
        if diag:
            log_keep = jnp.where(mask, log_keep, 0.0)
        zs.append(z)
        lks.append(log_keep)
    stacked = jnp.concatenate(lks, axis=0)
    rc_all = sum(_dot(t, lower) for t in _split(stacked, 2))
    for i, h in enumerate(heads):
        rc = rc_all[i * tq:(i + 1) * tq]
        carry = carry_ref[h]
        attn = jnp.exp(zs[i] + rc + carry)
        if diag:
            attn = jnp.where(mask, attn, 0.0)
        acc_ref[h] += _dot(attn.astype(BF16), load_v(h))
        carry_ref[h] = carry + rc[:, 0:1]


def _sb_block(q_ref, k_ref, v_ref, base, tk, acc_ref, carry_ref, lower, *, group, diag):
    def loader(ref):
        return lambda h: ref[pl.ds(base + h, tk, stride=SB_HEADS), :].astype(BF16)
    for g in range(0, SB_HEADS, group):
        _sb_group(q_ref, loader(k_ref), loader(v_ref), range(g, g + group),
                  acc_ref, carry_ref, lower, diag=diag)


def _sb_alive(carry_ref):
    return jnp.max(carry_ref[...]) > SB_DEAD_CARRY


def _sb_store(o_ref, acc_ref):
    for h in range(SB_HEADS):
        o_ref[:, h * SB_DH:(h + 1) * SB_DH] = acc_ref[h].astype(BF16)


def _sb_prompt_kernel(q_ref, k_ref, v_ref, *rest, tq, group):
    o_ref, acc_ref, carry_ref = rest[-3:]
    qi = pl.program_id(1)
    acc_ref[...] = jnp.zeros_like(acc_ref)
    carry_ref[...] = jnp.zeros_like(carry_ref)
    lower = _lower_tri(tq)
    blk_rows = tq * SB_HEADS

    _sb_block(q_ref, k_ref, v_ref, pl.multiple_of(qi * blk_rows, blk_rows), tq,
              acc_ref, carry_ref, lower, group=group, diag=True)

    def body(j, c):
        @pl.when(_sb_alive(carry_ref))
        def _():
            base = pl.multiple_of((qi - 1 - j) * blk_rows, blk_rows)
            _sb_block(q_ref, k_ref, v_ref, base, tq, acc_ref, carry_ref, lower,
                      group=group, diag=False)
        return c

    lax.fori_loop(0, qi, body, 0)
    _sb_store(o_ref, acc_ref)


def _sb_sample_kernel(q_ref, k_ref, v_ref, kp_ref, vp_ref, *rest, tq, n_past):
    o_ref, acc_ref, carry_ref = rest[-3:]
    s = pl.program_id(1)

    @pl.when(s == 0)
    def _():
        acc_ref[...] = jnp.zeros_like(acc_ref)
        carry_ref[...] = jnp.zeros_like(carry_ref)
        _sb_block(q_ref, k_ref, v_ref, 0, tq, acc_ref, carry_ref, _lower_tri(tq),
                  group=SB_HEADS, diag=True)

    @pl.when(jnp.logical_and(s > 0, _sb_alive(carry_ref)))
    def _():
        _sb_block(q_ref, kp_ref, vp_ref, 0, SB_TQ, acc_ref, carry_ref, _lower_tri(SB_TQ),
                  group=SB_HEADS, diag=False)

    @pl.when(s == n_past)
    def _():
        _sb_store(o_ref, acc_ref)


def _stick_breaking(bq, k8, v8, past_k, past_v, l, prev, *, bsz, t, row_off):
    tq = min(SB_TQ, t)
    nq = t // tq
    off_q = row_off // tq
    scratch = [pltpu.VMEM((SB_HEADS, tq, SB_DH), F32), pltpu.VMEM((SB_HEADS, tq, 1), F32)]
    out_shape = jax.ShapeDtypeStruct((bq.shape[0], SB_W), BF16)
    q_spec = pl.BlockSpec((tq, SB_W), lambda b, i: (off_q + b * nq + i, 0))
    if past_k is None:
        seq_spec = pl.BlockSpec((None, t * SB_HEADS, SB_DH), lambda b, i: (l, b, 0))
        args = [bq, k8, v8]
        prev_args, prev_specs, aliases = _alias_prev(prev, len(args))
        return pl.pallas_call(
            functools.partial(_sb_prompt_kernel, tq=tq, group=SB_HEADS // 2),
            out_shape=out_shape,
            grid=(bsz, nq),
            in_specs=[q_spec, seq_spec, seq_spec] + prev_specs,
            out_specs=q_spec,
            scratch_shapes=scratch,
            input_output_aliases=aliases,
            compiler_params=_params("parallel", "arbitrary"),
            name="stick_breaking_prompt",
        )(*args, *prev_args)
    assert nq == 1
    blk = SB_TQ * SB_HEADS
    n_past = past_k.shape[2] // blk
    q_spec = pl.BlockSpec((tq, SB_W), lambda b, s: (off_q + b, 0))
    new_spec = pl.BlockSpec((None, t * SB_HEADS, SB_DH), lambda b, s: (l, b, 0))
    past_spec = pl.BlockSpec((None, None, blk, SB_DH),
                             lambda b, s: (l, b, n_past - jnp.maximum(s, 1), 0))
    args = [bq, k8, v8, past_k, past_v]
    prev_args, prev_specs, aliases = _alias_prev(prev, len(args))
    return pl.pallas_call(
        functools.partial(_sb_sample_kernel, tq=tq, n_past=n_past),
        out_shape=out_shape,
        grid=(bsz, n_past + 1),
        in_specs=[q_spec, new_spec, new_spec, past_spec, past_spec] + prev_specs,
        out_specs=q_spec,
        scratch_shapes=scratch,
        input_output_aliases=aliases,
        compiler_params=_params("parallel", "arbitrary"),
        name="stick_breaking_sample",
    )(*args, *prev_args)


def _pool_kernel(u_ref, buf_ref, w_ref, s_ref, *rest, tt, pos0):
    o_ref, ext_ref = rest[-2:]
    i = pl.program_id(1)

    @pl.when(i == 0)
    def _():
        ext_ref[0:POOL_HALO, :] = buf_ref[0]

    @pl.when(i > 0)
    def _():
        ext_ref[0:POOL_HALO, :] = ext_ref[tt:tt + POOL_HALO, :]

    ext_ref[POOL_HALO:POOL_HALO + tt, :] = u_ref[...]
    pos = pos0 + i * tt + lax.broadcasted_iota(jnp.int32, (tt, 1), 0)
    for gi, w in enumerate(POOL_WINDOWS):
        cs = slice(gi * POOL_G, (gi + 1) * POOL_G)
        cur = ext_ref[POOL_HALO:POOL_HALO + tt, cs]
        wsum = cur
        for j in range(1, w):
            wsum = wsum + ext_ref[POOL_HALO - j:POOL_HALO - j + tt, cs]
        cnt = jnp.minimum(pos + 1, w).astype(F32)
        d = (wsum / cnt - cur).astype(BF16)
        y = _dot(d, w_ref[gi]) * s_ref[:, cs]
        o_ref[:, cs] = y.astype(BF16)


def _pool(u, buf, w_pool, scale, l, prev, *, bsz, t, row_off, pos0):
    tt = min(256, t)
    nt = t // tt
    off = row_off // tt
    row_spec = pl.BlockSpec((tt, POOL_W), lambda b, i: (off + b * nt + i, 0))
    args = [u, buf, w_pool, scale.reshape(1, POOL_W)]
    prev_args, prev_specs, aliases = _alias_prev(prev, len(args))
    return pl.pallas_call(
        functools.partial(_pool_kernel, tt=tt, pos0=pos0),
        out_shape=jax.ShapeDtypeStruct((u.shape[0], POOL_W), BF16),
        grid=(bsz, nt),
        in_specs=[
            row_spec,
            pl.BlockSpec((1, POOL_HALO, POOL_W), lambda b, i: (b, 0, 0)),
            pl.BlockSpec((None, len(POOL_WINDOWS), POOL_G, POOL_G), lambda b, i: (l, 0, 0, 0)),
            pl.BlockSpec((1, POOL_W), lambda b, i: (0, 0)),
        ] + prev_specs,
        out_specs=row_spec,
        scratch_shapes=[pltpu.VMEM((POOL_HALO + tt, POOL_W), F32)],
        input_output_aliases=aliases,
        compiler_params=_params("parallel", "arbitrary"),
        name="pool_mix",
    )(*args, *prev_args)


X_SCALE = X_DH ** -0.5


def _cross_kernel(q_ref, k_ref, v_ref, *rest):
    o_ref = rest[-1]
    for h in range(X_HEADS):
        cs = slice(h * X_DH, (h + 1) * X_DH)
        k = k_ref[0, :, cs].astype(BF16)
        v = v_ref[0, :, cs].astype(BF16)
        s = lax.dot_general(q_ref[:, cs], k, _NT, preferred_element_type=F32) * X_SCALE
        e = jnp.exp(s - jnp.max(s, axis=-1, keepdims=True))
        p = e / jnp.sum(e, axis=-1, keepdims=True)
        o_ref[:, cs] = _dot(p.astype(BF16), v).astype(BF16)


def _cross_attend(qx, mk, mv, l, prev, *, bsz, t, row_off):
    tq = min(256, t)
    nq = t // tq
    off = row_off // tq
    row_spec = pl.BlockSpec((tq, X_W), lambda b, i: (off + b * nq + i, 0))
    mem_spec = pl.BlockSpec((None, 1, mk.shape[2], X_W), lambda b, i: (l, b, 0, 0))
    args = [qx, mk, mv]
    prev_args, prev_specs, aliases = _alias_prev(prev, len(args))
    return pl.pallas_call(
        _cross_kernel,
        out_shape=jax.ShapeDtypeStruct((qx.shape[0], X_W), BF16),
        grid=(bsz, nq),
        in_specs=[row_spec, mem_spec, mem_spec] + prev_specs,
        out_specs=row_spec,
        input_output_aliases=aliases,
        compiler_params=_params("parallel", "arbitrary"),
        name="cross_attend",
    )(*args, *prev_args)


def _merge_kernel(x_ref, wg0_ref, wg1_ref, wg2_ref, bg0_ref, bg1_ref, bg2_ref,
                  ha_ref, hb_ref, hc_ref, wa_ref, wb_ref, wc_ref, o_ref):
    x = x_ref[...]
    out = None
    for wg_ref, bg_ref, h_ref, wp_ref in ((wg0_ref, bg0_ref, ha_ref, wa_ref),
                                          (wg1_ref, bg1_ref, hb_ref, wb_ref),
                                          (wg2_ref, bg2_ref, hc_ref, wc_ref)):
        gate = jax.nn.sigmoid(lax.dot_general(x, wg_ref[...], _NT, preferred_element_type=F32)
                              + bg_ref[...])
        term = gate * _dot(h_ref[...], wp_ref[...])
        out = term if out is None else out + term
    o_ref[...] = out.astype(BF16)


def _merge(xb, wt_b, bg, ha, hb, hc, w_pa, w_pb, w_pc, *, bm=512, bn=256):
    m = xb.shape[0]
    nb = D_MODEL // bn
    gb = B_G // bn
    row = lambda width: pl.BlockSpec((bm, width), lambda i, j: (i, 0))
    colw = lambda kdim: pl.BlockSpec((None, kdim, bn), lambda i, j: (0, 0, j))
    gate_w = [pl.BlockSpec((None, bn, D_MODEL), lambda i, j, g=g: (0, gb + g * nb + j, 0))
              for g in range(N_BRANCH)]
    gate_b = [pl.BlockSpec((1, bn), lambda i, j, g=g: (0, g * nb + j)) for g in range(N_BRANCH)]
    return pl.pallas_call(
        _merge_kernel,
        out_shape=jax.ShapeDtypeStruct((m, D_MODEL), BF16),
        grid=(m // bm, nb),
        in_specs=[row(D_MODEL)] + gate_w + gate_b
                 + [row(MLSTM_W), row(SB_W), row(POOL_W), colw(MLSTM_W), colw(SB_W), colw(POOL_W)],
        out_specs=pl.BlockSpec((bm, bn), lambda i, j: (i, j)),
        compiler_params=_params("parallel", "arbitrary"),
        name="gated_merge",
    )(xb, wt_b, wt_b, wt_b, bg, bg, bg, ha, hb, hc, w_pa, w_pb, w_pc)


def _trunk_layer(x, xb, W, l, groups, kv_prev, last):
    b_in = W["b_in"][l]
    proj_a = functools.partial(_matmul, xb, W["wt_in"], l, wt=True, bn=512)
    group_rows = tuple(g["bsz"] * g["t"] for g in groups)

    k_scale = jnp.concatenate([jnp.ones((MLSTM_QK_W,), F32),
                               jnp.full((MLSTM_QK_W,), MLSTM_DK ** -0.5, F32),
                               jnp.ones((MLSTM_W,), F32)])
    qkv, wt_b = proj_a(n=OFF_AO, bias=b_in[:OFF_AO], scale=k_scale, out_dtype=BF16,
                       side_cast=[(W["wt_in"], l, OFF_BQ, IN_W - OFF_BQ, 256)], name="proj_qkv")
    proj_b = functools.partial(_matmul, xb, wt_b, 0, wt=True)
    whole = lambda name, slab: (W[name], l, 0, W[name].shape[1], slab)
    ao, w_out = proj_a(n=MLSTM_W, col_off=OFF_AO, bias=b_in[OFF_AO:OFF_AI], act="sigmoid",
                       side_cast=[whole("w_out", 128)], name="proj_ogate")
    b_if = jnp.pad(b_in[OFF_AI:OFF_BQ], (0, GATE_PAD - 2 * MLSTM_HEADS))
    gif, w_pa, w_pb, w_pc = proj_a(
        n=GATE_PAD, col_off=OFF_AI, bias=b_if, bn=GATE_PAD,
        side_cast=[whole("w_pa", 256), whole("w_pb", 128), whole("w_pc", 128)], name="proj_if")
    sec = lambda i: b_in[OFF_BQ + i * SB_W:OFF_BQ + (i + 1) * SB_W]
    bq, wq_x, wo_x = proj_b(n=SB_W, bias=sec(0), out_dtype=BF16,
                            side_cast=[whole("wq_x", 512), whole("wo_x", 128)], name="proj_sb_q")
    k8 = _proj_heads(xb, wt_b, l, sec(1), B_BK, group_rows, kv_prev[0], name="proj_sb_k")
    v8 = _proj_heads(xb, wt_b, l, sec(2), B_BV, group_rows, kv_prev[1], name="proj_sb_v")
    u = proj_b(n=POOL_W, col_off=B_CU, bias=sec(3), name="proj_pool_u")

    rows = xb.shape[0]
    ha, hb, hc = (jnp.zeros((rows, w), BF16) for w in (MLSTM_W, SB_W, POOL_W))
    states = []
    for gi, g in enumerate(groups):
        dims = dict(bsz=g["bsz"], t=g["t"], row_off=g["row_off"])
        ha, c_new, n_new, m_new = _mlstm(qkv, ao, gif, g["c"], g["n"], g["m"],
                                         W["mlstm_norm_g"][l], l, ha, **dims)
        hb = _stick_breaking(bq, k8[gi], v8[gi], g["sb_k"], g["sb_v"], l, hb, **dims)
        hc = _pool(u, g["pool_buf"][l], W["w_pool"], W["pool_scale"][l], l, hc,
                   pos0=g["pos0"], **dims)
        states.append((c_new, n_new, m_new))

    merged = _merge(xb, wt_b, b_in[OFF_G:].reshape(1, -1), ha, hb, hc,
                    w_pa, w_pb, w_pc)
    y = _matmul(merged, w_out, 0, n=D_MODEL, res=x, name="w_out")
    x, xb = _layer_norm(y, W["ln1_g"][l], W["ln1_b"][l])

    qx = _matmul(xb, wq_x, 0, n=X_W, out_dtype=BF16, name="wq_x")
    ox = jnp.zeros((rows, X_W), BF16)
    for g in groups:
        ox = _cross_attend(qx, g["mk"], g["mv"], l, ox, bsz=g["bsz"], t=g["t"],
                           row_off=g["row_off"])
    y = _matmul(ox, wo_x, 0, n=D_MODEL, res=x, name="wo_x")
    x, xb = _layer_norm(y, W["ln2_g"][l], W["ln2_b"][l])

    hid, w_down = _matmul(xb, W["w_up"], l, n=D_FF, act="relu2", out_dtype=BF16, bn=512,
                          side_cast=[whole("w_down", 64)], name="w_up")
    y = _matmul(hid, w_down, 0, n=D_MODEL, bk=D_MODEL, name="w_down")
    if last:
        x = tuple(_layer_norm(x, W["ln3_g"][l], W["ln3_b"][l], res=y, in_off=g["row_off"],
                              n_rows=g["bsz"] * g["t"], emit_bf16=False) for g in groups)
        xb = None
    else:
        x, xb = _layer_norm(x, W["ln3_g"][l], W["ln3_b"][l], res=y)
    return x, xb, (k8, v8), u, states


def kernel(x_prompt, x_sample, cache_sb_k, cache_sb_v, state_mlstm_c, state_mlstm_n, state_mlstm_m, state_pool, cache_mem_k, cache_mem_v, mem_prompt, ln_in_g, ln_in_b, w_in, b_in, mlstm_norm_g, w_pool, pool_scale, w_pa, w_pb, w_pc, w_out, ln1_g, ln1_b, wq_x, wk_x, wv_x, wo_x, ln2_g, ln2_b, w_up, w_down, ln3_g, ln3_b):
    bf = lambda a: a.astype(BF16)
    wt_in = jnp.swapaxes(w_in, 1, 2)
    W = dict(wt_in=wt_in,
             b_in=b_in, mlstm_norm_g=mlstm_norm_g, w_pool=bf(w_pool),
             pool_scale=pool_scale, w_pa=w_pa, w_pb=w_pb, w_pc=w_pc, w_out=w_out,
             ln1_g=ln1_g, ln1_b=ln1_b, wq_x=wq_x, wo_x=wo_x, ln2_g=ln2_g, ln2_b=ln2_b,
             w_up=w_up, w_down=w_down, ln3_g=ln3_g, ln3_b=ln3_b)
    bp, tp, _ = x_prompt.shape
    bs, ts, _ = x_sample.shape
    rows_p, rows_s = bp * tp, bs * ts
    rows = rows_p + rows_s
    past = cache_sb_k.shape[2]
    mem_len = mem_prompt.shape[1]

    mem_b = mem_prompt.reshape(bp * mem_len, D_MODEL).astype(BF16)
    wk_b, wv_b = bf(wk_x), bf(wv_x)
    mem4 = lambda a, b: a.reshape(DEPTH, b, mem_len, X_W)
    mk_p = mem4(jnp.stack([_matmul(mem_b, wk_b, l, n=X_W, name="mem_k") for l in range(DEPTH)]), bp)
    mv_p = mem4(jnp.stack([_matmul(mem_b, wv_b, l, n=X_W, name="mem_v") for l in range(DEPTH)]), bp)

    lane_m = lambda m: jnp.broadcast_to(m[..., None], m.shape + (LANES,))
    halo = lambda b: jnp.pad(b, ((0, 0), (0, 0), (POOL_HALO - POOL_BUF, 0), (0, 0)))
    heads_rows = lambda c: c.reshape(DEPTH, bs, past * SB_HEADS, SB_DH)
    groups = [
        dict(bsz=bp, t=tp, row_off=0, pos0=0,
             c=jnp.zeros((DEPTH, bp, MLSTM_HEADS, MLSTM_DK, MLSTM_DV), F32),
             n=jnp.zeros((DEPTH, bp, MLSTM_HEADS, MLSTM_DK), F32),
             m=jnp.zeros((DEPTH, bp, MLSTM_HEADS, LANES), F32),
             sb_k=None, sb_v=None,
             pool_buf=jnp.zeros((DEPTH, bp, POOL_HALO, POOL_W), F32),
             mk=mk_p, mv=mv_p),
        dict(bsz=bs, t=ts, row_off=rows_p, pos0=past,
             c=state_mlstm_c, n=state_mlstm_n, m=lane_m(state_mlstm_m),
             sb_k=heads_rows(cache_sb_k), sb_v=heads_rows(cache_sb_v),
             pool_buf=halo(state_pool),
             mk=mem4(cache_mem_k, bs), mv=mem4(cache_mem_v, bs)),
    ]

    xs = None
    for g, xin in zip(groups, (x_prompt, x_sample)):
        xs = _layer_norm(xin.reshape(g["bsz"] * g["t"], D_MODEL), ln_in_g, ln_in_b,
                         out_off=g["row_off"], out_rows=rows, prev=xs)
    x, xb = xs

    kv = tuple(tuple(jnp.zeros((DEPTH, g["bsz"] * g["t"] * SB_HEADS, SB_DH), F32) for g in groups)
               for _ in range(2))
    per_layer = []
    for l in range(DEPTH):
        x, xb, kv, u, states = _trunk_layer(x, xb, W, l, groups, kv, last=l == DEPTH - 1)
        per_layer.append((u, states))

    def group_out(gi, g):
        bsz, t, r0 = g["bsz"], g["t"], g["row_off"]
        sbk = kv[0][gi].reshape(DEPTH, bsz, t, SB_HEADS, SB_DH)
        sbv = kv[1][gi].reshape(DEPTH, bsz, t, SB_HEADS, SB_DH)
        c = jnp.stack([st[gi][0] for _, st in per_layer])
        n = jnp.stack([st[gi][1] for _, st in per_layer])
        m = jnp.stack([st[gi][2][..., 0] for _, st in per_layer])
        pool = jnp.stack([u[r0:r0 + bsz * t].reshape(bsz, t, POOL_W)[:, t - POOL_BUF:, :]
                          for u, _ in per_layer])
        return sbk, sbv, c, n, m, pool

    out_p = group_out(0, groups[0])
    out_s = group_out(1, groups[1])
    y_prompt = x[0].reshape(bp, tp, D_MODEL)
    y_sample = x[1].reshape(bs, ts, D_MODEL)
    mem5 = lambda a: a.reshape(DEPTH, bp, mem_len, X_HEADS, X_DH)
    return (y_prompt, y_sample) + out_p + (mem5(mk_p), mem5(mv_p)) + out_s
```

```python
import functools

import jax
import jax.numpy as jnp
from jax import lax
from jax.experimental import pallas as pl
from jax.experimental.pallas import tpu as pltpu

F32 = jnp.float32
BF16 = jnp.bfloat16

D_MODEL = 4096
DEPTH = 2
MLSTM_CHUNK = 256
MLSTM_HEADS = 8
MLSTM_W = D_MODEL // 2
MLSTM_DV = MLSTM_W // MLSTM_HEADS
MLSTM_DK = MLSTM_DV // 2
MLSTM_QK_W = MLSTM_HEADS * MLSTM_DK
SB_HEADS = 8
SB_W = D_MODEL // 4
SB_DH = SB_W // SB_HEADS
POOL_WINDOWS = (2, 4, 8, 16)
POOL_W = D_MODEL // 4
POOL_G = POOL_W // len(POOL_WINDOWS)
POOL_BUF = max(POOL_WINDOWS) - 1
POOL_HALO = POOL_BUF + 1
X_HEADS = 4
X_W = D_MODEL // 4
X_DH = X_W // X_HEADS
D_FF = 4 * D_MODEL
N_BRANCH = 3
ALPHA = (2 * DEPTH) ** 0.25
LN_EPS = 1e-5
HEAD_NORM_EPS = 1e-6

OFF_AO = 2 * MLSTM_QK_W + MLSTM_W
OFF_AI = OFF_AO + MLSTM_W
OFF_BQ = OFF_AI + 2 * MLSTM_HEADS
OFF_G = OFF_BQ + 3 * SB_W + POOL_W
IN_W = OFF_G + N_BRANCH * D_MODEL

V7X_VMEM_LIMIT_BYTES = 56 * 1024 * 1024
LANES = 128
SUBLANES = 8
SB_TQ = 256
GATE_PAD = LANES

B_BK = SB_W
B_BV = 2 * SB_W
B_CU = 3 * SB_W
B_G = 3 * SB_W + POOL_W

_NT = (((1,), (1,)), ((), ()))
_TN = (((0,), (0,)), ((), ()))


def _params(*sem):
    return pltpu.CompilerParams(dimension_semantics=sem,
                                vmem_limit_bytes=V7X_VMEM_LIMIT_BYTES)


def _dot(a, b):
    return jnp.dot(a, b, preferred_element_type=F32)


def _split(x, terms):
    out = []
    for _ in range(terms - 1):
        h = x.astype(BF16)
        out.append(h)
        x = x - h.astype(F32)
    out.append(x.astype(BF16))
    return out


def _alias_prev(prev, n_in):
    if prev is None:
        return [], [], {}
    prev = list(prev) if isinstance(prev, (tuple, list)) else [prev]
    specs = [pl.BlockSpec(memory_space=pl.ANY)] * len(prev)
    return prev, specs, {n_in + i: i for i in range(len(prev))}


def _mm_kernel(*refs, act, has_bias, has_scale, has_res, nk, side_slabs, wt):
    it = iter(refs)
    x_ref, w_ref = next(it), next(it)
    b_ref = next(it) if has_bias else None
    s_ref = next(it) if has_scale else None
    r_ref = next(it) if has_res else None
    side_in_refs = [next(it) for _ in side_slabs]
    o_ref = next(it)
    side_out_refs = [next(it) for _ in side_slabs]
    acc_ref = next(it) if nk > 1 else None

    if side_slabs:
        step = pl.program_id(0) * pl.num_programs(1) + pl.program_id(1)
        for n_slabs, in_ref, out_ref in zip(side_slabs, side_in_refs, side_out_refs):
            @pl.when(step < n_slabs)
            def _(in_ref=in_ref, out_ref=out_ref):
                out_ref[...] = in_ref[0].astype(BF16)

    def epilogue(y):
        if has_bias:
            y = y + b_ref[...]
        if has_scale:
            y = y * s_ref[...]
        if act == "sigmoid":
            y = jax.nn.sigmoid(y)
        elif act == "relu2":
            y = jnp.square(jnp.maximum(y, 0.0))
        if has_res:
            y = ALPHA * r_ref[...] + y
        o_ref[...] = y.astype(o_ref.dtype)

    w = w_ref[...].astype(BF16)
    part = lax.dot_general(x_ref[...], w, _NT if wt else (((1,), (0,)), ((), ())),
                           preferred_element_type=F32)
    if nk == 1:
        epilogue(part)
    else:
        k = pl.program_id(2)

        @pl.when(k == 0)
        def _():
            acc_ref[...] = part

        @pl.when(k > 0)
        def _():
            acc_ref[...] += part

        @pl.when(k == nk - 1)
        def _():
            epilogue(acc_ref[...])


def _matmul(x, w, l, *, n, col_off=0, bias=None, scale=None, res=None, act="none",
            out_dtype=F32, bm=1024, bn=1024, bk=None, side_cast=(), wt=False, name="matmul"):
    m, kdim = x.shape
    bm, bn = min(bm, m), min(bn, n)
    bk = kdim if bk is None else bk
    assert m % bm == 0 and n % bn == 0 and kdim % bk == 0 and col_off % bn == 0
    nk = kdim // bk
    cb = col_off // bn
    nj = n // bn
    w_spec = (pl.BlockSpec((None, bn, bk), lambda i, j, k: (l, cb + j, k)) if wt
              else pl.BlockSpec((None, bk, bn), lambda i, j, k: (l, k, cb + j)))
    in_specs = [pl.BlockSpec((bm, bk), lambda i, j, k: (i, k)), w_spec]
    args = [x, w]
    for v in (bias, scale):
        if v is not None:
            in_specs.append(pl.BlockSpec((1, bn), lambda i, j, k: (0, j)))
            args.append(v.reshape(1, n).astype(F32))
    if res is not None:
        in_specs.append(pl.BlockSpec((bm, bn), lambda i, j, k: (i, j)))
        args.append(res)
    out_shape = [jax.ShapeDtypeStruct((m, n), out_dtype)]
    out_specs = [pl.BlockSpec((bm, bn), lambda i, j, k: (i, j))]
    side_slabs = []
    for side_w, side_l, row_off, sr, slab_rows in side_cast:
        sc = side_w.shape[2]
        n_slabs = sr // slab_rows
        assert nk == 1 and sr % slab_rows == 0 and n_slabs <= (m // bm) * nj
        slab = lambda i, j, k, n_slabs=n_slabs: jnp.minimum(i * nj + j, n_slabs - 1)
        in_specs.append(pl.BlockSpec(
            (pl.Element(1), pl.Element(slab_rows), pl.Element(sc)),
            lambda i, j, k, slab=slab, side_l=side_l, row_off=row_off, slab_rows=slab_rows: (
                side_l, pl.multiple_of(row_off + slab(i, j, k) * slab_rows, SUBLANES), 0)))
        args.append(side_w)
        out_shape.append(jax.ShapeDtypeStruct((1, sr, sc), BF16))
        out_specs.append(pl.BlockSpec((None, slab_rows, sc),
                                      lambda i, j, k, slab=slab: (0, slab(i, j, k), 0)))
        side_slabs.append(n_slabs)
    kern = functools.partial(_mm_kernel, act=act, has_bias=bias is not None,
                             has_scale=scale is not None, has_res=res is not None, nk=nk,
                             side_slabs=tuple(side_slabs), wt=wt)
    out = pl.pallas_call(
        kern,
        out_shape=tuple(out_shape),
        grid=(m // bm, nj, nk),
        in_specs=in_specs,
        out_specs=tuple(out_specs),
        scratch_shapes=[pltpu.VMEM((bm, bn), F32)] if nk > 1 else [],
        compiler_params=(_params("arbitrary", "arbitrary", "arbitrary") if side_slabs
                         else _params("parallel", "parallel", "arbitrary")),
        name=name,
    )(*args)
    return out if side_slabs else out[0]


def _proj_heads_kernel(x_ref, w_ref, b_ref, *rest, n_first):
    first_ref, second_ref = rest[-2:]
    y = lax.dot_general(x_ref[...], w_ref[...], _NT, preferred_element_type=F32) + b_ref[...]
    rows = y.shape[0]

    def store(o_ref):
        for h in range(SUBLANES):
            o_ref[pl.ds(h, rows, stride=SUBLANES), :] = y[:, h * LANES:(h + 1) * LANES]

    i = pl.program_id(0)

    @pl.when(i < n_first)
    def _():
        store(first_ref)

    @pl.when(i >= n_first)
    def _():
        store(second_ref)


def _proj_heads(x, w, l, bias, col_off, group_rows, prev, *, bm=1024, name="proj_heads"):
    m, kdim = x.shape
    n = SUBLANES * LANES
    rows_a, rows_b = group_rows
    assert rows_a % bm == 0 and rows_b % bm == 0 and rows_a + rows_b == m and col_off % n == 0
    n_first = rows_a // bm
    cb = col_off // n
    args = [x, w, bias.reshape(1, n)]
    prev_args, prev_specs, aliases = _alias_prev(prev, len(args))
    blk = (None, bm * SUBLANES, LANES)
    return pl.pallas_call(
        functools.partial(_proj_heads_kernel, n_first=n_first),
        out_shape=tuple(jax.ShapeDtypeStruct((DEPTH, r * SUBLANES, LANES), F32) for r in group_rows),
        grid=(m // bm,),
        in_specs=[pl.BlockSpec((bm, kdim), lambda i: (i, 0)),
                  pl.BlockSpec((None, n, kdim), lambda i: (0, cb, 0)),
                  pl.BlockSpec((1, n), lambda i: (0, 0))] + prev_specs,
        out_specs=(pl.BlockSpec(blk, lambda i: (l, jnp.minimum(i, n_first - 1), 0)),
                   pl.BlockSpec(blk, lambda i: (l, jnp.maximum(i - n_first, 0), 0))),
        input_output_aliases=aliases,
        compiler_params=_params("arbitrary"),
        name=name,
    )(*args, *prev_args)


def _ln_kernel(*refs, has_res, n_prev, emit_bf16, n_blocks):
    it = iter(refs)
    x_ref = next(it)
    r_ref = next(it) if has_res else None
    g_ref, b_ref = next(it), next(it)
    for _ in range(n_prev):
        next(it)
    o_ref = next(it)
    x = x_ref[...]
    if has_res:
        x = ALPHA * x + r_ref[...]
    mu = jnp.mean(x, axis=-1, keepdims=True)
    xc = x - mu
    var = jnp.mean(xc * xc, axis=-1, keepdims=True)
    y = xc * lax.rsqrt(var + LN_EPS) * g_ref[...] + b_ref[...]
    if n_blocks is not None:
        y = jnp.where(pl.program_id(0) < n_blocks, y, 0.0)
    o_ref[...] = y
    if emit_bf16:
        next(it)[...] = y.astype(BF16)


def _layer_norm(x, g, b, res=None, *, in_off=0, n_rows=None, out_off=0, out_rows=None,
                prev=None, emit_bf16=True, rows=256):
    d = x.shape[1]
    n_rows = x.shape[0] - in_off if n_rows is None else n_rows
    out_rows = n_rows if out_rows is None else out_rows
    ib, ob = in_off // rows, out_off // rows
    n_blocks = n_rows // rows
    fill = prev is None and out_rows > n_rows
    assert not fill or out_off == 0
    in_spec = pl.BlockSpec((rows, d), lambda i: (ib + jnp.minimum(i, n_blocks - 1), 0))
    out_spec = pl.BlockSpec((rows, d), lambda i: (ob + i, 0))
    vec_spec = pl.BlockSpec((1, d), lambda i: (0, 0))
    args = [x] + ([res] if res is not None else []) + [g.reshape(1, d), b.reshape(1, d)]
    in_specs = [in_spec] * (2 if res is not None else 1) + [vec_spec, vec_spec]
    prev_args, prev_specs, aliases = _alias_prev(prev, len(args))
    out_shape = [jax.ShapeDtypeStruct((out_rows, d), F32)]
    if emit_bf16:
        out_shape.append(jax.ShapeDtypeStruct((out_rows, d), BF16))
    out = pl.pallas_call(
        functools.partial(_ln_kernel, has_res=res is not None, n_prev=len(prev_args),
                          emit_bf16=emit_bf16, n_blocks=n_blocks if fill else None),
        out_shape=tuple(out_shape),
        grid=(out_rows // rows if fill else n_blocks,),
        in_specs=in_specs + prev_specs,
        out_specs=tuple([out_spec] * len(out_shape)),
        input_output_aliases=aliases,
        compiler_params=_params("parallel"),
        name="layer_norm",
    )(*args, *prev_args)
    return out if emit_bf16 else out[0]


def _mlstm_kernel(q_ref, k_ref, v_ref, ao_ref, gif_ref, c0_ref, n0_ref, m0_ref, g_ref, *rest):
    ha_ref, c_ref, n_ref, m_ref = rest[-4:]
    L = q_ref.shape[0]

    @pl.when(pl.program_id(1) == 0)
    def _():
        c_ref[...] = c0_ref[...]
        n_ref[...] = n0_ref[...]
        m_ref[...] = m0_ref[...]

    gif = gif_ref[...]
    logf = -(jnp.maximum(-gif, 0.0) + jnp.log1p(jnp.exp(-jnp.abs(gif))))
    row = lax.broadcasted_iota(jnp.int32, (L, L), 0)
    col = lax.broadcasted_iota(jnp.int32, (L, L), 1)
    causal = col <= row
    tril = jnp.where(causal, 1.0, 0.0).astype(BF16)
    b_all = sum(_dot(tril, t) for t in _split(logf, 3))
    sel_r = lax.broadcasted_iota(jnp.int32, (2 * MLSTM_HEADS, GATE_PAD), 0)
    sel_c = lax.broadcasted_iota(jnp.int32, (2 * MLSTM_HEADS, GATE_PAD), 1)
    sel = jnp.where(sel_r == sel_c, 1.0, 0.0).astype(BF16)

    def rows_of(x):
        return sum(lax.dot_general(sel, t, _NT, preferred_element_type=F32) for t in _split(x, 3))

    i_rows = rows_of(gif)
    b_rows = rows_of(b_all)

    for h in range(MLSTM_HEADS):
        qs = slice(h * MLSTM_DK, (h + 1) * MLSTM_DK)
        vs = slice(h * MLSTM_DV, (h + 1) * MLSTM_DV)
        q = q_ref[:, qs]
        k = k_ref[:, qs]
        v = v_ref[:, vs]
        i_col = gif[:, h:h + 1]
        b_col = b_all[:, MLSTM_HEADS + h:MLSTM_HEADS + h + 1]
        i_row = i_rows[h:h + 1, :]
        b_row = b_rows[MLSTM_HEADS + h:MLSTM_HEADS + h + 1, :]
        c_old = c_ref[0, h]
        n_old = n_ref[0, h:h + 1, :]
        m_old = m_ref[0, h:h + 1, 0:1]

        dmat = jnp.where(causal, b_col - b_row + i_row, -jnp.inf)
        m_inter = b_col + m_old
        m_t = jnp.maximum(m_inter, jnp.max(dmat, axis=-1, keepdims=True))
        s = lax.dot_general(q, k, _NT, preferred_element_type=F32) * jnp.exp(dmat - m_t)
        w_inter = jnp.exp(m_inter - m_t)
        kf = k.astype(F32)
        num = _dot(s.astype(BF16), v) + w_inter * _dot(q, c_old.astype(BF16))
        qn = jnp.sum(q.astype(F32) * n_old, axis=-1, keepdims=True)
        den = jnp.sum(s, axis=-1, keepdims=True) + w_inter * qn
        hh = num / jnp.maximum(jnp.abs(den), jnp.exp(-m_t))

        m_new = m_t[L - 1:L, :]
        b_last = b_col[L - 1:L, :]
        g_col = jnp.exp(b_last - b_col + i_col - m_new)
        decay = jnp.exp(b_last + m_old - m_new)
        kg = kf * g_col
        c_ref[0, h] = decay * c_old + lax.dot_general(kg.astype(BF16), v, _TN,
                                                      preferred_element_type=F32)
        n_ref[0, h:h + 1, :] = decay * n_old + jnp.sum(kg, axis=0, keepdims=True)
        m_ref[0, h:h + 1, :] = jnp.broadcast_to(m_new, (1, LANES))

        mu = jnp.mean(hh, axis=-1, keepdims=True)
        hc = hh - mu
        var = jnp.mean(hc * hc, axis=-1, keepdims=True)
        hn = hc * lax.rsqrt(var + HEAD_NORM_EPS) * g_ref[:, vs]
        ha_ref[:, vs] = (ao_ref[:, vs] * hn).astype(BF16)


def _mlstm(qkv, ao, gif, c0, n0, m0, norm_g, l, prev, *, bsz, t, row_off):
    chunk = min(MLSTM_CHUNK, t)
    nc = t // chunk
    off = row_off // chunk
    rows = lambda b, c: off + b * nc + c
    state4 = pl.BlockSpec((1, MLSTM_HEADS, MLSTM_DK, MLSTM_DV), lambda b, c: (b, 0, 0, 0))
    state3 = pl.BlockSpec((1, MLSTM_HEADS, LANES), lambda b, c: (b, 0, 0))
    init4 = pl.BlockSpec((None, 1, MLSTM_HEADS, MLSTM_DK, MLSTM_DV), lambda b, c: (l, b, 0, 0, 0))
    init3 = pl.BlockSpec((None, 1, MLSTM_HEADS, LANES), lambda b, c: (l, b, 0, 0))
    args = [qkv, qkv, qkv, ao, gif, c0, n0, m0, norm_g.reshape(1, MLSTM_W)]
    prev_args, prev_specs, aliases = _alias_prev(prev, len(args))
    return pl.pallas_call(
        _mlstm_kernel,
        out_shape=(jax.ShapeDtypeStruct((qkv.shape[0], MLSTM_W), BF16),
                   jax.ShapeDtypeStruct(c0.shape[1:], F32),
                   jax.ShapeDtypeStruct(n0.shape[1:], F32),
                   jax.ShapeDtypeStruct(m0.shape[1:], F32)),
        grid=(bsz, nc),
        in_specs=[
            pl.BlockSpec((chunk, MLSTM_QK_W), lambda b, c: (rows(b, c), 0)),
            pl.BlockSpec((chunk, MLSTM_QK_W), lambda b, c: (rows(b, c), 1)),
            pl.BlockSpec((chunk, MLSTM_W), lambda b, c: (rows(b, c), 1)),
            pl.BlockSpec((chunk, MLSTM_W), lambda b, c: (rows(b, c), 0)),
            pl.BlockSpec((chunk, GATE_PAD), lambda b, c: (rows(b, c), 0)),
            init4, init3, init3,
            pl.BlockSpec((1, MLSTM_W), lambda b, c: (0, 0)),
        ] + prev_specs,
        out_specs=(pl.BlockSpec((chunk, MLSTM_W), lambda b, c: (rows(b, c), 0)),
                   state4, state3, state3),
        input_output_aliases=aliases,
        compiler_params=_params("parallel", "arbitrary"),
        name="mlstm",
    )(*args, *prev_args)


SB_SCALE = SB_DH ** -0.5
SB_DEAD_CARRY = -110.0


def _lower_tri(n):
    return jnp.where(lax.broadcasted_iota(jnp.int32, (n, n), 0)
                     >= lax.broadcasted_iota(jnp.int32, (n, n), 1), 1.0, 0.0).astype(BF16)


def _sb_group(q_ref, load_k, load_v, heads, acc_ref, carry_ref, lower, *, diag):
    tq = q_ref.shape[0]
    tk = lower.shape[0]
    if diag:
        mask = (lax.broadcasted_iota(jnp.int32, (tq, tk), 1)
                < lax.broadcasted_iota(jnp.int32, (tq, tk), 0))
    zs, lks = [], []
    for h in heads:
        z = lax.dot_general(q_ref[:, h * SB_DH:(h + 1) * SB_DH], load_k(h), _NT,
                            preferred_element_type=F32) * SB_SCALE
        log_keep = -(jnp.maximum(z, 0.0) + jnp.log(1.0 + jnp.exp(-jnp.abs(z))))
        if diag:
            log_keep = jnp.where(mask, log_keep, 0.0)
        zs.append(z)
        lks.append(log_keep)
    stacked = jnp.concatenate(lks, axis=0)
    rc_all = sum(_dot(t, lower) for t in _split(stacked, 2))
    for i, h in enumerate(heads):
        rc = rc_all[i * tq:(i + 1) * tq]
        carry = carry_ref[h]
        attn = jnp.exp(zs[i] + rc + carry)
        if diag:
            attn = jnp.where(mask, attn, 0.0)
        acc_ref[h] += _dot(attn.astype(BF16), load_v(h))
        carry_ref[h] = carry + rc[:, 0:1]


def _sb_block(q_ref, k_ref, v_ref, base, tk, acc_ref, carry_ref, lower, *, group, diag):
    def loader(ref):
        return lambda h: ref[pl.ds(base + h, tk, stride=SB_HEADS), :].astype(BF16)
    for g in range(0, SB_HEADS, group):
        _sb_group(q_ref, loader(k_ref), loader(v_ref), range(g, g + group),
                  acc_ref, carry_ref, lower, diag=diag)


def _sb_alive(carry_ref):
    return jnp.max(carry_ref[...]) > SB_DEAD_CARRY


def _sb_store(o_ref, acc_ref):
    for h in range(SB_HEADS):
        o_ref[:, h * SB_DH:(h + 1) * SB_DH] = acc_ref[h].astype(BF16)


def _sb_prompt_kernel(q_ref, k_ref, v_ref, *rest, tq, group):
    o_ref, acc_ref, carry_ref = rest[-3:]
    qi = pl.program_id(1)
    acc_ref[...] = jnp.zeros_like(acc_ref)
    carry_ref[...] = jnp.zeros_like(carry_ref)
    lower = _lower_tri(tq)
    blk_rows = tq * SB_HEADS

    _sb_block(q_ref, k_ref, v_ref, pl.multiple_of(qi * blk_rows, blk_rows), tq,
              acc_ref, carry_ref, lower, group=group, diag=True)

    def body(j, c):
        @pl.when(_sb_alive(carry_ref))
        def _():
            base = pl.multiple_of((qi - 1 - j) * blk_rows, blk_rows)
            _sb_block(q_ref, k_ref, v_ref, base, tq, acc_ref, carry_ref, lower,
                      group=group, diag=False)
        return c

    lax.fori_loop(0, qi, body, 0)
    _sb_store(o_ref, acc_ref)


def _sb_head_kernel(q_ref, k_ref, v_ref, kp_ref, vp_ref, *rest, tq):
    o_ref, acc_out_ref, carry_out_ref, acc_ref, carry_ref = rest[-5:]
    s = pl.program_id(1)

    @pl.when(s == 0)
    def _():
        acc_ref[...] = jnp.zeros_like(acc_ref)
        carry_ref[...] = jnp.zeros_like(carry_ref)
        _sb_block(q_ref, k_ref, v_ref, 0, tq, acc_ref, carry_ref, _lower_tri(tq),
                  group=SB_HEADS, diag=True)

    @pl.when(s == 1)
    def _():
        @pl.when(_sb_alive(carry_ref))
        def _():
            _sb_block(q_ref, kp_ref, vp_ref, 0, SB_TQ, acc_ref, carry_ref, _lower_tri(SB_TQ),
                      group=SB_HEADS, diag=False)

        _sb_store(o_ref, acc_ref)
        acc_out_ref[0] = acc_ref[...]
        carry_out_ref[0] = jnp.broadcast_to(carry_ref[...], carry_out_ref.shape[1:])


def _sb_tail_kernel(alive_ref, fetch_ref, q_ref, kp_ref, vp_ref, acc_in_ref, carry_in_ref, hb_ref,
                    o_ref, acc_ref, carry_ref, *, n_tail):
    del fetch_ref
    b = pl.program_id(0)

    @pl.when(alive_ref[b] == 0)
    def _():
        o_ref[...] = hb_ref[...]

    @pl.when(alive_ref[b] != 0)
    def _():
        acc_ref[...] = acc_in_ref[0]
        carry_ref[...] = carry_in_ref[0][:, :, 0:1]
        lower = _lower_tri(SB_TQ)
        blk_rows = SB_TQ * SB_HEADS

        def body(j, c):
            @pl.when(_sb_alive(carry_ref))
            def _():
                base = pl.multiple_of((n_tail - 1 - j) * blk_rows, blk_rows)
                _sb_block(q_ref, kp_ref, vp_ref, base, SB_TQ, acc_ref, carry_ref, lower,
                          group=SB_HEADS, diag=False)
            return c

        lax.fori_loop(0, n_tail, body, 0)
        _sb_store(o_ref, acc_ref)


def _stick_breaking(bq, k8, v8, past_k, past_v, l, prev, *, bsz, t, row_off):
    tq = min(SB_TQ, t)
    nq = t // tq
    off_q = row_off // tq
    scratch = [pltpu.VMEM((SB_HEADS, tq, SB_DH), F32), pltpu.VMEM((SB_HEADS, tq, 1), F32)]
    out_shape = jax.ShapeDtypeStruct((bq.shape[0], SB_W), BF16)
    q_spec = pl.BlockSpec((tq, SB_W), lambda b, i: (off_q + b * nq + i, 0))
    if past_k is None:
        seq_spec = pl.BlockSpec((None, t * SB_HEADS, SB_DH), lambda b, i: (l, b, 0))
        args = [bq, k8, v8]
        prev_args, prev_specs, aliases = _alias_prev(prev, len(args))
        return pl.pallas_call(
            functools.partial(_sb_prompt_kernel, tq=tq, group=SB_HEADS // 2),
            out_shape=out_shape,
            grid=(bsz, nq),
            in_specs=[q_spec, seq_spec, seq_spec] + prev_specs,
            out_specs=q_spec,
            scratch_shapes=scratch,
            input_output_aliases=aliases,
            compiler_params=_params("parallel", "arbitrary"),
            name="stick_breaking_prompt",
        )(*args, *prev_args)
    assert nq == 1
    blk = SB_TQ * SB_HEADS
    n_past = past_k.shape[2] // blk
    n_tail = n_past - 1
    state_shape = jax.ShapeDtypeStruct((bsz, SB_HEADS, tq, SB_DH), F32)
    state_blk = (1, SB_HEADS, tq, SB_DH)
    q_spec = pl.BlockSpec((tq, SB_W), lambda b, s: (off_q + b, 0))
    new_spec = pl.BlockSpec((None, t * SB_HEADS, SB_DH), lambda b, s: (l, b, 0))
    near_spec = pl.BlockSpec((None, None, blk, SB_DH), lambda b, s: (l, b, n_tail, 0))
    state_spec = pl.BlockSpec(state_blk, lambda b, s: (b, 0, 0, 0))
    args = [bq, k8, v8, past_k, past_v]
    prev_args, prev_specs, aliases = _alias_prev(prev, len(args))
    hb, acc, carry = pl.pallas_call(
        functools.partial(_sb_head_kernel, tq=tq),
        out_shape=(out_shape, state_shape, state_shape),
        grid=(bsz, 2),
        in_specs=[q_spec, new_spec, new_spec, near_spec, near_spec] + prev_specs,
        out_specs=(q_spec, state_spec, state_spec),
        scratch_shapes=scratch,
        input_output_aliases=aliases,
        compiler_params=_params("parallel", "arbitrary"),
        name="stick_breaking_sample",
    )(*args, *prev_args)

    alive = jnp.max(carry, axis=(1, 2, 3)) > SB_DEAD_CARRY
    fetch = jnp.maximum(lax.cummax(jnp.where(alive, jnp.arange(bsz), -1), axis=0), 0)
    tail_spec = pl.BlockSpec((None, None, n_tail * blk, SB_DH),
                             lambda b, alive_ref, fetch_ref: (l, fetch_ref[b], 0, 0))
    state_in_spec = pl.BlockSpec(state_blk, lambda b, alive_ref, fetch_ref: (fetch_ref[b], 0, 0, 0))
    rows_spec = pl.BlockSpec((tq, SB_W), lambda b, alive_ref, fetch_ref: (off_q + b, 0))
    return pl.pallas_call(
        functools.partial(_sb_tail_kernel, n_tail=n_tail),
        out_shape=out_shape,
        grid_spec=pltpu.PrefetchScalarGridSpec(
            num_scalar_prefetch=2,
            grid=(bsz,),
            in_specs=[
                pl.BlockSpec((tq, SB_W), lambda b, alive_ref, fetch_ref: (off_q + fetch_ref[b], 0)),
                tail_spec, tail_spec, state_in_spec, state_in_spec,
                rows_spec,
            ],
            out_specs=rows_spec,
            scratch_shapes=scratch,
        ),
        input_output_aliases={7: 0},
        compiler_params=_params("arbitrary"),
        name="stick_breaking_tail",
    )(alive.astype(jnp.int32), fetch.astype(jnp.int32), bq, past_k, past_v, acc, carry, hb)


def _pool_kernel(u_ref, buf_ref, w_ref, s_ref, *rest, tt, pos0):
    o_ref, ext_ref = rest[-2:]
    i = pl.program_id(1)

    @pl.when(i == 0)
    def _():
        ext_ref[0:POOL_HALO, :] = buf_ref[0]

    @pl.when(i > 0)
    def _():
        ext_ref[0:POOL_HALO, :] = ext_ref[tt:tt + POOL_HALO, :]

    ext_ref[POOL_HALO:POOL_HALO + tt, :] = u_ref[...]
    pos = pos0 + i * tt + lax.broadcasted_iota(jnp.int32, (tt, 1), 0)
    for gi, w in enumerate(POOL_WINDOWS):
        cs = slice(gi * POOL_G, (gi + 1) * POOL_G)
        cur = ext_ref[POOL_HALO:POOL_HALO + tt, cs]
        wsum = cur
        for j in range(1, w):
            wsum = wsum + ext_ref[POOL_HALO - j:POOL_HALO - j + tt, cs]
        cnt = jnp.minimum(pos + 1, w).astype(F32)
        d = (wsum / cnt - cur).astype(BF16)
        y = _dot(d, w_ref[gi]) * s_ref[:, cs]
        o_ref[:, cs] = y.astype(BF16)


def _pool(u, buf, w_pool, scale, l, prev, *, bsz, t, row_off, pos0):
    tt = min(256, t)
    nt = t // tt
    off = row_off // tt
    row_spec = pl.BlockSpec((tt, POOL_W), lambda b, i: (off + b * nt + i, 0))
    args = [u, buf, w_pool, scale.reshape(1, POOL_W)]
    prev_args, prev_specs, aliases = _alias_prev(prev, len(args))
    return pl.pallas_call(
        functools.partial(_pool_kernel, tt=tt, pos0=pos0),
        out_shape=jax.ShapeDtypeStruct((u.shape[0], POOL_W), BF16),
        grid=(bsz, nt),
        in_specs=[
            row_spec,
            pl.BlockSpec((1, POOL_HALO, POOL_W), lambda b, i: (b, 0, 0)),
            pl.BlockSpec((None, len(POOL_WINDOWS), POOL_G, POOL_G), lambda b, i: (l, 0, 0, 0)),
            pl.BlockSpec((1, POOL_W), lambda b, i: (0, 0)),
        ] + prev_specs,
        out_specs=row_spec,
        scratch_shapes=[pltpu.VMEM((POOL_HALO + tt, POOL_W), F32)],
        input_output_aliases=aliases,
        compiler_params=_params("parallel", "arbitrary"),
        name="pool_mix",
    )(*args, *prev_args)


X_SCALE = X_DH ** -0.5


def _cross_kernel(q_ref, k_ref, v_ref, *rest):
    o_ref = rest[-1]
    for h in range(X_HEADS):
        cs = slice(h * X_DH, (h + 1) * X_DH)
        k = k_ref[0, :, cs].astype(BF16)
        v = v_ref[0, :, cs].astype(BF16)
        s = lax.dot_general(q_ref[:, cs], k, _NT, preferred_element_type=F32) * X_SCALE
        e = jnp.exp(s - jnp.max(s, axis=-1, keepdims=True))
        p = e / jnp.sum(e, axis=-1, keepdims=True)
        o_ref[:, cs] = _dot(p.astype(BF16), v).astype(BF16)


def _cross_attend(qx, mk, mv, l, prev, *, bsz, t, row_off):
    tq = min(256, t)
    nq = t // tq
    off = row_off // tq
    row_spec = pl.BlockSpec((tq, X_W), lambda b, i: (off + b * nq + i, 0))
    mem_spec = pl.BlockSpec((None, 1, mk.shape[2], X_W), lambda b, i: (l, b, 0, 0))
    args = [qx, mk, mv]
    prev_args, prev_specs, aliases = _alias_prev(prev, len(args))
    return pl.pallas_call(
        _cross_kernel,
        out_shape=jax.ShapeDtypeStruct((qx.shape[0], X_W), BF16),
        grid=(bsz, nq),
        in_specs=[row_spec, mem_spec, mem_spec] + prev_specs,
        out_specs=row_spec,
        input_output_aliases=aliases,
        compiler_params=_params("parallel", "arbitrary"),
        name="cross_attend",
    )(*args, *prev_args)


def _merge_kernel(x_ref, wg0_ref, wg1_ref, wg2_ref, bg0_ref, bg1_ref, bg2_ref,
                  ha_ref, hb_ref, hc_ref, wa_ref, wb_ref, wc_ref, o_ref):
    x = x_ref[...]
    out = None
    for wg_ref, bg_ref, h_ref, wp_ref in ((wg0_ref, bg0_ref, ha_ref, wa_ref),
                                          (wg1_ref, bg1_ref, hb_ref, wb_ref),
                                          (wg2_ref, bg2_ref, hc_ref, wc_ref)):
        gate = jax.nn.sigmoid(lax.dot_general(x, wg_ref[...], _NT, preferred_element_type=F32)
                              + bg_ref[...])
        term = gate * _dot(h_ref[...], wp_ref[...])
        out = term if out is None else out + term
    o_ref[...] = out.astype(BF16)


def _merge(xb, wt_b, bg, ha, hb, hc, w_pa, w_pb, w_pc, *, bm=512, bn=256):
    m = xb.shape[0]
    nb = D_MODEL // bn
    gb = B_G // bn
    row = lambda width: pl.BlockSpec((bm, width), lambda i, j: (i, 0))
    colw = lambda kdim: pl.BlockSpec((None, kdim, bn), lambda i, j: (0, 0, j))
    gate_w = [pl.BlockSpec((None, bn, D_MODEL), lambda i, j, g=g: (0, gb + g * nb + j, 0))
              for g in range(N_BRANCH)]
    gate_b = [pl.BlockSpec((1, bn), lambda i, j, g=g: (0, g * nb + j)) for g in range(N_BRANCH)]
    return pl.pallas_call(
        _merge_kernel,
        out_shape=jax.ShapeDtypeStruct((m, D_MODEL), BF16),
        grid=(m // bm, nb),
        in_specs=[row(D_MODEL)] + gate_w + gate_b
                 + [row(MLSTM_W), row(SB_W), row(POOL_W), colw(MLSTM_W), colw(SB_W), colw(POOL_W)],
        out_specs=pl.BlockSpec((bm, bn), lambda i, j: (i, j)),
        compiler_params=_params("parallel", "arbitrary"),
        name="gated_merge",
    )(xb, wt_b, wt_b, wt_b, bg, bg, bg, ha, hb, hc, w_pa, w_pb, w_pc)


def _trunk_layer(x, xb, W, l, groups, kv_prev, last):
    b_in = W["b_in"][l]
    proj_a = functools.partial(_matmul, xb, W["wt_in"], l, wt=True, bn=512)
    group_rows = tuple(g["bsz"] * g["t"] for g in groups)

    k_scale = jnp.concatenate([jnp.ones((MLSTM_QK_W,), F32),
                               jnp.full((MLSTM_QK_W,), MLSTM_DK ** -0.5, F32),
                               jnp.ones((MLSTM_W,), F32)])
    qkv, wt_b = proj_a(n=OFF_AO, bias=b_in[:OFF_AO], scale=k_scale, out_dtype=BF16,
                       side_cast=[(W["wt_in"], l, OFF_BQ, IN_W - OFF_BQ, 256)], name="proj_qkv")
    proj_b = functools.partial(_matmul, xb, wt_b, 0, wt=True)
    whole = lambda name, slab: (W[name], l, 0, W[name].shape[1], slab)
    ao, w_out = proj_a(n=MLSTM_W, col_off=OFF_AO, bias=b_in[OFF_AO:OFF_AI], act="sigmoid",
                       side_cast=[whole("w_out", 128)], name="proj_ogate")
    b_if = jnp.pad(b_in[OFF_AI:OFF_BQ], (0, GATE_PAD - 2 * MLSTM_HEADS))
    gif, w_pa, w_pb, w_pc = proj_a(
        n=GATE_PAD, col_off=OFF_AI, bias=b_if, bn=GATE_PAD,
        side_cast=[whole("w_pa", 256), whole("w_pb", 128), whole("w_pc", 128)], name="proj_if")
    sec = lambda i: b_in[OFF_BQ + i * SB_W:OFF_BQ + (i + 1) * SB_W]
    bq, wq_x, wo_x = proj_b(n=SB_W, bias=sec(0), out_dtype=BF16,
                            side_cast=[whole("wq_x", 512), whole("wo_x", 128)], name="proj_sb_q")
    k8 = _proj_heads(xb, wt_b, l, sec(1), B_BK, group_rows, kv_prev[0], name="proj_sb_k")
    v8 = _proj_heads(xb, wt_b, l, sec(2), B_BV, group_rows, kv_prev[1], name="proj_sb_v")
    u = proj_b(n=POOL_W, col_off=B_CU, bias=sec(3), name="proj_pool_u")

    rows = xb.shape[0]
    ha, hb, hc = (jnp.zeros((rows, w), BF16) for w in (MLSTM_W, SB_W, POOL_W))
    states = []
    for gi, g in enumerate(groups):
        dims = dict(bsz=g["bsz"], t=g["t"], row_off=g["row_off"])
        ha, c_new, n_new, m_new = _mlstm(qkv, ao, gif, g["c"], g["n"], g["m"],
                                         W["mlstm_norm_g"][l], l, ha, **dims)
        hb = _stick_breaking(bq, k8[gi], v8[gi], g["sb_k"], g["sb_v"], l, hb, **dims)
        hc = _pool(u, g["pool_buf"][l], W["w_pool"], W["pool_scale"][l], l, hc,
                   pos0=g["pos0"], **dims)
        states.append((c_new, n_new, m_new))

    merged = _merge(xb, wt_b, b_in[OFF_G:].reshape(1, -1), ha, hb, hc,
                    w_pa, w_pb, w_pc)
    y = _matmul(merged, w_out, 0, n=D_MODEL, res=x, name="w_out")
    x, xb = _layer_norm(y, W["ln1_g"][l], W["ln1_b"][l])

    qx = _matmul(xb, wq_x, 0, n=X_W, out_dtype=BF16, name="wq_x")
    ox = jnp.zeros((rows, X_W), BF16)
    for g in groups:
        ox = _cross_attend(qx, g["mk"], g["mv"], l, ox, bsz=g["bsz"], t=g["t"],
                           row_off=g["row_off"])
    y = _matmul(ox, wo_x, 0, n=D_MODEL, res=x, name="wo_x")
    x, xb = _layer_norm(y, W["ln2_g"][l], W["ln2_b"][l])

    hid, w_down = _matmul(xb, W["w_up"], l, n=D_FF, act="relu2", out_dtype=BF16, bn=512,
                          side_cast=[whole("w_down", 64)], name="w_up")
    y = _matmul(hid, w_down, 0, n=D_MODEL, bk=D_MODEL, name="w_down")
    if last:
        x = tuple(_layer_norm(x, W["ln3_g"][l], W["ln3_b"][l], res=y, in_off=g["row_off"],
                              n_rows=g["bsz"] * g["t"], emit_bf16=False) for g in groups)
        xb = None
    else:
        x, xb = _layer_norm(x, W["ln3_g"][l], W["ln3_b"][l], res=y)
    return x, xb, (k8, v8), u, states


def kernel(x_prompt, x_sample, cache_sb_k, cache_sb_v, state_mlstm_c, state_mlstm_n, state_mlstm_m, state_pool, cache_mem_k, cache_mem_v, mem_prompt, ln_in_g, ln_in_b, w_in, b_in, mlstm_norm_g, w_pool, pool_scale, w_pa, w_pb, w_pc, w_out, ln1_g, ln1_b, wq_x, wk_x, wv_x, wo_x, ln2_g, ln2_b, w_up, w_down, ln3_g, ln3_b):
    bf = lambda a: a.astype(BF16)
    wt_in = jnp.swapaxes(w_in, 1, 2)
    W = dict(wt_in=wt_in,
             b_in=b_in, mlstm_norm_g=mlstm_norm_g, w_pool=bf(w_pool),
             pool_scale=pool_scale, w_pa=w_pa, w_pb=w_pb, w_pc=w_pc, w_out=w_out,
             ln1_g=ln1_g, ln1_b=ln1_b, wq_x=wq_x, wo_x=wo_x, ln2_g=ln2_g, ln2_b=ln2_b,
             w_up=w_up, w_down=w_down, ln3_g=ln3_g, ln3_b=ln3_b)
    bp, tp, _ = x_prompt.shape
    bs, ts, _ = x_sample.shape
    rows_p, rows_s = bp * tp, bs * ts
    rows = rows_p + rows_s
    past = cache_sb_k.shape[2]
    mem_len = mem_prompt.shape[1]

    mem_b = mem_prompt.reshape(bp * mem_len, D_MODEL).astype(BF16)
    wk_b, wv_b = bf(wk_x), bf(wv_x)
    mem4 = lambda a, b: a.reshape(DEPTH, b, mem_len, X_W)
    mk_p = mem4(jnp.stack([_matmul(mem_b, wk_b, l, n=X_W, name="mem_k") for l in range(DEPTH)]), bp)
    mv_p = mem4(jnp.stack([_matmul(mem_b, wv_b, l, n=X_W, name="mem_v") for l in range(DEPTH)]), bp)

    lane_m = lambda m: jnp.broadcast_to(m[..., None], m.shape + (LANES,))
    halo = lambda b: jnp.pad(b, ((0, 0), (0, 0), (POOL_HALO - POOL_BUF, 0), (0, 0)))
    heads_rows = lambda c: c.reshape(DEPTH, bs, past * SB_HEADS, SB_DH)
    groups = [
        dict(bsz=bp, t=tp, row_off=0, pos0=0,
             c=jnp.zeros((DEPTH, bp, MLSTM_HEADS, MLSTM_DK, MLSTM_DV), F32),
             n=jnp.zeros((DEPTH, bp, MLSTM_HEADS, MLSTM_DK), F32),
             m=jnp.zeros((DEPTH, bp, MLSTM_HEADS, LANES), F32),
             sb_k=None, sb_v=None,
             pool_buf=jnp.zeros((DEPTH, bp, POOL_HALO, POOL_W), F32),
             mk=mk_p, mv=mv_p),
        dict(bsz=bs, t=ts, row_off=rows_p, pos0=past,
             c=state_mlstm_c, n=state_mlstm_n, m=lane_m(state_mlstm_m),
             sb_k=heads_rows(cache_sb_k), sb_v=heads_rows(cache_sb_v),
             pool_buf=halo(state_pool),
             mk=mem4(cache_mem_k, bs), mv=mem4(cache_mem_v, bs)),
    ]

    xs = None
    for g, xin in zip(groups, (x_prompt, x_sample)):
        xs = _layer_norm(xin.reshape(g["bsz"] * g["t"], D_MODEL), ln_in_g, ln_in_b,
                         out_off=g["row_off"], out_rows=rows, prev=xs)
    x, xb = xs

    kv = tuple(tuple(jnp.zeros((DEPTH, g["bsz"] * g["t"] * SB_HEADS, SB_DH), F32) for g in groups)
               for _ in range(2))
    per_layer = []
    for l in range(DEPTH):
        x, xb, kv, u, states = _trunk_layer(x, xb, W, l, groups, kv, last=l == DEPTH - 1)
        per_layer.append((u, states))

    def group_out(gi, g):
        bsz, t, r0 = g["bsz"], g["t"], g["row_off"]
        sbk = kv[0][gi].reshape(DEPTH, bsz, t, SB_HEADS, SB_DH)
        sbv = kv[1][gi].reshape(DEPTH, bsz, t, SB_HEADS, SB_DH)
        c = jnp.stack([st[gi][0] for _, st in per_layer])
        n = jnp.stack([st[gi][1] for _, st in per_layer])
        m = jnp.stack([st[gi][2][..., 0] for _, st in per_layer])
        pool = jnp.stack([u[r0:r0 + bsz * t].reshape(bsz, t, POOL_W)[:, t - POOL_BUF:, :]
                          for u, _ in per_layer])
        return sbk, sbv, c, n, m, pool

    out_p = group_out(0, groups[0])
    out_s = group_out(1, groups[1])
    y_prompt = x[0].reshape(bp, tp, D_MODEL)
    y_sample = x[1].reshape(bs, ts, D_MODEL)
    mem5 = lambda a: a.reshape(DEPTH, bp, mem_len, X_HEADS, X_DH)
    return (y_prompt, y_sample) + out_p + (mem5(mk_p), mem5(mv_p)) + out_s
```

```python
import functools

import jax
import jax.numpy as jnp
from jax import lax
from jax.experimental import pallas as pl
from jax.experimental.pallas import tpu as pltpu

F32 = jnp.float32
BF16 = jnp.bfloat16

D_MODEL = 4096
DEPTH = 2
MLSTM_CHUNK = 256
MLSTM_HEADS = 8
MLSTM_W = D_MODEL // 2
MLSTM_DV = MLSTM_W // MLSTM_HEADS
MLSTM_DK = MLSTM_DV // 2
MLSTM_QK_W = MLSTM_HEADS * MLSTM_DK
SB_HEADS = 8
SB_W = D_MODEL // 4
SB_DH = SB_W // SB_HEADS
POOL_WINDOWS = (2, 4, 8, 16)
POOL_W = D_MODEL // 4
POOL_G = POOL_W // len(POOL_WINDOWS)
POOL_BUF = max(POOL_WINDOWS) - 1
POOL_HALO = POOL_BUF + 1
X_HEADS = 4
X_W = D_MODEL // 4
X_DH = X_W // X_HEADS
D_FF = 4 * D_MODEL
N_BRANCH = 3
ALPHA = (2 * DEPTH) ** 0.25
LN_EPS = 1e-5
HEAD_NORM_EPS = 1e-6

OFF_AO = 2 * MLSTM_QK_W + MLSTM_W
OFF_AI = OFF_AO + MLSTM_W
OFF_BQ = OFF_AI + 2 * MLSTM_HEADS
OFF_G = OFF_BQ + 3 * SB_W + POOL_W
IN_W = OFF_G + N_BRANCH * D_MODEL

V7X_VMEM_LIMIT_BYTES = 56 * 1024 * 1024
LANES = 128
SUBLANES = 8
SB_TQ = 256
GATE_PAD = LANES

B_BK = SB_W
B_BV = 2 * SB_W
B_CU = 3 * SB_W
B_G = 3 * SB_W + POOL_W

_NT = (((1,), (1,)), ((), ()))
_TN = (((0,), (0,)), ((), ()))


def _params(*sem):
    return pltpu.CompilerParams(dimension_semantics=sem,
                                vmem_limit_bytes=V7X_VMEM_LIMIT_BYTES)


def _dot(a, b):
    return jnp.dot(a, b, preferred_element_type=F32)


def _split(x, terms):
    out = []
    for _ in range(terms - 1):
        h = x.astype(BF16)
        out.append(h)
        x = x - h.astype(F32)
    out.append(x.astype(BF16))
    return out


def _alias_prev(prev, n_in):
    if prev is None:
        return [], [], {}
    prev = list(prev) if isinstance(prev, (tuple, list)) else [prev]
    specs = [pl.BlockSpec(memory_space=pl.ANY)] * len(prev)
    return prev, specs, {n_in + i: i for i in range(len(prev))}


def _mm_kernel(*refs, act, has_bias, has_scale, has_res, nk, side_slabs, wt):
    it = iter(refs)
    x_ref, w_ref = next(it), next(it)
    b_ref = next(it) if has_bias else None
    s_ref = next(it) if has_scale else None
    r_ref = next(it) if has_res else None
    side_in_refs = [next(it) for _ in side_slabs]
    o_ref = next(it)
    side_out_refs = [next(it) for _ in side_slabs]
    acc_ref = next(it) if nk > 1 else None

    if side_slabs:
        step = pl.program_id(0) * pl.num_programs(1) + pl.program_id(1)
        for n_slabs, in_ref, out_ref in zip(side_slabs, side_in_refs, side_out_refs):
            @pl.when(step < n_slabs)
            def _(in_ref=in_ref, out_ref=out_ref):
                out_ref[...] = in_ref[0].astype(BF16)

    def epilogue(y):
        if has_bias:
            y = y + b_ref[...]
        if has_scale:
            y = y * s_ref[...]
        if act == "sigmoid":
            y = jax.nn.sigmoid(y)
        elif act == "relu2":
            y = jnp.square(jnp.maximum(y, 0.0))
        if has_res:
            y = ALPHA * r_ref[...] + y
        o_ref[...] = y.astype(o_ref.dtype)

    w = w_ref[...].astype(BF16)
    part = lax.dot_general(x_ref[...], w, _NT if wt else (((1,), (0,)), ((), ())),
                           preferred_element_type=F32)
    if nk == 1:
        epilogue(part)
    else:
        k = pl.program_id(2)

        @pl.when(k == 0)
        def _():
            acc_ref[...] = part

        @pl.when(k > 0)
        def _():
            acc_ref[...] += part

        @pl.when(k == nk - 1)
        def _():
            epilogue(acc_ref[...])


def _matmul(x, w, l, *, n, col_off=0, bias=None, scale=None, res=None, act="none",
            out_dtype=F32, bm=1024, bn=1024, bk=None, side_cast=(), wt=False, name="matmul"):
    m, kdim = x.shape
    bm, bn = min(bm, m), min(bn, n)
    bk = kdim if bk is None else bk
    assert m % bm == 0 and n % bn == 0 and kdim % bk == 0 and col_off % bn == 0
    nk = kdim // bk
    cb = col_off // bn
    nj = n // bn
    w_spec = (pl.BlockSpec((None, bn, bk), lambda i, j, k: (l, cb + j, k)) if wt
              else pl.BlockSpec((None, bk, bn), lambda i, j, k: (l, k, cb + j)))
    in_specs = [pl.BlockSpec((bm, bk), lambda i, j, k: (i, k)), w_spec]
    args = [x, w]
    for v in (bias, scale):
        if v is not None:
            in_specs.append(pl.BlockSpec((1, bn), lambda i, j, k: (0, j)))
            args.append(v.reshape(1, n).astype(F32))
    if res is not None:
        in_specs.append(pl.BlockSpec((bm, bn), lambda i, j, k: (i, j)))
        args.append(res)
    out_shape = [jax.ShapeDtypeStruct((m, n), out_dtype)]
    out_specs = [pl.BlockSpec((bm, bn), lambda i, j, k: (i, j))]
    side_slabs = []
    for side_w, side_l, row_off, sr, slab_rows in side_cast:
        sc = side_w.shape[2]
        n_slabs = sr // slab_rows
        assert nk == 1 and sr % slab_rows == 0 and n_slabs <= (m // bm) * nj
        slab = lambda i, j, k, n_slabs=n_slabs: jnp.minimum(i * nj + j, n_slabs - 1)
        in_specs.append(pl.BlockSpec(
            (pl.Element(1), pl.Element(slab_rows), pl.Element(sc)),
            lambda i, j, k, slab=slab, side_l=side_l, row_off=row_off, slab_rows=slab_rows: (
                side_l, pl.multiple_of(row_off + slab(i, j, k) * slab_rows, SUBLANES), 0)))
        args.append(side_w)
        out_shape.append(jax.ShapeDtypeStruct((1, sr, sc), BF16))
        out_specs.append(pl.BlockSpec((None, slab_rows, sc),
                                      lambda i, j, k, slab=slab: (0, slab(i, j, k), 0)))
        side_slabs.append(n_slabs)
    kern = functools.partial(_mm_kernel, act=act, has_bias=bias is not None,
                             has_scale=scale is not None, has_res=res is not None, nk=nk,
                             side_slabs=tuple(side_slabs), wt=wt)
    out = pl.pallas_call(
        kern,
        out_shape=tuple(out_shape),
        grid=(m // bm, nj, nk),
        in_specs=in_specs,
        out_specs=tuple(out_specs),
        scratch_shapes=[pltpu.VMEM((bm, bn), F32)] if nk > 1 else [],
        compiler_params=(_params("arbitrary", "arbitrary", "arbitrary") if side_slabs
                         else _params("parallel", "parallel", "arbitrary")),
        name=name,
    )(*args)
    return out if side_slabs else out[0]


def _proj_heads_kernel(x_ref, w_ref, b_ref, *rest, n_first):
    first_ref, second_ref = rest[-2:]
    y = lax.dot_general(x_ref[...], w_ref[...], _NT, preferred_element_type=F32) + b_ref[...]
    rows = y.shape[0]

    def store(o_ref):
        for h in range(SUBLANES):
            o_ref[pl.ds(h, rows, stride=SUBLANES), :] = y[:, h * LANES:(h + 1) * LANES]

    i = pl.program_id(0)

    @pl.when(i < n_first)
    def _():
        store(first_ref)

    @pl.when(i >= n_first)
    def _():
        store(second_ref)


def _proj_heads(x, w, l, bias, col_off, group_rows, prev, *, bm=1024, name="proj_heads"):
    m, kdim = x.shape
    n = SUBLANES * LANES
    rows_a, rows_b = group_rows
    assert rows_a % bm == 0 and rows_b % bm == 0 and rows_a + rows_b == m and col_off % n == 0
    n_first = rows_a // bm
    cb = col_off // n
    args = [x, w, bias.reshape(1, n)]
    prev_args, prev_specs, aliases = _alias_prev(prev, len(args))
    blk = (None, bm * SUBLANES, LANES)
    return pl.pallas_call(
        functools.partial(_proj_heads_kernel, n_first=n_first),
        out_shape=tuple(jax.ShapeDtypeStruct((DEPTH, r * SUBLANES, LANES), F32) for r in group_rows),
        grid=(m // bm,),
        in_specs=[pl.BlockSpec((bm, kdim), lambda i: (i, 0)),
                  pl.BlockSpec((None, n, kdim), lambda i: (0, cb, 0)),
                  pl.BlockSpec((1, n), lambda i: (0, 0))] + prev_specs,
        out_specs=(pl.BlockSpec(blk, lambda i: (l, jnp.minimum(i, n_first - 1), 0)),
                   pl.BlockSpec(blk, lambda i: (l, jnp.maximum(i - n_first, 0), 0))),
        input_output_aliases=aliases,
        compiler_params=_params("arbitrary"),
        name=name,
    )(*args, *prev_args)


def _ln_kernel(*refs, has_res, n_prev, emit_bf16, n_blocks):
    it = iter(refs)
    x_ref = next(it)
    r_ref = next(it) if has_res else None
    g_ref, b_ref = next(it), next(it)
    for _ in range(n_prev):
        next(it)
    o_ref = next(it)
    x = x_ref[...]
    if has_res:
        x = ALPHA * x + r_ref[...]
    mu = jnp.mean(x, axis=-1, keepdims=True)
    xc = x - mu
    var = jnp.mean(xc * xc, axis=-1, keepdims=True)
    y = xc * lax.rsqrt(var + LN_EPS) * g_ref[...] + b_ref[...]
    if n_blocks is not None:
        y = jnp.where(pl.program_id(0) < n_blocks, y, 0.0)
    o_ref[...] = y
    if emit_bf16:
        next(it)[...] = y.astype(BF16)


def _layer_norm(x, g, b, res=None, *, in_off=0, n_rows=None, out_off=0, out_rows=None,
                prev=None, emit_bf16=True, rows=256):
    d = x.shape[1]
    n_rows = x.shape[0] - in_off if n_rows is None else n_rows
    out_rows = n_rows if out_rows is None else out_rows
    ib, ob = in_off // rows, out_off // rows
    n_blocks = n_rows // rows
    fill = prev is None and out_rows > n_rows
    assert not fill or out_off == 0
    in_spec = pl.BlockSpec((rows, d), lambda i: (ib + jnp.minimum(i, n_blocks - 1), 0))
    out_spec = pl.BlockSpec((rows, d), lambda i: (ob + i, 0))
    vec_spec = pl.BlockSpec((1, d), lambda i: (0, 0))
    args = [x] + ([res] if res is not None else []) + [g.reshape(1, d), b.reshape(1, d)]
    in_specs = [in_spec] * (2 if res is not None else 1) + [vec_spec, vec_spec]
    prev_args, prev_specs, aliases = _alias_prev(prev, len(args))
    out_shape = [jax.ShapeDtypeStruct((out_rows, d), F32)]
    if emit_bf16:
        out_shape.append(jax.ShapeDtypeStruct((out_rows, d), BF16))
    out = pl.pallas_call(
        functools.partial(_ln_kernel, has_res=res is not None, n_prev=len(prev_args),
                          emit_bf16=emit_bf16, n_blocks=n_blocks if fill else None),
        out_shape=tuple(out_shape),
        grid=(out_rows // rows if fill else n_blocks,),
        in_specs=in_specs + prev_specs,
        out_specs=tuple([out_spec] * len(out_shape)),
        input_output_aliases=aliases,
        compiler_params=_params("parallel"),
        name="layer_norm",
    )(*args, *prev_args)
    return out if emit_bf16 else out[0]


def _mlstm_kernel(q_ref, k_ref, v_ref, ao_ref, gif_ref, c0_ref, n0_ref, m0_ref, g_ref, *rest):
    ha_ref, c_ref, n_ref, m_ref = rest[-4:]
    L = q_ref.shape[0]

    @pl.when(pl.program_id(1) == 0)
    def _():
        c_ref[...] = c0_ref[...]
        n_ref[...] = n0_ref[...]
        m_ref[...] = m0_ref[...]

    gif = gif_ref[...]
    logf = -(jnp.maximum(-gif, 0.0) + jnp.log1p(jnp.exp(-jnp.abs(gif))))
    row = lax.broadcasted_iota(jnp.int32, (L, L), 0)
    col = lax.broadcasted_iota(jnp.int32, (L, L), 1)
    causal = col <= row
    tril = jnp.where(causal, 1.0, 0.0).astype(BF16)
    b_all = sum(_dot(tril, t) for t in _split(logf, 3))
    sel_r = lax.broadcasted_iota(jnp.int32, (2 * MLSTM_HEADS, GATE_PAD), 0)
    sel_c = lax.broadcasted_iota(jnp.int32, (2 * MLSTM_HEADS, GATE_PAD), 1)
    sel = jnp.where(sel_r == sel_c, 1.0, 0.0).astype(BF16)

    def rows_of(x):
        return sum(lax.dot_general(sel, t, _NT, preferred_element_type=F32) for t in _split(x, 3))

    i_rows = rows_of(gif)
    b_rows = rows_of(b_all)

    for h in range(MLSTM_HEADS):
        qs = slice(h * MLSTM_DK, (h + 1) * MLSTM_DK)
        vs = slice(h * MLSTM_DV, (h + 1) * MLSTM_DV)
        q = q_ref[:, qs]
        k = k_ref[:, qs]
        v = v_ref[:, vs]
        i_col = gif[:, h:h + 1]
        b_col = b_all[:, MLSTM_HEADS + h:MLSTM_HEADS + h + 1]
        i_row = i_rows[h:h + 1, :]
        b_row = b_rows[MLSTM_HEADS + h:MLSTM_HEADS + h + 1, :]
        c_old = c_ref[0, h]
        n_old = n_ref[0, h:h + 1, :]
        m_old = m_ref[0, h:h + 1, 0:1]

        dmat = jnp.where(causal, b_col - b_row + i_row, -jnp.inf)
        m_inter = b_col + m_old
        m_t = jnp.maximum(m_inter, jnp.max(dmat, axis=-1, keepdims=True))
        s = lax.dot_general(q, k, _NT, preferred_element_type=F32) * jnp.exp(dmat - m_t)
        w_inter = jnp.exp(m_inter - m_t)
        kf = k.astype(F32)
        num = _dot(s.astype(BF16), v) + w_inter * _dot(q, c_old.astype(BF16))
        qn = jnp.sum(q.astype(F32) * n_old, axis=-1, keepdims=True)
        den = jnp.sum(s, axis=-1, keepdims=True) + w_inter * qn
        hh = num / jnp.maximum(jnp.abs(den), jnp.exp(-m_t))

        m_new = m_t[L - 1:L, :]
        b_last = b_col[L - 1:L, :]
        g_col = jnp.exp(b_last - b_col + i_col - m_new)
        decay = jnp.exp(b_last + m_old - m_new)
        kg = kf * g_col
        c_ref[0, h] = decay * c_old + lax.dot_general(kg.astype(BF16), v, _TN,
                                                      preferred_element_type=F32)
        n_ref[0, h:h + 1, :] = decay * n_old + jnp.sum(kg, axis=0, keepdims=True)
        m_ref[0, h:h + 1, :] = jnp.broadcast_to(m_new, (1, LANES))

        mu = jnp.mean(hh, axis=-1, keepdims=True)
        hc = hh - mu
        var = jnp.mean(hc * hc, axis=-1, keepdims=True)
        hn = hc * lax.rsqrt(var + HEAD_NORM_EPS) * g_ref[:, vs]
        ha_ref[:, vs] = (ao_ref[:, vs] * hn).astype(BF16)


def _mlstm(qkv, ao, gif, c0, n0, m0, norm_g, l, prev, *, bsz, t, row_off):
    chunk = min(MLSTM_CHUNK, t)
    nc = t // chunk
    off = row_off // chunk
    rows = lambda b, c: off + b * nc + c
    state4 = pl.BlockSpec((1, MLSTM_HEADS, MLSTM_DK, MLSTM_DV), lambda b, c: (b, 0, 0, 0))
    state3 = pl.BlockSpec((1, MLSTM_HEADS, LANES), lambda b, c: (b, 0, 0))
    init4 = pl.BlockSpec((None, 1, MLSTM_HEADS, MLSTM_DK, MLSTM_DV), lambda b, c: (l, b, 0, 0, 0))
    init3 = pl.BlockSpec((None, 1, MLSTM_HEADS, LANES), lambda b, c: (l, b, 0, 0))
    args = [qkv, qkv, qkv, ao, gif, c0, n0, m0, norm_g.reshape(1, MLSTM_W)]
    prev_args, prev_specs, aliases = _alias_prev(prev, len(args))
    return pl.pallas_call(
        _mlstm_kernel,
        out_shape=(jax.ShapeDtypeStruct((qkv.shape[0], MLSTM_W), BF16),
                   jax.ShapeDtypeStruct(c0.shape[1:], F32),
                   jax.ShapeDtypeStruct(n0.shape[1:], F32),
                   jax.ShapeDtypeStruct(m0.shape[1:], F32)),
        grid=(bsz, nc),
        in_specs=[
            pl.BlockSpec((chunk, MLSTM_QK_W), lambda b, c: (rows(b, c), 0)),
            pl.BlockSpec((chunk, MLSTM_QK_W), lambda b, c: (rows(b, c), 1)),
            pl.BlockSpec((chunk, MLSTM_W), lambda b, c: (rows(b, c), 1)),
            pl.BlockSpec((chunk, MLSTM_W), lambda b, c: (rows(b, c), 0)),
            pl.BlockSpec((chunk, GATE_PAD), lambda b, c: (rows(b, c), 0)),
            init4, init3, init3,
            pl.BlockSpec((1, MLSTM_W), lambda b, c: (0, 0)),
        ] + prev_specs,
        out_specs=(pl.BlockSpec((chunk, MLSTM_W), lambda b, c: (rows(b, c), 0)),
                   state4, state3, state3),
        input_output_aliases=aliases,
        compiler_params=_params("parallel", "arbitrary"),
        name="mlstm",
    )(*args, *prev_args)


SB_SCALE = SB_DH ** -0.5
SB_DEAD_CARRY = -110.0


def _lower_tri(n):
    return jnp.where(lax.broadcasted_iota(jnp.int32, (n, n), 0)
                     >= lax.broadcasted_iota(jnp.int32, (n, n), 1), 1.0, 0.0).astype(BF16)


def _sb_group(q_ref, load_k, load_v, heads, acc_ref, carry_ref, lower, *, diag):
    tq = q_ref.shape[0]
    tk = lower.shape[0]
    if diag:
        mask = (lax.broadcasted_iota(jnp.int32, (tq, tk), 1)
                < lax.broadcasted_iota(jnp.int32, (tq, tk), 0))
    zs, lks = [], []
    for h in heads:
        z = lax.dot_general(q_ref[:, h * SB_DH:(h + 1) * SB_DH], load_k(h), _NT,
                            preferred_element_type=F32) * SB_SCALE
        log_keep = -(jnp.maximum(z, 0.0) + jnp.log(1.0 + jnp.exp(-jnp.abs(z))))
        if diag:
            log_keep = jnp.where(mask, log_keep, 0.0)
        zs.append(z)
        lks.append(log_keep)
    stacked = jnp.concatenate(lks, axis=0)
    rc_all = sum(_dot(t, lower) for t in _split(stacked, 2))
    for i, h in enumerate(heads):
        rc = rc_all[i * tq:(i + 1) * tq]
        carry = carry_ref[h]
        attn = jnp.exp(zs[i] + rc + carry)
        if diag:
            attn = jnp.where(mask, attn, 0.0)
        acc_ref[h] += _dot(attn.astype(BF16), load_v(h))
        carry_ref[h] = carry + rc[:, 0:1]


def _sb_block(q_ref, k_ref, v_ref, base, tk, acc_ref, carry_ref, lower, *, group, diag):
    def loader(ref):
        return lambda h: ref[pl.ds(base + h, tk, stride=SB_HEADS), :].astype(BF16)
    for g in range(0, SB_HEADS, group):
        _sb_group(q_ref, loader(k_ref), loader(v_ref), range(g, g + group),
                  acc_ref, carry_ref, lower, diag=diag)


def _sb_alive(carry_ref):
    return jnp.max(carry_ref[...]) > SB_DEAD_CARRY


def _sb_store(o_ref, acc_ref):
    for h in range(SB_HEADS):
        o_ref[:, h * SB_DH:(h + 1) * SB_DH] = acc_ref[h].astype(BF16)


def _sb_prompt_kernel(q_ref, k_ref, v_ref, *rest, tq, group):
    o_ref, acc_ref, carry_ref = rest[-3:]
    qi = pl.program_id(1)
    acc_ref[...] = jnp.zeros_like(acc_ref)
    carry_ref[...] = jnp.zeros_like(carry_ref)
    lower = _lower_tri(tq)
    blk_rows = tq * SB_HEADS

    _sb_block(q_ref, k_ref, v_ref, pl.multiple_of(qi * blk_rows, blk_rows), tq,
              acc_ref, carry_ref, lower, group=group, diag=True)

    def body(j, c):
        @pl.when(_sb_alive(carry_ref))
        def _():
            base = pl.multiple_of((qi - 1 - j) * blk_rows, blk_rows)
            _sb_block(q_ref, k_ref, v_ref, base, tq, acc_ref, carry_ref, lower,
                      group=group, diag=False)
        return c

    lax.fori_loop(0, qi, body, 0)
    _sb_store(o_ref, acc_ref)


def _sb_head_kernel(q_ref, k_ref, v_ref, kp_ref, vp_ref, *rest, tq):
    o_ref, acc_out_ref, carry_out_ref, acc_ref, carry_ref = rest[-5:]
    s = pl.program_id(1)

    @pl.when(s == 0)
    def _():
        acc_ref[...] = jnp.zeros_like(acc_ref)
        carry_ref[...] = jnp.zeros_like(carry_ref)
        _sb_block(q_ref, k_ref, v_ref, 0, tq, acc_ref, carry_ref, _lower_tri(tq),
                  group=SB_HEADS, diag=True)

    @pl.when(s == 1)
    def _():
        @pl.when(_sb_alive(carry_ref))
        def _():
            _sb_block(q_ref, kp_ref, vp_ref, 0, SB_TQ, acc_ref, carry_ref, _lower_tri(SB_TQ),
                      group=SB_HEADS, diag=False)

        _sb_store(o_ref, acc_ref)
        acc_out_ref[0] = acc_ref[...]
        carry_out_ref[0] = jnp.broadcast_to(carry_ref[...], carry_out_ref.shape[1:])


def _sb_tail_kernel(alive_ref, fetch_ref, q_ref, kp_ref, vp_ref, acc_in_ref, carry_in_ref, hb_ref,
                    o_ref, acc_ref, carry_ref, *, n_tail):
    del fetch_ref
    b = pl.program_id(0)

    @pl.when(alive_ref[b] == 0)
    def _():
        o_ref[...] = hb_ref[...]

    @pl.when(alive_ref[b] != 0)
    def _():
        acc_ref[...] = acc_in_ref[0]
        carry_ref[...] = carry_in_ref[0][:, :, 0:1]
        lower = _lower_tri(SB_TQ)
        blk_rows = SB_TQ * SB_HEADS

        def body(j, c):
            @pl.when(_sb_alive(carry_ref))
            def _():
                base = pl.multiple_of((n_tail - 1 - j) * blk_rows, blk_rows)
                _sb_block(q_ref, kp_ref, vp_ref, base, SB_TQ, acc_ref, carry_ref, lower,
                          group=SB_HEADS, diag=False)
            return c

        lax.fori_loop(0, n_tail, body, 0)
        _sb_store(o_ref, acc_ref)


def _stick_breaking(bq, k8, v8, past_k, past_v, l, prev, *, bsz, t, row_off):
    tq = min(SB_TQ, t)
    nq = t // tq
    off_q = row_off // tq
    scratch = [pltpu.VMEM((SB_HEADS, tq, SB_DH), F32), pltpu.VMEM((SB_HEADS, tq, 1), F32)]
    out_shape = jax.ShapeDtypeStruct((bq.shape[0], SB_W), BF16)
    q_spec = pl.BlockSpec((tq, SB_W), lambda b, i: (off_q + b * nq + i, 0))
    if past_k is None:
        seq_spec = pl.BlockSpec((None, t * SB_HEADS, SB_DH), lambda b, i: (l, b, 0))
        args = [bq, k8, v8]
        prev_args, prev_specs, aliases = _alias_prev(prev, len(args))
        return pl.pallas_call(
            functools.partial(_sb_prompt_kernel, tq=tq, group=SB_HEADS // 2),
            out_shape=out_shape,
            grid=(bsz, nq),
            in_specs=[q_spec, seq_spec, seq_spec] + prev_specs,
            out_specs=q_spec,
            scratch_shapes=scratch,
            input_output_aliases=aliases,
            compiler_params=_params("parallel", "arbitrary"),
            name="stick_breaking_prompt",
        )(*args, *prev_args)
    assert nq == 1
    blk = SB_TQ * SB_HEADS
    n_past = past_k.shape[2] // blk
    n_tail = n_past - 1
    state_shape = jax.ShapeDtypeStruct((bsz, SB_HEADS, tq, SB_DH), F32)
    state_blk = (1, SB_HEADS, tq, SB_DH)
    q_spec = pl.BlockSpec((tq, SB_W), lambda b, s: (off_q + b, 0))
    new_spec = pl.BlockSpec((None, t * SB_HEADS, SB_DH), lambda b, s: (l, b, 0))
    near_spec = pl.BlockSpec((None, None, blk, SB_DH), lambda b, s: (l, b, n_tail, 0))
    state_spec = pl.BlockSpec(state_blk, lambda b, s: (b, 0, 0, 0))
    args = [bq, k8, v8, past_k, past_v]
    prev_args, prev_specs, aliases = _alias_prev(prev, len(args))
    hb, acc, carry = pl.pallas_call(
        functools.partial(_sb_head_kernel, tq=tq),
        out_shape=(out_shape, state_shape, state_shape),
        grid=(bsz, 2),
        in_specs=[q_spec, new_spec, new_spec, near_spec, near_spec] + prev_specs,
        out_specs=(q_spec, state_spec, state_spec),
        scratch_shapes=scratch,
        input_output_aliases=aliases,
        compiler_params=_params("parallel", "arbitrary"),
        name="stick_breaking_sample",
    )(*args, *prev_args)

    alive = jnp.max(carry, axis=(1, 2, 3)) > SB_DEAD_CARRY
    fetch = jnp.maximum(lax.cummax(jnp.where(alive, jnp.arange(bsz), -1), axis=0), 0)
    tail_spec = pl.BlockSpec((None, None, n_tail * blk, SB_DH),
                             lambda b, alive_ref, fetch_ref: (l, fetch_ref[b], 0, 0))
    state_in_spec = pl.BlockSpec(state_blk, lambda b, alive_ref, fetch_ref: (fetch_ref[b], 0, 0, 0))
    rows_spec = pl.BlockSpec((tq, SB_W), lambda b, alive_ref, fetch_ref: (off_q + b, 0))
    return pl.pallas_call(
        functools.partial(_sb_tail_kernel, n_tail=n_tail),
        out_shape=out_shape,
        grid_spec=pltpu.PrefetchScalarGridSpec(
            num_scalar_prefetch=2,
            grid=(bsz,),
            in_specs=[
                pl.BlockSpec((tq, SB_W), lambda b, alive_ref, fetch_ref: (off_q + fetch_ref[b], 0)),
                tail_spec, tail_spec, state_in_spec, state_in_spec,
                rows_spec,
            ],
            out_specs=rows_spec,
            scratch_shapes=scratch,
        ),
        input_output_aliases={7: 0},
        compiler_params=_params("arbitrary"),
        name="stick_breaking_tail",
    )(alive.astype(jnp.int32), fetch.astype(jnp.int32), bq, past_k, past_v, acc, carry, hb)


def _pool_kernel(u_ref, buf_ref, w_ref, s_ref, *rest, tt, pos0):
    o_ref, ext_ref = rest[-2:]
    i = pl.program_id(1)

    @pl.when(i == 0)
    def _():
        ext_ref[0:POOL_HALO, :] = buf_ref[0]

    @pl.when(i > 0)
    def _():
        ext_ref[0:POOL_HALO, :] = ext_ref[tt:tt + POOL_HALO, :]

    ext_ref[POOL_HALO:POOL_HALO + tt, :] = u_ref[...]
    pos = pos0 + i * tt + lax.broadcasted_iota(jnp.int32, (tt, 1), 0)
    for gi, w in enumerate(POOL_WINDOWS):
        cs = slice(gi * POOL_G, (gi + 1) * POOL_G)
        cur = ext_ref[POOL_HALO:POOL_HALO + tt, cs]
        wsum = cur
        for j in range(1, w):
            wsum = wsum + ext_ref[POOL_HALO - j:POOL_HALO - j + tt, cs]
        cnt = jnp.minimum(pos + 1, w).astype(F32)
        d = (wsum / cnt - cur).astype(BF16)
        y = _dot(d, w_ref[gi]) * s_ref[:, cs]
        o_ref[:, cs] = y.astype(BF16)


def _pool(u, buf, w_pool, scale, l, prev, *, bsz, t, row_off, pos0):
    tt = min(256, t)
    nt = t // tt
    off = row_off // tt
    row_spec = pl.BlockSpec((tt, POOL_W), lambda b, i: (off + b * nt + i, 0))
    args = [u, buf, w_pool, scale.reshape(1, POOL_W)]
    prev_args, prev_specs, aliases = _alias_prev(prev, len(args))
    return pl.pallas_call(
        functools.partial(_pool_kernel, tt=tt, pos0=pos0),
        out_shape=jax.ShapeDtypeStruct((u.shape[0], POOL_W), BF16),
        grid=(bsz, nt),
        in_specs=[
            row_spec,
            pl.BlockSpec((1, POOL_HALO, POOL_W), lambda b, i: (b, 0, 0)),
            pl.BlockSpec((None, len(POOL_WINDOWS), POOL_G, POOL_G), lambda b, i: (l, 0, 0, 0)),
            pl.BlockSpec((1, POOL_W), lambda b, i: (0, 0)),
        ] + prev_specs,
        out_specs=row_spec,
        scratch_shapes=[pltpu.VMEM((POOL_HALO + tt, POOL_W), F32)],
        input_output_aliases=aliases,
        compiler_params=_params("parallel", "arbitrary"),
        name="pool_mix",
    )(*args, *prev_args)


X_SCALE = X_DH ** -0.5


def _cross_kernel(q_ref, k_ref, v_ref, *rest):
    o_ref = rest[-1]
    for h in range(X_HEADS):
        cs = slice(h * X_DH, (h + 1) * X_DH)
        k = k_ref[0, :, cs].astype(BF16)
        v = v_ref[0, :, cs].astype(BF16)
        s = lax.dot_general(q_ref[:, cs], k, _NT, preferred_element_type=F32) * X_SCALE
        e = jnp.exp(s - jnp.max(s, axis=-1, keepdims=True))
        p = e / jnp.sum(e, axis=-1, keepdims=True)
        o_ref[:, cs] = _dot(p.astype(BF16), v).astype(BF16)


def _cross_attend(qx, mk, mv, l, prev, *, bsz, t, row_off):
    tq = min(256, t)
    nq = t // tq
    off = row_off // tq
    row_spec = pl.BlockSpec((tq, X_W), lambda b, i: (off + b * nq + i, 0))
    mem_spec = pl.BlockSpec((None, 1, mk.shape[2], X_W), lambda b, i: (l, b, 0, 0))
    args = [qx, mk, mv]
    prev_args, prev_specs, aliases = _alias_prev(prev, len(args))
    return pl.pallas_call(
        _cross_kernel,
        out_shape=jax.ShapeDtypeStruct((qx.shape[0], X_W), BF16),
        grid=(bsz, nq),
        in_specs=[row_spec, mem_spec, mem_spec] + prev_specs,
        out_specs=row_spec,
        input_output_aliases=aliases,
        compiler_params=_params("parallel", "arbitrary"),
        name="cross_attend",
    )(*args, *prev_args)


def _merge_kernel(x_ref, wg0_ref, wg1_ref, wg2_ref, bg0_ref, bg1_ref, bg2_ref,
                  ha_ref, hb_ref, hc_ref, wa_ref, wb_ref, wc_ref, side_in_ref,
                  o_ref, side_out_ref, *, side_slabs):
    step = pl.program_id(0) * pl.num_programs(1) + pl.program_id(1)

    @pl.when(step < side_slabs)
    def _():
        side_out_ref[...] = side_in_ref[...].astype(BF16)

    x = x_ref[...]
    out = None
    for wg_ref, bg_ref, h_ref, wp_ref in ((wg0_ref, bg0_ref, ha_ref, wa_ref),
                                          (wg1_ref, bg1_ref, hb_ref, wb_ref),
                                          (wg2_ref, bg2_ref, hc_ref, wc_ref)):
        gate = jax.nn.sigmoid(lax.dot_general(x, wg_ref[...], _NT, preferred_element_type=F32)
                              + bg_ref[...])
        term = gate * _dot(h_ref[...], wp_ref[...])
        out = term if out is None else out + term
    o_ref[...] = out.astype(BF16)


MERGE_SIDE_ROWS = 16


def _merge(xb, wt_b, bg, ha, hb, hc, w_pa, w_pb, w_pc, side_w, l, *, bm=512, bn=256):
    m = xb.shape[0]
    _, sr, sc = side_w.shape
    side_slabs = sr // MERGE_SIDE_ROWS
    assert sr % MERGE_SIDE_ROWS == 0 and side_slabs <= (m // bm) * (D_MODEL // bn)
    nb = D_MODEL // bn
    gb = B_G // bn
    row = lambda width: pl.BlockSpec((bm, width), lambda i, j: (i, 0))
    colw = lambda kdim: pl.BlockSpec((None, kdim, bn), lambda i, j: (0, 0, j))
    gate_w = [pl.BlockSpec((None, bn, D_MODEL), lambda i, j, g=g: (0, gb + g * nb + j, 0))
              for g in range(N_BRANCH)]
    gate_b = [pl.BlockSpec((1, bn), lambda i, j, g=g: (0, g * nb + j)) for g in range(N_BRANCH)]
    slab = lambda i, j: jnp.minimum(i * nb + j, side_slabs - 1)
    return pl.pallas_call(
        functools.partial(_merge_kernel, side_slabs=side_slabs),
        out_shape=(jax.ShapeDtypeStruct((m, D_MODEL), BF16),
                   jax.ShapeDtypeStruct((1, sr, sc), BF16)),
        grid=(m // bm, nb),
        in_specs=[row(D_MODEL)] + gate_w + gate_b
                 + [row(MLSTM_W), row(SB_W), row(POOL_W), colw(MLSTM_W), colw(SB_W), colw(POOL_W),
                    pl.BlockSpec((None, MERGE_SIDE_ROWS, sc), lambda i, j: (l, slab(i, j), 0))],
        out_specs=(pl.BlockSpec((bm, bn), lambda i, j: (i, j)),
                   pl.BlockSpec((None, MERGE_SIDE_ROWS, sc), lambda i, j: (0, slab(i, j), 0))),
        compiler_params=_params("arbitrary", "arbitrary"),
        name="gated_merge",
    )(xb, wt_b, wt_b, wt_b, bg, bg, bg, ha, hb, hc, w_pa, w_pb, w_pc, side_w)


def _trunk_layer(x, xb, W, l, groups, kv_prev, bufs, last):
    b_in = W["b_in"][l]
    proj_a = functools.partial(_matmul, xb, W["wt_in"], l, wt=True, bn=512)
    group_rows = tuple(g["bsz"] * g["t"] for g in groups)

    k_scale = jnp.concatenate([jnp.ones((MLSTM_QK_W,), F32),
                               jnp.full((MLSTM_QK_W,), MLSTM_DK ** -0.5, F32),
                               jnp.ones((MLSTM_W,), F32)])
    qkv, wt_b = proj_a(n=OFF_AO, bias=b_in[:OFF_AO], scale=k_scale, out_dtype=BF16,
                       side_cast=[(W["wt_in"], l, OFF_BQ, IN_W - OFF_BQ, 256)], name="proj_qkv")
    proj_b = functools.partial(_matmul, xb, wt_b, 0, wt=True)
    whole = lambda name, slab: (W[name], l, 0, W[name].shape[1], slab)
    ao, w_out = proj_a(n=MLSTM_W, col_off=OFF_AO, bias=b_in[OFF_AO:OFF_AI], act="sigmoid",
                       side_cast=[whole("w_out", 128)], name="proj_ogate")
    b_if = jnp.pad(b_in[OFF_AI:OFF_BQ], (0, GATE_PAD - 2 * MLSTM_HEADS))
    gif, w_pa, w_pb, w_pc = proj_a(
        n=GATE_PAD, col_off=OFF_AI, bias=b_if, bn=GATE_PAD,
        side_cast=[whole("w_pa", 256), whole("w_pb", 128), whole("w_pc", 128)], name="proj_if")
    sec = lambda i: b_in[OFF_BQ + i * SB_W:OFF_BQ + (i + 1) * SB_W]
    bq, wq_x, wo_x = proj_b(n=SB_W, bias=sec(0), out_dtype=BF16,
                            side_cast=[whole("wq_x", 512), whole("wo_x", 128)], name="proj_sb_q")
    k8 = _proj_heads(xb, wt_b, l, sec(1), B_BK, group_rows, kv_prev[0], name="proj_sb_k")
    v8 = _proj_heads(xb, wt_b, l, sec(2), B_BV, group_rows, kv_prev[1], name="proj_sb_v")
    u = proj_b(n=POOL_W, col_off=B_CU, bias=sec(3), name="proj_pool_u")

    rows = xb.shape[0]
    if bufs is None:
        bufs = tuple(jnp.zeros((rows, w), BF16) for w in (MLSTM_W, SB_W, POOL_W, X_W))
    ha, hb, hc, ox = bufs
    states = []
    for gi, g in enumerate(groups):
        dims = dict(bsz=g["bsz"], t=g["t"], row_off=g["row_off"])
        ha, c_new, n_new, m_new = _mlstm(qkv, ao, gif, g["c"], g["n"], g["m"],
                                         W["mlstm_norm_g"][l], l, ha, **dims)
        hb = _stick_breaking(bq, k8[gi], v8[gi], g["sb_k"], g["sb_v"], l, hb, **dims)
        hc = _pool(u, g["pool_buf"][l], W["w_pool"], W["pool_scale"][l], l, hc,
                   pos0=g["pos0"], **dims)
        states.append((c_new, n_new, m_new))

    merged, w_up = _merge(xb, wt_b, b_in[OFF_G:].reshape(1, -1), ha, hb, hc,
                          w_pa, w_pb, w_pc, W["w_up"], l)
    y = _matmul(merged, w_out, 0, n=D_MODEL, res=x, name="w_out")
    x, xb = _layer_norm(y, W["ln1_g"][l], W["ln1_b"][l])

    qx = _matmul(xb, wq_x, 0, n=X_W, out_dtype=BF16, name="wq_x")
    for g in groups:
        ox = _cross_attend(qx, g["mk"], g["mv"], l, ox, bsz=g["bsz"], t=g["t"],
                           row_off=g["row_off"])
    y = _matmul(ox, wo_x, 0, n=D_MODEL, res=x, name="wo_x")
    x, xb = _layer_norm(y, W["ln2_g"][l], W["ln2_b"][l])

    hid, w_down = _matmul(xb, w_up, 0, n=D_FF, act="relu2", out_dtype=BF16,
                          side_cast=[whole("w_down", 128)], name="w_up")
    y = _matmul(hid, w_down, 0, n=D_MODEL, bk=D_MODEL, name="w_down")
    if last:
        x = tuple(_layer_norm(x, W["ln3_g"][l], W["ln3_b"][l], res=y, in_off=g["row_off"],
                              n_rows=g["bsz"] * g["t"], emit_bf16=False) for g in groups)
        xb = None
    else:
        x, xb = _layer_norm(x, W["ln3_g"][l], W["ln3_b"][l], res=y)
    return x, xb, (k8, v8), u, states, (ha, hb, hc, ox)


def kernel(x_prompt, x_sample, cache_sb_k, cache_sb_v, state_mlstm_c, state_mlstm_n, state_mlstm_m, state_pool, cache_mem_k, cache_mem_v, mem_prompt, ln_in_g, ln_in_b, w_in, b_in, mlstm_norm_g, w_pool, pool_scale, w_pa, w_pb, w_pc, w_out, ln1_g, ln1_b, wq_x, wk_x, wv_x, wo_x, ln2_g, ln2_b, w_up, w_down, ln3_g, ln3_b):
    bf = lambda a: a.astype(BF16)
    wt_in = jnp.swapaxes(w_in, 1, 2)
    W = dict(wt_in=wt_in,
             b_in=b_in, mlstm_norm_g=mlstm_norm_g, w_pool=bf(w_pool),
             pool_scale=pool_scale, w_pa=w_pa, w_pb=w_pb, w_pc=w_pc, w_out=w_out,
             ln1_g=ln1_g, ln1_b=ln1_b, wq_x=wq_x, wo_x=wo_x, ln2_g=ln2_g, ln2_b=ln2_b,
             w_up=w_up, w_down=w_down, ln3_g=ln3_g, ln3_b=ln3_b)
    bp, tp, _ = x_prompt.shape
    bs, ts, _ = x_sample.shape
    rows_p, rows_s = bp * tp, bs * ts
    rows = rows_p + rows_s
    past = cache_sb_k.shape[2]
    mem_len = mem_prompt.shape[1]

    mem_b = mem_prompt.reshape(bp * mem_len, D_MODEL).astype(BF16)
    wk_b, wv_b = bf(wk_x), bf(wv_x)
    mem4 = lambda a, b: a.reshape(DEPTH, b, mem_len, X_W)
    mk_p = mem4(jnp.stack([_matmul(mem_b, wk_b, l, n=X_W, name="mem_k") for l in range(DEPTH)]), bp)
    mv_p = mem4(jnp.stack([_matmul(mem_b, wv_b, l, n=X_W, name="mem_v") for l in range(DEPTH)]), bp)

    lane_m = lambda m: jnp.broadcast_to(m[..., None], m.shape + (LANES,))
    halo = lambda b: jnp.pad(b, ((0, 0), (0, 0), (POOL_HALO - POOL_BUF, 0), (0, 0)))
    heads_rows = lambda c: c.reshape(DEPTH, bs, past * SB_HEADS, SB_DH)
    groups = [
        dict(bsz=bp, t=tp, row_off=0, pos0=0,
             c=jnp.zeros((DEPTH, bp, MLSTM_HEADS, MLSTM_DK, MLSTM_DV), F32),
             n=jnp.zeros((DEPTH, bp, MLSTM_HEADS, MLSTM_DK), F32),
             m=jnp.zeros((DEPTH, bp, MLSTM_HEADS, LANES), F32),
             sb_k=None, sb_v=None,
             pool_buf=jnp.zeros((DEPTH, bp, POOL_HALO, POOL_W), F32),
             mk=mk_p, mv=mv_p),
        dict(bsz=bs, t=ts, row_off=rows_p, pos0=past,
             c=state_mlstm_c, n=state_mlstm_n, m=lane_m(state_mlstm_m),
             sb_k=heads_rows(cache_sb_k), sb_v=heads_rows(cache_sb_v),
             pool_buf=halo(state_pool),
             mk=mem4(cache_mem_k, bs), mv=mem4(cache_mem_v, bs)),
    ]

    xs = None
    for g, xin in zip(groups, (x_prompt, x_sample)):
        xs = _layer_norm(xin.reshape(g["bsz"] * g["t"], D_MODEL), ln_in_g, ln_in_b,
                         out_off=g["row_off"], out_rows=rows, prev=xs)
    x, xb = xs

    kv = tuple(tuple(jnp.zeros((DEPTH, g["bsz"] * g["t"] * SB_HEADS, SB_DH), F32) for g in groups)
               for _ in range(2))
    per_layer = []
    bufs = None
    for l in range(DEPTH):
        x, xb, kv, u, states, bufs = _trunk_layer(x, xb, W, l, groups, kv, bufs,
                                                  last=l == DEPTH - 1)
        per_layer.append((u, states))

    def group_out(gi, g):
        bsz, t, r0 = g["bsz"], g["t"], g["row_off"]
        sbk = kv[0][gi].reshape(DEPTH, bsz, t, SB_HEADS, SB_DH)
        sbv = kv[1][gi].reshape(DEPTH, bsz, t, SB_HEADS, SB_DH)
        c = jnp.stack([st[gi][0] for _, st in per_layer])
        n = jnp.stack([st[gi][1] for _, st in per_layer])
        m = jnp.stack([st[gi][2][..., 0] for _, st in per_layer])
        pool = jnp.stack([u[r0:r0 + bsz * t].reshape(bsz, t, POOL_W)[:, t - POOL_BUF:, :]
                          for u, _ in per_layer])
        return sbk, sbv, c, n, m, pool

    out_p = group_out(0, groups[0])
    out_s = group_out(1, groups[1])
    y_prompt = x[0].reshape(bp, tp, D_MODEL)
    y_sample = x[1].reshape(bs, ts, D_MODEL)
    mem5 = lambda a: a.reshape(DEPTH, bp, mem_len, X_HEADS, X_DH)
    return (y_prompt, y_sample) + out_p + (mem5(mk_p), mem5(mv_p)) + out_s
```

```python
import functools

import jax
import jax.numpy as jnp
from jax import lax
from jax.experimental import pallas as pl
from jax.experimental.pallas import tpu as pltpu

F32 = jnp.float32
BF16 = jnp.bfloat16

D_MODEL = 4096
DEPTH = 2
MLSTM_CHUNK = 256
MLSTM_HEADS = 8
MLSTM_W = D_MODEL // 2
MLSTM_DV = MLSTM_W // MLSTM_HEADS
MLSTM_DK = MLSTM_DV // 2
MLSTM_QK_W = MLSTM_HEADS * MLSTM_DK
SB_HEADS = 8
SB_W = D_MODEL // 4
SB_DH = SB_W // SB_HEADS
POOL_WINDOWS = (2, 4, 8, 16)
POOL_W = D_MODEL // 4
POOL_G = POOL_W // len(POOL_WINDOWS)
POOL_BUF = max(POOL_WINDOWS) - 1
POOL_HALO = POOL_BUF + 1
X_HEADS = 4
X_W = D_MODEL // 4
X_DH = X_W // X_HEADS
D_FF = 4 * D_MODEL
N_BRANCH = 3
ALPHA = (2 * DEPTH) ** 0.25
LN_EPS = 1e-5
HEAD_NORM_EPS = 1e-6

OFF_AO = 2 * MLSTM_QK_W + MLSTM_W
OFF_AI = OFF_AO + MLSTM_W
OFF_BQ = OFF_AI + 2 * MLSTM_HEADS
OFF_G = OFF_BQ + 3 * SB_W + POOL_W
IN_W = OFF_G + N_BRANCH * D_MODEL

V7X_VMEM_LIMIT_BYTES = 56 * 1024 * 1024
LANES = 128
SUBLANES = 8
SB_TQ = 256
GATE_PAD = LANES

B_BK = SB_W
B_BV = 2 * SB_W
B_CU = 3 * SB_W
B_G = 3 * SB_W + POOL_W

_NT = (((1,), (1,)), ((), ()))
_TN = (((0,), (0,)), ((), ()))


def _params(*sem):
    return pltpu.CompilerParams(dimension_semantics=sem,
                                vmem_limit_bytes=V7X_VMEM_LIMIT_BYTES)


def _dot(a, b):
    return jnp.dot(a, b, preferred_element_type=F32)


def _split(x, terms):
    out = []
    for _ in range(terms - 1):
        h = x.astype(BF16)
        out.append(h)
        x = x - h.astype(F32)
    out.append(x.astype(BF16))
    return out


def _apply_ln(x, mu_ref, rs_ref, g_ref, b_ref):
    return (x - mu_ref[:, 0:1]) * rs_ref[:, 0:1] * g_ref[...] + b_ref[...]


def _alias_prev(prev, n_in):
    if prev is None:
        return [], [], {}
    prev = list(prev) if isinstance(prev, (tuple, list)) else [prev]
    specs = [pl.BlockSpec(memory_space=pl.ANY)] * len(prev)
    return prev, specs, {n_in + i: i for i in range(len(prev))}


def _mm_kernel(*refs, act, has_bias, has_scale, has_res, res_ln, nk, side_slabs, wt):
    it = iter(refs)
    x_ref, w_ref = next(it), next(it)
    b_ref = next(it) if has_bias else None
    s_ref = next(it) if has_scale else None
    r_ref = next(it) if has_res else None
    ln_refs = [next(it) for _ in range(4)] if res_ln else None
    side_in_refs = [next(it) for _ in side_slabs]
    o_ref = next(it)
    side_out_refs = [next(it) for _ in side_slabs]
    acc_ref = next(it) if nk > 1 else None

    if side_slabs:
        step = pl.program_id(0) * pl.num_programs(1) + pl.program_id(1)
        for n_slabs, in_ref, out_ref in zip(side_slabs, side_in_refs, side_out_refs):
            @pl.when(step < n_slabs)
            def _(in_ref=in_ref, out_ref=out_ref):
                out_ref[...] = in_ref[0].astype(BF16)

    def epilogue(y):
        if has_bias:
            y = y + b_ref[...]
        if has_scale:
            y = y * s_ref[...]
        if act == "sigmoid":
            y = jax.nn.sigmoid(y)
        elif act == "relu2":
            y = jnp.square(jnp.maximum(y, 0.0))
        if has_res:
            r = r_ref[...]
            if res_ln:
                r = _apply_ln(r, *ln_refs)
            y = ALPHA * r + y
        o_ref[...] = y.astype(o_ref.dtype)

    w = w_ref[...].astype(BF16)
    part = lax.dot_general(x_ref[...], w, _NT if wt else (((1,), (0,)), ((), ())),
                           preferred_element_type=F32)
    if nk == 1:
        epilogue(part)
    else:
        k = pl.program_id(2)

        @pl.when(k == 0)
        def _():
            acc_ref[...] = part

        @pl.when(k > 0)
        def _():
            acc_ref[...] += part

        @pl.when(k == nk - 1)
        def _():
            epilogue(acc_ref[...])


def _matmul(x, w, l, *, n, col_off=0, bias=None, scale=None, res=None, res_ln=None, act="none",
            out_dtype=F32, bm=1024, bn=1024, bk=None, side_cast=(), wt=False, name="matmul"):
    m, kdim = x.shape
    bm, bn = min(bm, m), min(bn, n)
    bk = kdim if bk is None else bk
    assert m % bm == 0 and n % bn == 0 and kdim % bk == 0 and col_off % bn == 0
    nk = kdim // bk
    cb = col_off // bn
    nj = n // bn
    w_spec = (pl.BlockSpec((None, bn, bk), lambda i, j, k: (l, cb + j, k)) if wt
              else pl.BlockSpec((None, bk, bn), lambda i, j, k: (l, k, cb + j)))
    in_specs = [pl.BlockSpec((bm, bk), lambda i, j, k: (i, k)), w_spec]
    args = [x, w]
    for v in (bias, scale):
        if v is not None:
            in_specs.append(pl.BlockSpec((1, bn), lambda i, j, k: (0, j)))
            args.append(v.reshape(1, n).astype(F32))
    if res is not None:
        in_specs.append(pl.BlockSpec((bm, bn), lambda i, j, k: (i, j)))
        args.append(res)
    if res_ln is not None:
        mu, rs, ln_g, ln_b = res_ln
        in_specs += [pl.BlockSpec((bm, LANES), lambda i, j, k: (i, 0))] * 2
        in_specs += [pl.BlockSpec((1, bn), lambda i, j, k: (0, j))] * 2
        args += [mu, rs, ln_g.reshape(1, n), ln_b.reshape(1, n)]
    out_shape = [jax.ShapeDtypeStruct((m, n), out_dtype)]
    out_specs = [pl.BlockSpec((bm, bn), lambda i, j, k: (i, j))]
    side_slabs = []
    for side_w, side_l, row_off, sr, slab_rows in side_cast:
        sc = side_w.shape[2]
        n_slabs = sr // slab_rows
        assert nk == 1 and sr % slab_rows == 0 and n_slabs <= (m // bm) * nj
        slab = lambda i, j, k, n_slabs=n_slabs: jnp.minimum(i * nj + j, n_slabs - 1)
        in_specs.append(pl.BlockSpec(
            (pl.Element(1), pl.Element(slab_rows), pl.Element(sc)),
            lambda i, j, k, slab=slab, side_l=side_l, row_off=row_off, slab_rows=slab_rows: (
                side_l, pl.multiple_of(row_off + slab(i, j, k) * slab_rows, SUBLANES), 0)))
        args.append(side_w)
        out_shape.append(jax.ShapeDtypeStruct((1, sr, sc), BF16))
        out_specs.append(pl.BlockSpec((None, slab_rows, sc),
                                      lambda i, j, k, slab=slab: (0, slab(i, j, k), 0)))
        side_slabs.append(n_slabs)
    kern = functools.partial(_mm_kernel, act=act, has_bias=bias is not None,
                             has_scale=scale is not None, has_res=res is not None,
                             res_ln=res_ln is not None, nk=nk,
                             side_slabs=tuple(side_slabs), wt=wt)
    out = pl.pallas_call(
        kern,
        out_shape=tuple(out_shape),
        grid=(m // bm, nj, nk),
        in_specs=in_specs,
        out_specs=tuple(out_specs),
        scratch_shapes=[pltpu.VMEM((bm, bn), F32)] if nk > 1 else [],
        compiler_params=(_params("arbitrary", "arbitrary", "arbitrary") if side_slabs
                         else _params("parallel", "parallel", "arbitrary")),
        name=name,
    )(*args)
    return out if side_slabs else out[0]


def _proj_heads_kernel(x_ref, w_ref, b_ref, *rest, n_first):
    first_ref, second_ref = rest[-2:]
    y = lax.dot_general(x_ref[...], w_ref[...], _NT, preferred_element_type=F32) + b_ref[...]
    rows = y.shape[0]

    def store(o_ref):
        for h in range(SUBLANES):
            o_ref[pl.ds(h, rows, stride=SUBLANES), :] = y[:, h * LANES:(h + 1) * LANES]

    i = pl.program_id(0)

    @pl.when(i < n_first)
    def _():
        store(first_ref)

    @pl.when(i >= n_first)
    def _():
        store(second_ref)


def _proj_heads(x, w, l, bias, col_off, group_rows, prev, *, bm=1024, name="proj_heads"):
    m, kdim = x.shape
    n = SUBLANES * LANES
    rows_a, rows_b = group_rows
    assert rows_a % bm == 0 and rows_b % bm == 0 and rows_a + rows_b == m and col_off % n == 0
    n_first = rows_a // bm
    cb = col_off // n
    args = [x, w, bias.reshape(1, n)]
    prev_args, prev_specs, aliases = _alias_prev(prev, len(args))
    blk = (None, bm * SUBLANES, LANES)
    return pl.pallas_call(
        functools.partial(_proj_heads_kernel, n_first=n_first),
        out_shape=tuple(jax.ShapeDtypeStruct((DEPTH, r * SUBLANES, LANES), F32) for r in group_rows),
        grid=(m // bm,),
        in_specs=[pl.BlockSpec((bm, kdim), lambda i: (i, 0)),
                  pl.BlockSpec((None, n, kdim), lambda i: (0, cb, 0)),
                  pl.BlockSpec((1, n), lambda i: (0, 0))] + prev_specs,
        out_specs=(pl.BlockSpec(blk, lambda i: (l, jnp.minimum(i, n_first - 1), 0)),
                   pl.BlockSpec(blk, lambda i: (l, jnp.maximum(i - n_first, 0), 0))),
        input_output_aliases=aliases,
        compiler_params=_params("arbitrary"),
        name=name,
    )(*args, *prev_args)


def _ln_kernel(*refs, x_ln, has_res, n_prev, emit_f32, emit_bf16, n_blocks):
    it = iter(refs)
    x_ref = next(it)
    x_ln_refs = [next(it) for _ in range(4)] if x_ln else None
    r_ref = next(it) if has_res else None
    g_ref, b_ref = next(it), next(it)
    for _ in range(n_prev):
        next(it)
    x = x_ref[...]
    if x_ln:
        x = _apply_ln(x, *x_ln_refs)
    if has_res:
        x = ALPHA * x + r_ref[...]
    mu = jnp.mean(x, axis=-1, keepdims=True)
    xc = x - mu
    var = jnp.mean(xc * xc, axis=-1, keepdims=True)
    rs = lax.rsqrt(var + LN_EPS)
    y = xc * rs * g_ref[...] + b_ref[...]
    if n_blocks is not None:
        y = jnp.where(pl.program_id(0) < n_blocks, y, 0.0)
    if emit_f32:
        next(it)[...] = y
    else:
        mu_ref, rs_ref = next(it), next(it)
        mu_ref[...] = jnp.broadcast_to(mu, mu_ref.shape)
        rs_ref[...] = jnp.broadcast_to(rs, rs_ref.shape)
    if emit_bf16:
        next(it)[...] = y.astype(BF16)


def _layer_norm(x, g, b, res=None, *, x_ln=None, in_off=0, n_rows=None, out_off=0, out_rows=None,
                prev=None, emit_f32=True, emit_bf16=True, rows=256):
    d = x.shape[1]
    n_rows = x.shape[0] - in_off if n_rows is None else n_rows
    out_rows = n_rows if out_rows is None else out_rows
    ib, ob = in_off // rows, out_off // rows
    n_blocks = n_rows // rows
    fill = prev is None and out_rows > n_rows
    assert not fill or out_off == 0
    in_spec = pl.BlockSpec((rows, d), lambda i: (ib + jnp.minimum(i, n_blocks - 1), 0))
    stat_in_spec = pl.BlockSpec((rows, LANES), lambda i: (ib + jnp.minimum(i, n_blocks - 1), 0))
    out_spec = pl.BlockSpec((rows, d), lambda i: (ob + i, 0))
    stat_spec = pl.BlockSpec((rows, LANES), lambda i: (ob + i, 0))
    vec_spec = pl.BlockSpec((1, d), lambda i: (0, 0))
    args, in_specs = [x], [in_spec]
    if x_ln is not None:
        args += [x_ln[0], x_ln[1], x_ln[2].reshape(1, d), x_ln[3].reshape(1, d)]
        in_specs += [stat_in_spec, stat_in_spec, vec_spec, vec_spec]
    if res is not None:
        args.append(res)
        in_specs.append(in_spec)
    args += [g.reshape(1, d), b.reshape(1, d)]
    in_specs += [vec_spec, vec_spec]
    prev_args, prev_specs, aliases = _alias_prev(prev, len(args))
    if emit_f32:
        out_shape, out_specs = [jax.ShapeDtypeStruct((out_rows, d), F32)], [out_spec]
    else:
        out_shape = [jax.ShapeDtypeStruct((out_rows, LANES), F32)] * 2
        out_specs = [stat_spec, stat_spec]
    if emit_bf16:
        out_shape.append(jax.ShapeDtypeStruct((out_rows, d), BF16))
        out_specs.append(out_spec)
    out = pl.pallas_call(
        functools.partial(_ln_kernel, x_ln=x_ln is not None, has_res=res is not None,
                          n_prev=len(prev_args), emit_f32=emit_f32, emit_bf16=emit_bf16,
                          n_blocks=n_blocks if fill else None),
        out_shape=tuple(out_shape),
        grid=(out_rows // rows if fill else n_blocks,),
        in_specs=in_specs + prev_specs,
        out_specs=tuple(out_specs),
        input_output_aliases=aliases,
        compiler_params=_params("parallel"),
        name="layer_norm",
    )(*args, *prev_args)
    return out if len(out) > 1 else out[0]


def _mlstm_kernel(q_ref, k_ref, v_ref, ao_ref, gif_ref, c0_ref, n0_ref, m0_ref, g_ref, *rest):
    ha_ref, c_ref, n_ref, m_ref = rest[-4:]
    L = q_ref.shape[0]

    @pl.when(pl.program_id(1) == 0)
    def _():
        c_ref[...] = c0_ref[...]
        n_ref[...] = n0_ref[...]
        m_ref[...] = m0_ref[...]

    gif = gif_ref[...]
    logf = -(jnp.maximum(-gif, 0.0) + jnp.log1p(jnp.exp(-jnp.abs(gif))))
    row = lax.broadcasted_iota(jnp.int32, (L, L), 0)
    col = lax.broadcasted_iota(jnp.int32, (L, L), 1)
    causal = col <= row
    tril = jnp.where(causal, 1.0, 0.0).astype(BF16)
    b_all = sum(_dot(tril, t) for t in _split(logf, 3))
    sel_r = lax.broadcasted_iota(jnp.int32, (2 * MLSTM_HEADS, GATE_PAD), 0)
    sel_c = lax.broadcasted_iota(jnp.int32, (2 * MLSTM_HEADS, GATE_PAD), 1)
    sel = jnp.where(sel_r == sel_c, 1.0, 0.0).astype(BF16)

    def rows_of(x):
        return sum(lax.dot_general(sel, t, _NT, preferred_element_type=F32) for t in _split(x, 3))

    i_rows = rows_of(gif)
    b_rows = rows_of(b_all)

    for h in range(MLSTM_HEADS):
        qs = slice(h * MLSTM_DK, (h + 1) * MLSTM_DK)
        vs = slice(h * MLSTM_DV, (h + 1) * MLSTM_DV)
        q = q_ref[:, qs]
        k = k_ref[:, qs]
        v = v_ref[:, vs]
        i_col = gif[:, h:h + 1]
        b_col = b_all[:, MLSTM_HEADS + h:MLSTM_HEADS + h + 1]
        i_row = i_rows[h:h + 1, :]
        b_row = b_rows[MLSTM_HEADS + h:MLSTM_HEADS + h + 1, :]
        c_old = c_ref[0, h]
        n_old = n_ref[0, h:h + 1, :]
        m_old = m_ref[0, h:h + 1, 0:1]

        dmat = jnp.where(causal, b_col - b_row + i_row, -jnp.inf)
        m_inter = b_col + m_old
        m_t = jnp.maximum(m_inter, jnp.max(dmat, axis=-1, keepdims=True))
        s = lax.dot_general(q, k, _NT, preferred_element_type=F32) * jnp.exp(dmat - m_t)
        w_inter = jnp.exp(m_inter - m_t)
        kf = k.astype(F32)
        num = _dot(s.astype(BF16), v) + w_inter * _dot(q, c_old.astype(BF16))
        qn = jnp.sum(q.astype(F32) * n_old, axis=-1, keepdims=True)
        den = jnp.sum(s, axis=-1, keepdims=True) + w_inter * qn
        hh = num / jnp.maximum(jnp.abs(den), jnp.exp(-m_t))

        m_new = m_t[L - 1:L, :]
        b_last = b_col[L - 1:L, :]
        g_col = jnp.exp(b_last - b_col + i_col - m_new)
        decay = jnp.exp(b_last + m_old - m_new)
        kg = kf * g_col
        c_ref[0, h] = decay * c_old + lax.dot_general(kg.astype(BF16), v, _TN,
                                                      preferred_element_type=F32)
        n_ref[0, h:h + 1, :] = decay * n_old + jnp.sum(kg, axis=0, keepdims=True)
        m_ref[0, h:h + 1, :] = jnp.broadcast_to(m_new, (1, LANES))

        mu = jnp.mean(hh, axis=-1, keepdims=True)
        hc = hh - mu
        var = jnp.mean(hc * hc, axis=-1, keepdims=True)
        hn = hc * lax.rsqrt(var + HEAD_NORM_EPS) * g_ref[:, vs]
        ha_ref[:, vs] = (ao_ref[:, vs] * hn).astype(BF16)


def _mlstm(qkv, ao, gif, c0, n0, m0, norm_g, l, prev, *, bsz, t, row_off):
    chunk = min(MLSTM_CHUNK, t)
    nc = t // chunk
    off = row_off // chunk
    rows = lambda b, c: off + b * nc + c
    state4 = pl.BlockSpec((1, MLSTM_HEADS, MLSTM_DK, MLSTM_DV), lambda b, c: (b, 0, 0, 0))
    state3 = pl.BlockSpec((1, MLSTM_HEADS, LANES), lambda b, c: (b, 0, 0))
    init4 = pl.BlockSpec((None, 1, MLSTM_HEADS, MLSTM_DK, MLSTM_DV), lambda b, c: (l, b, 0, 0, 0))
    init3 = pl.BlockSpec((None, 1, MLSTM_HEADS, LANES), lambda b, c: (l, b, 0, 0))
    args = [qkv, qkv, qkv, ao, gif, c0, n0, m0, norm_g.reshape(1, MLSTM_W)]
    prev_args, prev_specs, aliases = _alias_prev(prev, len(args))
    return pl.pallas_call(
        _mlstm_kernel,
        out_shape=(jax.ShapeDtypeStruct((qkv.shape[0], MLSTM_W), BF16),
                   jax.ShapeDtypeStruct(c0.shape[1:], F32),
                   jax.ShapeDtypeStruct(n0.shape[1:], F32),
                   jax.ShapeDtypeStruct(m0.shape[1:], F32)),
        grid=(bsz, nc),
        in_specs=[
            pl.BlockSpec((chunk, MLSTM_QK_W), lambda b, c: (rows(b, c), 0)),
            pl.BlockSpec((chunk, MLSTM_QK_W), lambda b, c: (rows(b, c), 1)),
            pl.BlockSpec((chunk, MLSTM_W), lambda b, c: (rows(b, c), 1)),
            pl.BlockSpec((chunk, MLSTM_W), lambda b, c: (rows(b, c), 0)),
            pl.BlockSpec((chunk, GATE_PAD), lambda b, c: (rows(b, c), 0)),
            init4, init3, init3,
            pl.BlockSpec((1, MLSTM_W), lambda b, c: (0, 0)),
        ] + prev_specs,
        out_specs=(pl.BlockSpec((chunk, MLSTM_W), lambda b, c: (rows(b, c), 0)),
                   state4, state3, state3),
        input_output_aliases=aliases,
        compiler_params=_params("parallel", "arbitrary"),
        name="mlstm",
    )(*args, *prev_args)


SB_SCALE = SB_DH ** -0.5
SB_DEAD_CARRY = -110.0


def _lower_tri(n):
    return jnp.where(lax.broadcasted_iota(jnp.int32, (n, n), 0)
                     >= lax.broadcasted_iota(jnp.int32, (n, n), 1), 1.0, 0.0).astype(BF16)


def _sb_group(q_ref, load_k, load_v, heads, acc_ref, carry_ref, lower, *, diag):
    tq = q_ref.shape[0]
    tk = lower.shape[0]
    if diag:
        mask = (lax.broadcasted_iota(jnp.int32, (tq, tk), 1)
                < lax.broadcasted_iota(jnp.int32, (tq, tk), 0))
    zs, lks = [], []
    for h in heads:
        z = lax.dot_general(q_ref[:, h * SB_DH:(h + 1) * SB_DH], load_k(h), _NT,
                            preferred_element_type=F32) * SB_SCALE
        log_keep = -(jnp.maximum(z, 0.0) + jnp.log(1.0 + jnp.exp(-jnp.abs(z))))
        if diag:
            log_keep = jnp.where(mask, log_keep, 0.0)
        zs.append(z)
        lks.append(log_keep)
    stacked = jnp.concatenate(lks, axis=0)
    rc_all = sum(_dot(t, lower) for t in _split(stacked, 2))
    for i, h in enumerate(heads):
        rc = rc_all[i * tq:(i + 1) * tq]
        carry = carry_ref[h]
        attn = jnp.exp(zs[i] + rc + carry)
        if diag:
            attn = jnp.where(mask, attn, 0.0)
        acc_ref[h] += _dot(attn.astype(BF16), load_v(h))
        carry_ref[h] = carry + rc[:, 0:1]


def _sb_block(q_ref, k_ref, v_ref, base, tk, acc_ref, carry_ref, lower, *, group, diag):
    def loader(ref):
        return lambda h: ref[pl.ds(base + h, tk, stride=SB_HEADS), :].astype(BF16)
    for g in range(0, SB_HEADS, group):
        _sb_group(q_ref, loader(k_ref), loader(v_ref), range(g, g + group),
                  acc_ref, carry_ref, lower, diag=diag)


def _sb_alive(carry_ref):
    return jnp.max(carry_ref[...]) > SB_DEAD_CARRY


def _sb_store(o_ref, acc_ref):
    for h in range(SB_HEADS):
        o_ref[:, h * SB_DH:(h + 1) * SB_DH] = acc_ref[h].astype(BF16)


def _sb_prompt_kernel(q_ref, k_ref, v_ref, *rest, tq, group):
    o_ref, acc_ref, carry_ref = rest[-3:]
    qi = pl.program_id(1)
    acc_ref[...] = jnp.zeros_like(acc_ref)
    carry_ref[...] = jnp.zeros_like(carry_ref)
    lower = _lower_tri(tq)
    blk_rows = tq * SB_HEADS

    _sb_block(q_ref, k_ref, v_ref, pl.multiple_of(qi * blk_rows, blk_rows), tq,
              acc_ref, carry_ref, lower, group=group, diag=True)

    def body(j, c):
        @pl.when(_sb_alive(carry_ref))
        def _():
            base = pl.multiple_of((qi - 1 - j) * blk_rows, blk_rows)
            _sb_block(q_ref, k_ref, v_ref, base, tq, acc_ref, carry_ref, lower,
                      group=group, diag=False)
        return c

    lax.fori_loop(0, qi, body, 0)
    _sb_store(o_ref, acc_ref)


def _sb_head_kernel(q_ref, k_ref, v_ref, kp_ref, vp_ref, *rest, tq):
    o_ref, acc_out_ref, carry_out_ref, acc_ref, carry_ref = rest[-5:]
    s = pl.program_id(1)

    @pl.when(s == 0)
    def _():
        acc_ref[...] = jnp.zeros_like(acc_ref)
        carry_ref[...] = jnp.zeros_like(carry_ref)
        _sb_block(q_ref, k_ref, v_ref, 0, tq, acc_ref, carry_ref, _lower_tri(tq),
                  group=SB_HEADS, diag=True)

    @pl.when(s == 1)
    def _():
        @pl.when(_sb_alive(carry_ref))
        def _():
            _sb_block(q_ref, kp_ref, vp_ref, 0, SB_TQ, acc_ref, carry_ref, _lower_tri(SB_TQ),
                      group=SB_HEADS, diag=False)

        _sb_store(o_ref, acc_ref)
        acc_out_ref[0] = acc_ref[...]
        carry_out_ref[0] = jnp.broadcast_to(carry_ref[...], carry_out_ref.shape[1:])


def _sb_tail_kernel(alive_ref, fetch_ref, q_ref, kp_ref, vp_ref, acc_in_ref, carry_in_ref, hb_ref,
                    o_ref, acc_ref, carry_ref, *, n_tail):
    del fetch_ref
    b = pl.program_id(0)

    @pl.when(alive_ref[b] == 0)
    def _():
        o_ref[...] = hb_ref[...]

    @pl.when(alive_ref[b] != 0)
    def _():
        acc_ref[...] = acc_in_ref[0]
        carry_ref[...] = carry_in_ref[0][:, :, 0:1]
        lower = _lower_tri(SB_TQ)
        blk_rows = SB_TQ * SB_HEADS

        def body(j, c):
            @pl.when(_sb_alive(carry_ref))
            def _():
                base = pl.multiple_of((n_tail - 1 - j) * blk_rows, blk_rows)
                _sb_block(q_ref, kp_ref, vp_ref, base, SB_TQ, acc_ref, carry_ref, lower,
                          group=SB_HEADS, diag=False)
            return c

        lax.fori_loop(0, n_tail, body, 0)
        _sb_store(o_ref, acc_ref)


def _stick_breaking(bq, k8, v8, past_k, past_v, l, prev, *, bsz, t, row_off):
    tq = min(SB_TQ, t)
    nq = t // tq
    off_q = row_off // tq
    scratch = [pltpu.VMEM((SB_HEADS, tq, SB_DH), F32), pltpu.VMEM((SB_HEADS, tq, 1), F32)]
    out_shape = jax.ShapeDtypeStruct((bq.shape[0], SB_W), BF16)
    q_spec = pl.BlockSpec((tq, SB_W), lambda b, i: (off_q + b * nq + i, 0))
    if past_k is None:
        seq_spec = pl.BlockSpec((None, t * SB_HEADS, SB_DH), lambda b, i: (l, b, 0))
        args = [bq, k8, v8]
        prev_args, prev_specs, aliases = _alias_prev(prev, len(args))
        return pl.pallas_call(
            functools.partial(_sb_prompt_kernel, tq=tq, group=SB_HEADS // 2),
            out_shape=out_shape,
            grid=(bsz, nq),
            in_specs=[q_spec, seq_spec, seq_spec] + prev_specs,
            out_specs=q_spec,
            scratch_shapes=scratch,
            input_output_aliases=aliases,
            compiler_params=_params("parallel", "arbitrary"),
            name="stick_breaking_prompt",
        )(*args, *prev_args)
    assert nq == 1
    blk = SB_TQ * SB_HEADS
    n_past = past_k.shape[2] // blk
    n_tail = n_past - 1
    state_shape = jax.ShapeDtypeStruct((bsz, SB_HEADS, tq, SB_DH), F32)
    state_blk = (1, SB_HEADS, tq, SB_DH)
    q_spec = pl.BlockSpec((tq, SB_W), lambda b, s: (off_q + b, 0))
    new_spec = pl.BlockSpec((None, t * SB_HEADS, SB_DH), lambda b, s: (l, b, 0))
    near_spec = pl.BlockSpec((None, None, blk, SB_DH), lambda b, s: (l, b, n_tail, 0))
    state_spec = pl.BlockSpec(state_blk, lambda b, s: (b, 0, 0, 0))
    args = [bq, k8, v8, past_k, past_v]
    prev_args, prev_specs, aliases = _alias_prev(prev, len(args))
    hb, acc, carry = pl.pallas_call(
        functools.partial(_sb_head_kernel, tq=tq),
        out_shape=(out_shape, state_shape, state_shape),
        grid=(bsz, 2),
        in_specs=[q_spec, new_spec, new_spec, near_spec, near_spec] + prev_specs,
        out_specs=(q_spec, state_spec, state_spec),
        scratch_shapes=scratch,
        input_output_aliases=aliases,
        compiler_params=_params("parallel", "arbitrary"),
        name="stick_breaking_sample",
    )(*args, *prev_args)

    alive = jnp.max(carry, axis=(1, 2, 3)) > SB_DEAD_CARRY
    fetch = jnp.maximum(lax.cummax(jnp.where(alive, jnp.arange(bsz), -1), axis=0), 0)
    tail_spec = pl.BlockSpec((None, None, n_tail * blk, SB_DH),
                             lambda b, alive_ref, fetch_ref: (l, fetch_ref[b], 0, 0))
    state_in_spec = pl.BlockSpec(state_blk, lambda b, alive_ref, fetch_ref: (fetch_ref[b], 0, 0, 0))
    rows_spec = pl.BlockSpec((tq, SB_W), lambda b, alive_ref, fetch_ref: (off_q + b, 0))
    return pl.pallas_call(
        functools.partial(_sb_tail_kernel, n_tail=n_tail),
        out_shape=out_shape,
        grid_spec=pltpu.PrefetchScalarGridSpec(
            num_scalar_prefetch=2,
            grid=(bsz,),
            in_specs=[
                pl.BlockSpec((tq, SB_W), lambda b, alive_ref, fetch_ref: (off_q + fetch_ref[b], 0)),
                tail_spec, tail_spec, state_in_spec, state_in_spec,
                rows_spec,
            ],
            out_specs=rows_spec,
            scratch_shapes=scratch,
        ),
        input_output_aliases={7: 0},
        compiler_params=_params("arbitrary"),
        name="stick_breaking_tail",
    )(alive.astype(jnp.int32), fetch.astype(jnp.int32), bq, past_k, past_v, acc, carry, hb)


def _pool_kernel(u_ref, buf_ref, w_ref, s_ref, *rest, tt, pos0):
    o_ref, ext_ref = rest[-2:]
    i = pl.program_id(1)

    @pl.when(i == 0)
    def _():
        ext_ref[0:POOL_HALO, :] = buf_ref[0]

    @pl.when(i > 0)
    def _():
        ext_ref[0:POOL_HALO, :] = ext_ref[tt:tt + POOL_HALO, :]

    ext_ref[POOL_HALO:POOL_HALO + tt, :] = u_ref[...]
    pos = pos0 + i * tt + lax.broadcasted_iota(jnp.int32, (tt, 1), 0)
    for gi, w in enumerate(POOL_WINDOWS):
        cs = slice(gi * POOL_G, (gi + 1) * POOL_G)
        cur = ext_ref[POOL_HALO:POOL_HALO + tt, cs]
        wsum = cur
        for j in range(1, w):
            wsum = wsum + ext_ref[POOL_HALO - j:POOL_HALO - j + tt, cs]
        cnt = jnp.minimum(pos + 1, w).astype(F32)
        d = (wsum / cnt - cur).astype(BF16)
        y = _dot(d, w_ref[gi]) * s_ref[:, cs]
        o_ref[:, cs] = y.astype(BF16)


def _pool(u, buf, w_pool, scale, l, prev, *, bsz, t, row_off, pos0):
    tt = min(256, t)
    nt = t // tt
    off = row_off // tt
    row_spec = pl.BlockSpec((tt, POOL_W), lambda b, i: (off + b * nt + i, 0))
    args = [u, buf, w_pool, scale.reshape(1, POOL_W)]
    prev_args, prev_specs, aliases = _alias_prev(prev, len(args))
    return pl.pallas_call(
        functools.partial(_pool_kernel, tt=tt, pos0=pos0),
        out_shape=jax.ShapeDtypeStruct((u.shape[0], POOL_W), BF16),
        grid=(bsz, nt),
        in_specs=[
            row_spec,
            pl.BlockSpec((1, POOL_HALO, POOL_W), lambda b, i: (b, 0, 0)),
            pl.BlockSpec((None, len(POOL_WINDOWS), POOL_G, POOL_G), lambda b, i: (l, 0, 0, 0)),
            pl.BlockSpec((1, POOL_W), lambda b, i: (0, 0)),
        ] + prev_specs,
        out_specs=row_spec,
        scratch_shapes=[pltpu.VMEM((POOL_HALO + tt, POOL_W), F32)],
        input_output_aliases=aliases,
        compiler_params=_params("parallel", "arbitrary"),
        name="pool_mix",
    )(*args, *prev_args)


X_SCALE = X_DH ** -0.5


def _cross_kernel(q_ref, k_ref, v_ref, *rest):
    o_ref = rest[-1]
    for h in range(X_HEADS):
        cs = slice(h * X_DH, (h + 1) * X_DH)
        k = k_ref[0, :, cs].astype(BF16)
        v = v_ref[0, :, cs].astype(BF16)
        s = lax.dot_general(q_ref[:, cs], k, _NT, preferred_element_type=F32) * X_SCALE
        e = jnp.exp(s - jnp.max(s, axis=-1, keepdims=True))
        p = e / jnp.sum(e, axis=-1, keepdims=True)
        o_ref[:, cs] = _dot(p.astype(BF16), v).astype(BF16)


def _cross_attend(qx, mk, mv, l, prev, *, bsz, t, row_off):
    tq = min(256, t)
    nq = t // tq
    off = row_off // tq
    row_spec = pl.BlockSpec((tq, X_W), lambda b, i: (off + b * nq + i, 0))
    mem_spec = pl.BlockSpec((None, 1, mk.shape[2], X_W), lambda b, i: (l, b, 0, 0))
    args = [qx, mk, mv]
    prev_args, prev_specs, aliases = _alias_prev(prev, len(args))
    return pl.pallas_call(
        _cross_kernel,
        out_shape=jax.ShapeDtypeStruct((qx.shape[0], X_W), BF16),
        grid=(bsz, nq),
        in_specs=[row_spec, mem_spec, mem_spec] + prev_specs,
        out_specs=row_spec,
        input_output_aliases=aliases,
        compiler_params=_params("parallel", "arbitrary"),
        name="cross_attend",
    )(*args, *prev_args)


def _merge_kernel(x_ref, wg0_ref, wg1_ref, wg2_ref, bg0_ref, bg1_ref, bg2_ref,
                  ha_ref, hb_ref, hc_ref, wa_ref, wb_ref, wc_ref, side_in_ref,
                  o_ref, side_out_ref, *, side_slabs):
    step = pl.program_id(0) * pl.num_programs(1) + pl.program_id(1)

    @pl.when(step < side_slabs)
    def _():
        side_out_ref[...] = side_in_ref[...].astype(BF16)

    x = x_ref[...]
    out = None
    for wg_ref, bg_ref, h_ref, wp_ref in ((wg0_ref, bg0_ref, ha_ref, wa_ref),
                                          (wg1_ref, bg1_ref, hb_ref, wb_ref),
                                          (wg2_ref, bg2_ref, hc_ref, wc_ref)):
        gate = jax.nn.sigmoid(lax.dot_general(x, wg_ref[...], _NT, preferred_element_type=F32)
                              + bg_ref[...])
        term = gate * _dot(h_ref[...], wp_ref[...])
        out = term if out is None else out + term
    o_ref[...] = out.astype(BF16)


MERGE_SIDE_ROWS = 16


def _merge(xb, wt_b, bg, ha, hb, hc, w_pa, w_pb, w_pc, side_w, l, *, bm=512, bn=256):
    m = xb.shape[0]
    _, sr, sc = side_w.shape
    side_slabs = sr // MERGE_SIDE_ROWS
    assert sr % MERGE_SIDE_ROWS == 0 and side_slabs <= (m // bm) * (D_MODEL // bn)
    nb = D_MODEL // bn
    gb = B_G // bn
    row = lambda width: pl.BlockSpec((bm, width), lambda i, j: (i, 0))
    colw = lambda kdim: pl.BlockSpec((None, kdim, bn), lambda i, j: (0, 0, j))
    gate_w = [pl.BlockSpec((None, bn, D_MODEL), lambda i, j, g=g: (0, gb + g * nb + j, 0))
              for g in range(N_BRANCH)]
    gate_b = [pl.BlockSpec((1, bn), lambda i, j, g=g: (0, g * nb + j)) for g in range(N_BRANCH)]
    slab = lambda i, j: jnp.minimum(i * nb + j, side_slabs - 1)
    return pl.pallas_call(
        functools.partial(_merge_kernel, side_slabs=side_slabs),
        out_shape=(jax.ShapeDtypeStruct((m, D_MODEL), BF16),
                   jax.ShapeDtypeStruct((1, sr, sc), BF16)),
        grid=(m // bm, nb),
        in_specs=[row(D_MODEL)] + gate_w + gate_b
                 + [row(MLSTM_W), row(SB_W), row(POOL_W), colw(MLSTM_W), colw(SB_W), colw(POOL_W),
                    pl.BlockSpec((None, MERGE_SIDE_ROWS, sc), lambda i, j: (l, slab(i, j), 0))],
        out_specs=(pl.BlockSpec((bm, bn), lambda i, j: (i, j)),
                   pl.BlockSpec((None, MERGE_SIDE_ROWS, sc), lambda i, j: (0, slab(i, j), 0))),
        compiler_params=_params("arbitrary", "arbitrary"),
        name="gated_merge",
    )(xb, wt_b, wt_b, wt_b, bg, bg, bg, ha, hb, hc, w_pa, w_pb, w_pc, side_w)


def _trunk_layer(x, xb, W, l, groups, kv_prev, bufs, last):
    b_in = W["b_in"][l]
    proj_a = functools.partial(_matmul, xb, W["wt_in"], l, wt=True, bn=512)
    group_rows = tuple(g["bsz"] * g["t"] for g in groups)

    k_scale = jnp.concatenate([jnp.ones((MLSTM_QK_W,), F32),
                               jnp.full((MLSTM_QK_W,), MLSTM_DK ** -0.5, F32),
                               jnp.ones((MLSTM_W,), F32)])
    qkv, wt_b = proj_a(n=OFF_AO, bias=b_in[:OFF_AO], scale=k_scale, out_dtype=BF16,
                       side_cast=[(W["wt_in"], l, OFF_BQ, IN_W - OFF_BQ, 256)], name="proj_qkv")
    proj_b = functools.partial(_matmul, xb, wt_b, 0, wt=True)
    whole = lambda name, slab: (W[name], l, 0, W[name].shape[1], slab)
    ao, w_out = proj_a(n=MLSTM_W, col_off=OFF_AO, bias=b_in[OFF_AO:OFF_AI], act="sigmoid",
                       side_cast=[whole("w_out", 128)], name="proj_ogate")
    b_if = jnp.pad(b_in[OFF_AI:OFF_BQ], (0, GATE_PAD - 2 * MLSTM_HEADS))
    gif, w_pa, w_pb, w_pc = proj_a(
        n=GATE_PAD, col_off=OFF_AI, bias=b_if, bn=GATE_PAD,
        side_cast=[whole("w_pa", 256), whole("w_pb", 128), whole("w_pc", 128)], name="proj_if")
    sec = lambda i: b_in[OFF_BQ + i * SB_W:OFF_BQ + (i + 1) * SB_W]
    bq, wq_x, wo_x = proj_b(n=SB_W, bias=sec(0), out_dtype=BF16,
                            side_cast=[whole("wq_x", 512), whole("wo_x", 128)], name="proj_sb_q")
    k8 = _proj_heads(xb, wt_b, l, sec(1), B_BK, group_rows, kv_prev[0], name="proj_sb_k")
    v8 = _proj_heads(xb, wt_b, l, sec(2), B_BV, group_rows, kv_prev[1], name="proj_sb_v")
    u = proj_b(n=POOL_W, col_off=B_CU, bias=sec(3), name="proj_pool_u")

    rows = xb.shape[0]
    if bufs is None:
        bufs = tuple(jnp.zeros((rows, w), BF16) for w in (MLSTM_W, SB_W, POOL_W, X_W))
    ha, hb, hc, ox = bufs
    states = []
    for gi, g in enumerate(groups):
        dims = dict(bsz=g["bsz"], t=g["t"], row_off=g["row_off"])
        ha, c_new, n_new, m_new = _mlstm(qkv, ao, gif, g["c"], g["n"], g["m"],
                                         W["mlstm_norm_g"][l], l, ha, **dims)
        hb = _stick_breaking(bq, k8[gi], v8[gi], g["sb_k"], g["sb_v"], l, hb, **dims)
        hc = _pool(u, g["pool_buf"][l], W["w_pool"], W["pool_scale"][l], l, hc,
                   pos0=g["pos0"], **dims)
        states.append((c_new, n_new, m_new))

    merged, w_up = _merge(xb, wt_b, b_in[OFF_G:].reshape(1, -1), ha, hb, hc,
                          w_pa, w_pb, w_pc, W["w_up"], l)
    y1 = _matmul(merged, w_out, 0, n=D_MODEL, res=x, name="w_out")
    ln1 = (W["ln1_g"][l], W["ln1_b"][l])
    mu1, rs1, xb = _layer_norm(y1, *ln1, emit_f32=False)

    qx = _matmul(xb, wq_x, 0, n=X_W, out_dtype=BF16, name="wq_x")
    for g in groups:
        ox = _cross_attend(qx, g["mk"], g["mv"], l, ox, bsz=g["bsz"], t=g["t"],
                           row_off=g["row_off"])
    y2 = _matmul(ox, wo_x, 0, n=D_MODEL, res=y1, res_ln=(mu1, rs1, *ln1), name="wo_x")
    ln2 = (W["ln2_g"][l], W["ln2_b"][l])
    mu2, rs2, xb = _layer_norm(y2, *ln2, emit_f32=False)

    hid, w_down = _matmul(xb, w_up, 0, n=D_FF, act="relu2", out_dtype=BF16,
                          side_cast=[whole("w_down", 128)], name="w_up")
    y = _matmul(hid, w_down, 0, n=D_MODEL, bk=D_MODEL, name="w_down")
    if last:
        x = tuple(_layer_norm(y2, W["ln3_g"][l], W["ln3_b"][l], res=y, x_ln=(mu2, rs2, *ln2),
                              in_off=g["row_off"], n_rows=g["bsz"] * g["t"], emit_bf16=False)
                  for g in groups)
        xb = None
    else:
        x, xb = _layer_norm(y2, W["ln3_g"][l], W["ln3_b"][l], res=y, x_ln=(mu2, rs2, *ln2))
    return x, xb, (k8, v8), u, states, (ha, hb, hc, ox)


def kernel(x_prompt, x_sample, cache_sb_k, cache_sb_v, state_mlstm_c, state_mlstm_n, state_mlstm_m, state_pool, cache_mem_k, cache_mem_v, mem_prompt, ln_in_g, ln_in_b, w_in, b_in, mlstm_norm_g, w_pool, pool_scale, w_pa, w_pb, w_pc, w_out, ln1_g, ln1_b, wq_x, wk_x, wv_x, wo_x, ln2_g, ln2_b, w_up, w_down, ln3_g, ln3_b):
    bf = lambda a: a.astype(BF16)
    wt_in = jnp.swapaxes(w_in, 1, 2)
    W = dict(wt_in=wt_in,
             b_in=b_in, mlstm_norm_g=mlstm_norm_g, w_pool=bf(w_pool),
             pool_scale=pool_scale, w_pa=w_pa, w_pb=w_pb, w_pc=w_pc, w_out=w_out,
             ln1_g=ln1_g, ln1_b=ln1_b, wq_x=wq_x, wo_x=wo_x, ln2_g=ln2_g, ln2_b=ln2_b,
             w_up=w_up, w_down=w_down, ln3_g=ln3_g, ln3_b=ln3_b)
    bp, tp, _ = x_prompt.shape
    bs, ts, _ = x_sample.shape
    rows_p, rows_s = bp * tp, bs * ts
    rows = rows_p + rows_s
    past = cache_sb_k.shape[2]
    mem_len = mem_prompt.shape[1]

    mem_b = mem_prompt.reshape(bp * mem_len, D_MODEL).astype(BF16)
    wk_b, wv_b = bf(wk_x), bf(wv_x)
    mem4 = lambda a, b: a.reshape(DEPTH, b, mem_len, X_W)
    mk_p = mem4(jnp.stack([_matmul(mem_b, wk_b, l, n=X_W, name="mem_k") for l in range(DEPTH)]), bp)
    mv_p = mem4(jnp.stack([_matmul(mem_b, wv_b, l, n=X_W, name="mem_v") for l in range(DEPTH)]), bp)

    lane_m = lambda m: jnp.broadcast_to(m[..., None], m.shape + (LANES,))
    halo = lambda b: jnp.pad(b, ((0, 0), (0, 0), (POOL_HALO - POOL_BUF, 0), (0, 0)))
    heads_rows = lambda c: c.reshape(DEPTH, bs, past * SB_HEADS, SB_DH)
    groups = [
        dict(bsz=bp, t=tp, row_off=0, pos0=0,
             c=jnp.zeros((DEPTH, bp, MLSTM_HEADS, MLSTM_DK, MLSTM_DV), F32),
             n=jnp.zeros((DEPTH, bp, MLSTM_HEADS, MLSTM_DK), F32),
             m=jnp.zeros((DEPTH, bp, MLSTM_HEADS, LANES), F32),
             sb_k=None, sb_v=None,
             pool_buf=jnp.zeros((DEPTH, bp, POOL_HALO, POOL_W), F32),
             mk=mk_p, mv=mv_p),
        dict(bsz=bs, t=ts, row_off=rows_p, pos0=past,
             c=state_mlstm_c, n=state_mlstm_n, m=lane_m(state_mlstm_m),
             sb_k=heads_rows(cache_sb_k), sb_v=heads_rows(cache_sb_v),
             pool_buf=halo(state_pool),
             mk=mem4(cache_mem_k, bs), mv=mem4(cache_mem_v, bs)),
    ]

    xs = None
    for g, xin in zip(groups, (x_prompt, x_sample)):
        xs = _layer_norm(xin.reshape(g["bsz"] * g["t"], D_MODEL), ln_in_g, ln_in_b,
                         out_off=g["row_off"], out_rows=rows, prev=xs)
    x, xb = xs

    kv = tuple(tuple(jnp.zeros((DEPTH, g["bsz"] * g["t"] * SB_HEADS, SB_DH), F32) for g in groups)
               for _ in range(2))
    per_layer = []
    bufs = None
    for l in range(DEPTH):
        x, xb, kv, u, states, bufs = _trunk_layer(x, xb, W, l, groups, kv, bufs,
                                                  last=l == DEPTH - 1)
        per_layer.append((u, states))

    def group_out(gi, g):
        bsz, t, r0 = g["bsz"], g["t"], g["row_off"]
        sbk = kv[0][gi].reshape(DEPTH, bsz, t, SB_HEADS, SB_DH)
        sbv = kv[1][gi].reshape(DEPTH, bsz, t, SB_HEADS, SB_DH)
        c = jnp.stack([st[gi][0] for _, st in per_layer])
        n = jnp.stack([st[gi][1] for _, st in per_layer])
        m = jnp.stack([st[gi][2][..., 0] for _, st in per_layer])
        pool = jnp.stack([u[r0:r0 + bsz * t].reshape(bsz, t, POOL_W)[:, t - POOL_BUF:, :]
                          for u, _ in per_layer])
        return sbk, sbv, c, n, m, pool

    out_p = group_out(0, groups[0])
    out_s = group_out(1, groups[1])
    y_prompt = x[0].reshape(bp, tp, D_MODEL)
    y_sample = x[1].reshape(bs, ts, D_MODEL)
    mem5 = lambda a: a.reshape(DEPTH, bp, mem_len, X_HEADS, X_DH)
    return (y_prompt, y_sample) + out_p + (mem5(mk_p), mem5(mv_p)) + out_s
```

```python
import functools

import jax
import jax.numpy as jnp
from jax import lax
from jax.experimental import pallas as pl
from jax.experimental.pallas import tpu as pltpu

F32 = jnp.float32
BF16 = jnp.bfloat16

D_MODEL = 4096
DEPTH = 2
MLSTM_CHUNK = 256
MLSTM_HEADS = 8
MLSTM_W = D_MODEL // 2
MLSTM_DV = MLSTM_W // MLSTM_HEADS
MLSTM_DK = MLSTM_DV // 2
MLSTM_QK_W = MLSTM_HEADS * MLSTM_DK
SB_HEADS = 8
SB_W = D_MODEL // 4
SB_DH = SB_W // SB_HEADS
POOL_WINDOWS = (2, 4, 8, 16)
POOL_W = D_MODEL // 4
POOL_G = POOL_W // len(POOL_WINDOWS)
POOL_BUF = max(POOL_WINDOWS) - 1
POOL_HALO = POOL_BUF + 1
X_HEADS = 4
X_W = D_MODEL // 4
X_DH = X_W // X_HEADS
D_FF = 4 * D_MODEL
N_BRANCH = 3
ALPHA = (2 * DEPTH) ** 0.25
LN_EPS = 1e-5
HEAD_NORM_EPS = 1e-6

OFF_AO = 2 * MLSTM_QK_W + MLSTM_W
OFF_AI = OFF_AO + MLSTM_W
OFF_BQ = OFF_AI + 2 * MLSTM_HEADS
OFF_G = OFF_BQ + 3 * SB_W + POOL_W
IN_W = OFF_G + N_BRANCH * D_MODEL

V7X_VMEM_LIMIT_BYTES = 56 * 1024 * 1024
LANES = 128
SUBLANES = 8
SB_TQ = 256
GATE_PAD = LANES

B_BK = SB_W
B_BV = 2 * SB_W
B_CU = 3 * SB_W
B_G = 3 * SB_W + POOL_W

_NT = (((1,), (1,)), ((), ()))
_TN = (((0,), (0,)), ((), ()))


def _params(*sem):
    return pltpu.CompilerParams(dimension_semantics=sem,
                                vmem_limit_bytes=V7X_VMEM_LIMIT_BYTES)


def _dot(a, b):
    return jnp.dot(a, b, preferred_element_type=F32)


def _split(x, terms):
    out = []
    for _ in range(terms - 1):
        h = x.astype(BF16)
        out.append(h)
        x = x - h.astype(F32)
    out.append(x.astype(BF16))
    return out


def _apply_ln(x, mu_ref, rs_ref, g_ref, b_ref):
    return (x - mu_ref[:, 0:1]) * rs_ref[:, 0:1] * g_ref[...] + b_ref[...]


def _alias_prev(prev, n_in):
    if prev is None:
        return [], [], {}
    prev = list(prev) if isinstance(prev, (tuple, list)) else [prev]
    specs = [pl.BlockSpec(memory_space=pl.ANY)] * len(prev)
    return prev, specs, {n_in + i: i for i in range(len(prev))}


def _mm_kernel(*refs, act, has_bias, has_scale, has_res, res_ln, nk, side_slabs, wt):
    it = iter(refs)
    x_ref, w_ref = next(it), next(it)
    b_ref = next(it) if has_bias else None
    s_ref = next(it) if has_scale else None
    r_ref = next(it) if has_res else None
    ln_refs = [next(it) for _ in range(4)] if res_ln else None
    side_in_refs = [next(it) for _ in side_slabs]
    o_ref = next(it)
    side_out_refs = [next(it) for _ in side_slabs]
    acc_ref = next(it) if nk > 1 else None

    if side_slabs:
        step = pl.program_id(0) * pl.num_programs(1) + pl.program_id(1)
        for n_slabs, in_ref, out_ref in zip(side_slabs, side_in_refs, side_out_refs):
            @pl.when(step < n_slabs)
            def _(in_ref=in_ref, out_ref=out_ref):
                out_ref[...] = in_ref[0].astype(BF16)

    def epilogue(y):
        if has_bias:
            y = y + b_ref[...]
        if has_scale:
            y = y * s_ref[...]
        if act == "sigmoid":
            y = jax.nn.sigmoid(y)
        elif act == "relu2":
            y = jnp.square(jnp.maximum(y, 0.0))
        if has_res:
            r = r_ref[...]
            if res_ln:
                r = _apply_ln(r, *ln_refs)
            y = ALPHA * r + y
        o_ref[...] = y.astype(o_ref.dtype)

    w = w_ref[...].astype(BF16)
    part = lax.dot_general(x_ref[...], w, _NT if wt else (((1,), (0,)), ((), ())),
                           preferred_element_type=F32)
    if nk == 1:
        epilogue(part)
    else:
        k = pl.program_id(2)

        @pl.when(k == 0)
        def _():
            acc_ref[...] = part

        @pl.when(k > 0)
        def _():
            acc_ref[...] += part

        @pl.when(k == nk - 1)
        def _():
            epilogue(acc_ref[...])


def _matmul(x, w, l, *, n, col_off=0, bias=None, scale=None, res=None, res_ln=None, act="none",
            out_dtype=F32, bm=1024, bn=1024, bk=None, side_cast=(), wt=False, name="matmul"):
    m, kdim = x.shape
    bm, bn = min(bm, m), min(bn, n)
    bk = kdim if bk is None else bk
    assert m % bm == 0 and n % bn == 0 and kdim % bk == 0 and col_off % bn == 0
    nk = kdim // bk
    cb = col_off // bn
    nj = n // bn
    w_spec = (pl.BlockSpec((None, bn, bk), lambda i, j, k: (l, cb + j, k)) if wt
              else pl.BlockSpec((None, bk, bn), lambda i, j, k: (l, k, cb + j)))
    in_specs = [pl.BlockSpec((bm, bk), lambda i, j, k: (i, k)), w_spec]
    args = [x, w]
    for v in (bias, scale):
        if v is not None:
            in_specs.append(pl.BlockSpec((1, bn), lambda i, j, k: (0, j)))
            args.append(v.reshape(1, n).astype(F32))
    if res is not None:
        in_specs.append(pl.BlockSpec((bm, bn), lambda i, j, k: (i, j)))
        args.append(res)
    if res_ln is not None:
        mu, rs, ln_g, ln_b = res_ln
        in_specs += [pl.BlockSpec((bm, LANES), lambda i, j, k: (i, 0))] * 2
        in_specs += [pl.BlockSpec((1, bn), lambda i, j, k: (0, j))] * 2
        args += [mu, rs, ln_g.reshape(1, n), ln_b.reshape(1, n)]
    out_shape = [jax.ShapeDtypeStruct((m, n), out_dtype)]
    out_specs = [pl.BlockSpec((bm, bn), lambda i, j, k: (i, j))]
    side_slabs = []
    for side_w, side_l, row_off, sr, slab_rows in side_cast:
        sc = side_w.shape[2]
        n_slabs = sr // slab_rows
        assert nk == 1 and sr % slab_rows == 0 and n_slabs <= (m // bm) * nj
        slab = lambda i, j, k, n_slabs=n_slabs: jnp.minimum(i * nj + j, n_slabs - 1)
        in_specs.append(pl.BlockSpec(
            (pl.Element(1), pl.Element(slab_rows), pl.Element(sc)),
            lambda i, j, k, slab=slab, side_l=side_l, row_off=row_off, slab_rows=slab_rows: (
                side_l, pl.multiple_of(row_off + slab(i, j, k) * slab_rows, SUBLANES), 0)))
        args.append(side_w)
        out_shape.append(jax.ShapeDtypeStruct((1, sr, sc), BF16))
        out_specs.append(pl.BlockSpec((None, slab_rows, sc),
                                      lambda i, j, k, slab=slab: (0, slab(i, j, k), 0)))
        side_slabs.append(n_slabs)
    kern = functools.partial(_mm_kernel, act=act, has_bias=bias is not None,
                             has_scale=scale is not None, has_res=res is not None,
                             res_ln=res_ln is not None, nk=nk,
                             side_slabs=tuple(side_slabs), wt=wt)
    out = pl.pallas_call(
        kern,
        out_shape=tuple(out_shape),
        grid=(m // bm, nj, nk),
        in_specs=in_specs,
        out_specs=tuple(out_specs),
        scratch_shapes=[pltpu.VMEM((bm, bn), F32)] if nk > 1 else [],
        compiler_params=(_params("arbitrary", "arbitrary", "arbitrary") if side_slabs
                         else _params("parallel", "parallel", "arbitrary")),
        name=name,
    )(*args)
    return out if side_slabs else out[0]


def _proj_heads_kernel(x_ref, w_ref, b_ref, *rest, n_first):
    first_ref, second_ref = rest[-2:]
    y = lax.dot_general(x_ref[...], w_ref[...], _NT, preferred_element_type=F32) + b_ref[...]
    rows = y.shape[0]

    def store(o_ref):
        for h in range(SUBLANES):
            o_ref[pl.ds(h, rows, stride=SUBLANES), :] = y[:, h * LANES:(h + 1) * LANES]

    i = pl.program_id(0)

    @pl.when(i < n_first)
    def _():
        store(first_ref)

    @pl.when(i >= n_first)
    def _():
        store(second_ref)


def _proj_heads(x, w, l, bias, col_off, group_rows, prev, *, bm=1024, name="proj_heads"):
    m, kdim = x.shape
    n = SUBLANES * LANES
    rows_a, rows_b = group_rows
    assert rows_a % bm == 0 and rows_b % bm == 0 and rows_a + rows_b == m and col_off % n == 0
    n_first = rows_a // bm
    cb = col_off // n
    args = [x, w, bias.reshape(1, n)]
    prev_args, prev_specs, aliases = _alias_prev(prev, len(args))
    blk = (None, bm * SUBLANES, LANES)
    return pl.pallas_call(
        functools.partial(_proj_heads_kernel, n_first=n_first),
        out_shape=tuple(jax.ShapeDtypeStruct((DEPTH, r * SUBLANES, LANES), F32) for r in group_rows),
        grid=(m // bm,),
        in_specs=[pl.BlockSpec((bm, kdim), lambda i: (i, 0)),
                  pl.BlockSpec((None, n, kdim), lambda i: (0, cb, 0)),
                  pl.BlockSpec((1, n), lambda i: (0, 0))] + prev_specs,
        out_specs=(pl.BlockSpec(blk, lambda i: (l, jnp.minimum(i, n_first - 1), 0)),
                   pl.BlockSpec(blk, lambda i: (l, jnp.maximum(i - n_first, 0), 0))),
        input_output_aliases=aliases,
        compiler_params=_params("arbitrary"),
        name=name,
    )(*args, *prev_args)


def _ln_kernel(*refs, x_ln, has_res, n_prev, emit_f32, emit_bf16, n_blocks):
    it = iter(refs)
    x_ref = next(it)
    x_ln_refs = [next(it) for _ in range(4)] if x_ln else None
    r_ref = next(it) if has_res else None
    g_ref, b_ref = next(it), next(it)
    for _ in range(n_prev):
        next(it)
    x = x_ref[...]
    if x_ln:
        x = _apply_ln(x, *x_ln_refs)
    if has_res:
        x = ALPHA * x + r_ref[...]
    mu = jnp.mean(x, axis=-1, keepdims=True)
    xc = x - mu
    var = jnp.mean(xc * xc, axis=-1, keepdims=True)
    rs = lax.rsqrt(var + LN_EPS)
    y = xc * rs * g_ref[...] + b_ref[...]
    if n_blocks is not None:
        y = jnp.where(pl.program_id(0) < n_blocks, y, 0.0)
    if emit_f32:
        next(it)[...] = y
    else:
        mu_ref, rs_ref = next(it), next(it)
        mu_ref[...] = jnp.broadcast_to(mu, mu_ref.shape)
        rs_ref[...] = jnp.broadcast_to(rs, rs_ref.shape)
    if emit_bf16:
        next(it)[...] = y.astype(BF16)


def _layer_norm(x, g, b, res=None, *, x_ln=None, in_off=0, n_rows=None, out_off=0, out_rows=None,
                prev=None, emit_f32=True, emit_bf16=True, rows=256):
    d = x.shape[1]
    n_rows = x.shape[0] - in_off if n_rows is None else n_rows
    out_rows = n_rows if out_rows is None else out_rows
    ib, ob = in_off // rows, out_off // rows
    n_blocks = n_rows // rows
    fill = prev is None and out_rows > n_rows
    assert not fill or out_off == 0
    in_spec = pl.BlockSpec((rows, d), lambda i: (ib + jnp.minimum(i, n_blocks - 1), 0))
    stat_in_spec = pl.BlockSpec((rows, LANES), lambda i: (ib + jnp.minimum(i, n_blocks - 1), 0))
    out_spec = pl.BlockSpec((rows, d), lambda i: (ob + i, 0))
    stat_spec = pl.BlockSpec((rows, LANES), lambda i: (ob + i, 0))
    vec_spec = pl.BlockSpec((1, d), lambda i: (0, 0))
    args, in_specs = [x], [in_spec]
    if x_ln is not None:
        args += [x_ln[0], x_ln[1], x_ln[2].reshape(1, d), x_ln[3].reshape(1, d)]
        in_specs += [stat_in_spec, stat_in_spec, vec_spec, vec_spec]
    if res is not None:
        args.append(res)
        in_specs.append(in_spec)
    args += [g.reshape(1, d), b.reshape(1, d)]
    in_specs += [vec_spec, vec_spec]
    prev_args, prev_specs, aliases = _alias_prev(prev, len(args))
    if emit_f32:
        out_shape, out_specs = [jax.ShapeDtypeStruct((out_rows, d), F32)], [out_spec]
    else:
        out_shape = [jax.ShapeDtypeStruct((out_rows, LANES), F32)] * 2
        out_specs = [stat_spec, stat_spec]
    if emit_bf16:
        out_shape.append(jax.ShapeDtypeStruct((out_rows, d), BF16))
        out_specs.append(out_spec)
    out = pl.pallas_call(
        functools.partial(_ln_kernel, x_ln=x_ln is not None, has_res=res is not None,
                          n_prev=len(prev_args), emit_f32=emit_f32, emit_bf16=emit_bf16,
                          n_blocks=n_blocks if fill else None),
        out_shape=tuple(out_shape),
        grid=(out_rows // rows if fill else n_blocks,),
        in_specs=in_specs + prev_specs,
        out_specs=tuple(out_specs),
        input_output_aliases=aliases,
        compiler_params=_params("parallel"),
        name="layer_norm",
    )(*args, *prev_args)
    return out if len(out) > 1 else out[0]


def _mlstm_kernel(q_ref, k_ref, v_ref, ao_ref, gif_ref, c0_ref, n0_ref, m0_ref, g_ref, *rest):
    ha_ref, c_ref, n_ref, m_ref = rest[-4:]
    L = q_ref.shape[0]

    @pl.when(pl.program_id(1) == 0)
    def _():
        c_ref[...] = c0_ref[...]
        n_ref[...] = n0_ref[...]
        m_ref[...] = m0_ref[...]

    gif = gif_ref[...]
    logf = -(jnp.maximum(-gif, 0.0) + jnp.log1p(jnp.exp(-jnp.abs(gif))))
    row = lax.broadcasted_iota(jnp.int32, (L, L), 0)
    col = lax.broadcasted_iota(jnp.int32, (L, L), 1)
    causal = col <= row
    tril = jnp.where(causal, 1.0, 0.0).astype(BF16)
    b_all = sum(_dot(tril, t) for t in _split(logf, 3))
    sel_r = lax.broadcasted_iota(jnp.int32, (2 * MLSTM_HEADS, GATE_PAD), 0)
    sel_c = lax.broadcasted_iota(jnp.int32, (2 * MLSTM_HEADS, GATE_PAD), 1)
    sel = jnp.where(sel_r == sel_c, 1.0, 0.0).astype(BF16)

    def rows_of(x):
        return sum(lax.dot_general(sel, t, _NT, preferred_element_type=F32) for t in _split(x, 3))

    i_rows = rows_of(gif)
    b_rows = rows_of(b_all)

    for h in range(MLSTM_HEADS):
        qs = slice(h * MLSTM_DK, (h + 1) * MLSTM_DK)
        vs = slice(h * MLSTM_DV, (h + 1) * MLSTM_DV)
        q = q_ref[:, qs]
        k = k_ref[:, qs]
        v = v_ref[:, vs]
        i_col = gif[:, h:h + 1]
        b_col = b_all[:, MLSTM_HEADS + h:MLSTM_HEADS + h + 1]
        i_row = i_rows[h:h + 1, :]
        b_row = b_rows[MLSTM_HEADS + h:MLSTM_HEADS + h + 1, :]
        c_old = c_ref[0, h]
        n_old = n_ref[0, h:h + 1, :]
        m_old = m_ref[0, h:h + 1, 0:1]

        dmat = jnp.where(causal, b_col - b_row + i_row, -jnp.inf)
        m_inter = b_col + m_old
        m_t = jnp.maximum(m_inter, jnp.max(dmat, axis=-1, keepdims=True))
        s = lax.dot_general(q, k, _NT, preferred_element_type=F32) * jnp.exp(dmat - m_t)
        w_inter = jnp.exp(m_inter - m_t)
        kf = k.astype(F32)
        num = _dot(s.astype(BF16), v) + w_inter * _dot(q, c_old.astype(BF16))
        qn = jnp.sum(q.astype(F32) * n_old, axis=-1, keepdims=True)
        den = jnp.sum(s, axis=-1, keepdims=True) + w_inter * qn
        hh = num * (1.0 / jnp.maximum(jnp.abs(den), jnp.exp(-m_t)))

        m_new = m_t[L - 1:L, :]
        b_last = b_col[L - 1:L, :]
        g_col = jnp.exp(b_last - b_col + i_col - m_new)
        decay = jnp.exp(b_last + m_old - m_new)
        kg = kf * g_col
        c_ref[0, h] = decay * c_old + lax.dot_general(kg.astype(BF16), v, _TN,
                                                      preferred_element_type=F32)
        n_ref[0, h:h + 1, :] = decay * n_old + jnp.sum(kg, axis=0, keepdims=True)
        m_ref[0, h:h + 1, :] = jnp.broadcast_to(m_new, (1, LANES))

        mu = jnp.mean(hh, axis=-1, keepdims=True)
        hc = hh - mu
        var = jnp.mean(hc * hc, axis=-1, keepdims=True)
        hn = hc * lax.rsqrt(var + HEAD_NORM_EPS) * g_ref[:, vs]
        ha_ref[:, vs] = (ao_ref[:, vs] * hn).astype(BF16)


def _mlstm(qkv, ao, gif, c0, n0, m0, norm_g, l, prev, *, bsz, t, row_off):
    chunk = min(MLSTM_CHUNK, t)
    nc = t // chunk
    off = row_off // chunk
    rows = lambda b, c: off + b * nc + c
    state4 = pl.BlockSpec((1, MLSTM_HEADS, MLSTM_DK, MLSTM_DV), lambda b, c: (b, 0, 0, 0))
    state3 = pl.BlockSpec((1, MLSTM_HEADS, LANES), lambda b, c: (b, 0, 0))
    init4 = pl.BlockSpec((None, 1, MLSTM_HEADS, MLSTM_DK, MLSTM_DV), lambda b, c: (l, b, 0, 0, 0))
    init3 = pl.BlockSpec((None, 1, MLSTM_HEADS, LANES), lambda b, c: (l, b, 0, 0))
    args = [qkv, qkv, qkv, ao, gif, c0, n0, m0, norm_g.reshape(1, MLSTM_W)]
    prev_args, prev_specs, aliases = _alias_prev(prev, len(args))
    return pl.pallas_call(
        _mlstm_kernel,
        out_shape=(jax.ShapeDtypeStruct((qkv.shape[0], MLSTM_W), BF16),
                   jax.ShapeDtypeStruct(c0.shape[1:], F32),
                   jax.ShapeDtypeStruct(n0.shape[1:], F32),
                   jax.ShapeDtypeStruct(m0.shape[1:], F32)),
        grid=(bsz, nc),
        in_specs=[
            pl.BlockSpec((chunk, MLSTM_QK_W), lambda b, c: (rows(b, c), 0)),
            pl.BlockSpec((chunk, MLSTM_QK_W), lambda b, c: (rows(b, c), 1)),
            pl.BlockSpec((chunk, MLSTM_W), lambda b, c: (rows(b, c), 1)),
            pl.BlockSpec((chunk, MLSTM_W), lambda b, c: (rows(b, c), 0)),
            pl.BlockSpec((chunk, GATE_PAD), lambda b, c: (rows(b, c), 0)),
            init4, init3, init3,
            pl.BlockSpec((1, MLSTM_W), lambda b, c: (0, 0)),
        ] + prev_specs,
        out_specs=(pl.BlockSpec((chunk, MLSTM_W), lambda b, c: (rows(b, c), 0)),
                   state4, state3, state3),
        input_output_aliases=aliases,
        compiler_params=_params("parallel", "arbitrary"),
        name="mlstm",
    )(*args, *prev_args)


SB_SCALE = SB_DH ** -0.5
SB_DEAD_CARRY = -110.0


def _lower_tri(n):
    return jnp.where(lax.broadcasted_iota(jnp.int32, (n, n), 0)
                     >= lax.broadcasted_iota(jnp.int32, (n, n), 1), 1.0, 0.0).astype(BF16)


def _sb_group(q_ref, load_k, load_v, heads, acc_ref, carry_ref, lower, *, diag):
    tq = q_ref.shape[0]
    tk = lower.shape[0]
    if diag:
        mask = (lax.broadcasted_iota(jnp.int32, (tq, tk), 1)
                < lax.broadcasted_iota(jnp.int32, (tq, tk), 0))
    zs, lks = [], []
    for h in heads:
        z = lax.dot_general(q_ref[:, h * SB_DH:(h + 1) * SB_DH], load_k(h), _NT,
                            preferred_element_type=F32) * SB_SCALE
        log_keep = -(jnp.maximum(z, 0.0) + jnp.log(1.0 + jnp.exp(-jnp.abs(z))))
        if diag:
            log_keep = jnp.where(mask, log_keep, 0.0)
        zs.append(z)
        lks.append(log_keep)
    stacked = jnp.concatenate(lks, axis=0)
    rc_all = sum(_dot(t, lower) for t in _split(stacked, 2))
    for i, h in enumerate(heads):
        rc = rc_all[i * tq:(i + 1) * tq]
        carry = carry_ref[h]
        attn = jnp.exp(zs[i] + rc + carry)
        if diag:
            attn = jnp.where(mask, attn, 0.0)
        acc_ref[h] += _dot(attn.astype(BF16), load_v(h))
        carry_ref[h] = carry + rc[:, 0:1]


def _sb_block(q_ref, k_ref, v_ref, base, tk, acc_ref, carry_ref, lower, *, group, diag):
    def loader(ref):
        return lambda h: ref[pl.ds(base + h, tk, stride=SB_HEADS), :].astype(BF16)
    for g in range(0, SB_HEADS, group):
        _sb_group(q_ref, loader(k_ref), loader(v_ref), range(g, g + group),
                  acc_ref, carry_ref, lower, diag=diag)


def _sb_alive(carry_ref):
    return jnp.max(carry_ref[...]) > SB_DEAD_CARRY


def _sb_store(o_ref, acc_ref):
    for h in range(SB_HEADS):
        o_ref[:, h * SB_DH:(h + 1) * SB_DH] = acc_ref[h].astype(BF16)


def _sb_prompt_kernel(q_ref, k_ref, v_ref, *rest, tq, group):
    o_ref, acc_ref, carry_ref = rest[-3:]
    qi = pl.program_id(1)
    acc_ref[...] = jnp.zeros_like(acc_ref)
    carry_ref[...] = jnp.zeros_like(carry_ref)
    lower = _lower_tri(tq)
    blk_rows = tq * SB_HEADS

    _sb_block(q_ref, k_ref, v_ref, pl.multiple_of(qi * blk_rows, blk_rows), tq,
              acc_ref, carry_ref, lower, group=group, diag=True)

    def more(state):
        j, alive = state
        return jnp.logical_and(j < qi, alive)

    def body(state):
        j, _ = state
        base = pl.multiple_of((qi - 1 - j) * blk_rows, blk_rows)
        _sb_block(q_ref, k_ref, v_ref, base, tq, acc_ref, carry_ref, lower,
                  group=group, diag=False)
        return j + 1, _sb_alive(carry_ref)

    lax.while_loop(more, body, (jnp.int32(0), _sb_alive(carry_ref)))
    _sb_store(o_ref, acc_ref)


def _sb_head_kernel(q_ref, k_ref, v_ref, kp_ref, vp_ref, *rest, tq):
    o_ref, acc_out_ref, carry_out_ref, acc_ref, carry_ref = rest[-5:]
    s = pl.program_id(1)

    @pl.when(s == 0)
    def _():
        acc_ref[...] = jnp.zeros_like(acc_ref)
        carry_ref[...] = jnp.zeros_like(carry_ref)
        _sb_block(q_ref, k_ref, v_ref, 0, tq, acc_ref, carry_ref, _lower_tri(tq),
                  group=SB_HEADS, diag=True)

    @pl.when(s == 1)
    def _():
        @pl.when(_sb_alive(carry_ref))
        def _():
            _sb_block(q_ref, kp_ref, vp_ref, 0, SB_TQ, acc_ref, carry_ref, _lower_tri(SB_TQ),
                      group=SB_HEADS, diag=False)

        _sb_store(o_ref, acc_ref)
        acc_out_ref[0] = acc_ref[...]
        carry_out_ref[0] = jnp.broadcast_to(carry_ref[...], carry_out_ref.shape[1:])


def _sb_tail_kernel(alive_ref, fetch_ref, q_ref, kp_ref, vp_ref, acc_in_ref, carry_in_ref, hb_ref,
                    o_ref, acc_ref, carry_ref, *, n_tail):
    del fetch_ref
    b = pl.program_id(0)

    @pl.when(alive_ref[b] == 0)
    def _():
        o_ref[...] = hb_ref[...]

    @pl.when(alive_ref[b] != 0)
    def _():
        acc_ref[...] = acc_in_ref[0]
        carry_ref[...] = carry_in_ref[0][:, :, 0:1]
        lower = _lower_tri(SB_TQ)
        blk_rows = SB_TQ * SB_HEADS

        def body(j, c):
            @pl.when(_sb_alive(carry_ref))
            def _():
                base = pl.multiple_of((n_tail - 1 - j) * blk_rows, blk_rows)
                _sb_block(q_ref, kp_ref, vp_ref, base, SB_TQ, acc_ref, carry_ref, lower,
                          group=SB_HEADS, diag=False)
            return c

        lax.fori_loop(0, n_tail, body, 0)
        _sb_store(o_ref, acc_ref)


def _stick_breaking(bq, k8, v8, past_k, past_v, l, prev, *, bsz, t, row_off):
    tq = min(SB_TQ, t)
    nq = t // tq
    off_q = row_off // tq
    scratch = [pltpu.VMEM((SB_HEADS, tq, SB_DH), F32), pltpu.VMEM((SB_HEADS, tq, 1), F32)]
    out_shape = jax.ShapeDtypeStruct((bq.shape[0], SB_W), BF16)
    q_spec = pl.BlockSpec((tq, SB_W), lambda b, i: (off_q + b * nq + i, 0))
    if past_k is None:
        seq_spec = pl.BlockSpec((None, t * SB_HEADS, SB_DH), lambda b, i: (l, b, 0))
        args = [bq, k8, v8]
        prev_args, prev_specs, aliases = _alias_prev(prev, len(args))
        return pl.pallas_call(
            functools.partial(_sb_prompt_kernel, tq=tq, group=SB_HEADS // 2),
            out_shape=out_shape,
            grid=(bsz, nq),
            in_specs=[q_spec, seq_spec, seq_spec] + prev_specs,
            out_specs=q_spec,
            scratch_shapes=scratch,
            input_output_aliases=aliases,
            compiler_params=_params("parallel", "arbitrary"),
            name="stick_breaking_prompt",
        )(*args, *prev_args)
    assert nq == 1
    blk = SB_TQ * SB_HEADS
    n_past = past_k.shape[2] // blk
    n_tail = n_past - 1
    state_shape = jax.ShapeDtypeStruct((bsz, SB_HEADS, tq, SB_DH), F32)
    state_blk = (1, SB_HEADS, tq, SB_DH)
    q_spec = pl.BlockSpec((tq, SB_W), lambda b, s: (off_q + b, 0))
    new_spec = pl.BlockSpec((None, t * SB_HEADS, SB_DH), lambda b, s: (l, b, 0))
    near_spec = pl.BlockSpec((None, None, blk, SB_DH), lambda b, s: (l, b, n_tail, 0))
    state_spec = pl.BlockSpec(state_blk, lambda b, s: (b, 0, 0, 0))
    args = [bq, k8, v8, past_k, past_v]
    prev_args, prev_specs, aliases = _alias_prev(prev, len(args))
    hb, acc, carry = pl.pallas_call(
        functools.partial(_sb_head_kernel, tq=tq),
        out_shape=(out_shape, state_shape, state_shape),
        grid=(bsz, 2),
        in_specs=[q_spec, new_spec, new_spec, near_spec, near_spec] + prev_specs,
        out_specs=(q_spec, state_spec, state_spec),
        scratch_shapes=scratch,
        input_output_aliases=aliases,
        compiler_params=_params("parallel", "arbitrary"),
        name="stick_breaking_sample",
    )(*args, *prev_args)

    alive = jnp.max(carry, axis=(1, 2, 3)) > SB_DEAD_CARRY
    fetch = jnp.maximum(lax.cummax(jnp.where(alive, jnp.arange(bsz), -1), axis=0), 0)
    tail_spec = pl.BlockSpec((None, None, n_tail * blk, SB_DH),
                             lambda b, alive_ref, fetch_ref: (l, fetch_ref[b], 0, 0))
    state_in_spec = pl.BlockSpec(state_blk, lambda b, alive_ref, fetch_ref: (fetch_ref[b], 0, 0, 0))
    rows_spec = pl.BlockSpec((tq, SB_W), lambda b, alive_ref, fetch_ref: (off_q + b, 0))
    return pl.pallas_call(
        functools.partial(_sb_tail_kernel, n_tail=n_tail),
        out_shape=out_shape,
        grid_spec=pltpu.PrefetchScalarGridSpec(
            num_scalar_prefetch=2,
            grid=(bsz,),
            in_specs=[
                pl.BlockSpec((tq, SB_W), lambda b, alive_ref, fetch_ref: (off_q + fetch_ref[b], 0)),
                tail_spec, tail_spec, state_in_spec, state_in_spec,
                rows_spec,
            ],
            out_specs=rows_spec,
            scratch_shapes=scratch,
        ),
        input_output_aliases={7: 0},
        compiler_params=_params("arbitrary"),
        name="stick_breaking_tail",
    )(alive.astype(jnp.int32), fetch.astype(jnp.int32), bq, past_k, past_v, acc, carry, hb)


def _pool_kernel(u_ref, buf_ref, w_ref, s_ref, *rest, tt, pos0):
    o_ref, ext_ref = rest[-2:]
    i = pl.program_id(1)

    @pl.when(i == 0)
    def _():
        ext_ref[0:POOL_HALO, :] = buf_ref[0]

    @pl.when(i > 0)
    def _():
        ext_ref[0:POOL_HALO, :] = ext_ref[tt:tt + POOL_HALO, :]

    ext_ref[POOL_HALO:POOL_HALO + tt, :] = u_ref[...]
    pos = pos0 + i * tt + lax.broadcasted_iota(jnp.int32, (tt, 1), 0)
    for gi, w in enumerate(POOL_WINDOWS):
        cs = slice(gi * POOL_G, (gi + 1) * POOL_G)
        cur = ext_ref[POOL_HALO:POOL_HALO + tt, cs]
        wsum = cur
        for j in range(1, w):
            wsum = wsum + ext_ref[POOL_HALO - j:POOL_HALO - j + tt, cs]
        cnt = jnp.minimum(pos + 1, w).astype(F32)
        d = (wsum / cnt - cur).astype(BF16)
        y = _dot(d, w_ref[gi]) * s_ref[:, cs]
        o_ref[:, cs] = y.astype(BF16)


def _pool(u, buf, w_pool, scale, l, prev, *, bsz, t, row_off, pos0):
    tt = min(256, t)
    nt = t // tt
    off = row_off // tt
    row_spec = pl.BlockSpec((tt, POOL_W), lambda b, i: (off + b * nt + i, 0))
    args = [u, buf, w_pool, scale.reshape(1, POOL_W)]
    prev_args, prev_specs, aliases = _alias_prev(prev, len(args))
    return pl.pallas_call(
        functools.partial(_pool_kernel, tt=tt, pos0=pos0),
        out_shape=jax.ShapeDtypeStruct((u.shape[0], POOL_W), BF16),
        grid=(bsz, nt),
        in_specs=[
            row_spec,
            pl.BlockSpec((1, POOL_HALO, POOL_W), lambda b, i: (b, 0, 0)),
            pl.BlockSpec((None, len(POOL_WINDOWS), POOL_G, POOL_G), lambda b, i: (l, 0, 0, 0)),
            pl.BlockSpec((1, POOL_W), lambda b, i: (0, 0)),
        ] + prev_specs,
        out_specs=row_spec,
        scratch_shapes=[pltpu.VMEM((POOL_HALO + tt, POOL_W), F32)],
        input_output_aliases=aliases,
        compiler_params=_params("parallel", "arbitrary"),
        name="pool_mix",
    )(*args, *prev_args)


X_SCALE = X_DH ** -0.5


def _cross_kernel(q_ref, k_ref, v_ref, *rest):
    o_ref = rest[-1]
    for h in range(X_HEADS):
        cs = slice(h * X_DH, (h + 1) * X_DH)
        k = k_ref[0, :, cs].astype(BF16)
        v = v_ref[0, :, cs].astype(BF16)
        s = lax.dot_general(q_ref[:, cs], k, _NT, preferred_element_type=F32) * X_SCALE
        e = jnp.exp(s - jnp.max(s, axis=-1, keepdims=True))
        p = e * (1.0 / jnp.sum(e, axis=-1, keepdims=True))
        o_ref[:, cs] = _dot(p.astype(BF16), v).astype(BF16)


def _cross_attend(qx, mk, mv, l, prev, *, bsz, t, row_off):
    tq = min(256, t)
    nq = t // tq
    off = row_off // tq
    row_spec = pl.BlockSpec((tq, X_W), lambda b, i: (off + b * nq + i, 0))
    mem_spec = pl.BlockSpec((None, 1, mk.shape[2], X_W), lambda b, i: (l, b, 0, 0))
    args = [qx, mk, mv]
    prev_args, prev_specs, aliases = _alias_prev(prev, len(args))
    return pl.pallas_call(
        _cross_kernel,
        out_shape=jax.ShapeDtypeStruct((qx.shape[0], X_W), BF16),
        grid=(bsz, nq),
        in_specs=[row_spec, mem_spec, mem_spec] + prev_specs,
        out_specs=row_spec,
        input_output_aliases=aliases,
        compiler_params=_params("parallel", "arbitrary"),
        name="cross_attend",
    )(*args, *prev_args)


def _merge_kernel(x_ref, wg0_ref, wg1_ref, wg2_ref, bg0_ref, bg1_ref, bg2_ref,
                  ha_ref, hb_ref, hc_ref, wa_ref, wb_ref, wc_ref, side_in_ref,
                  o_ref, side_out_ref, *, side_slabs):
    step = pl.program_id(0) * pl.num_programs(1) + pl.program_id(1)

    @pl.when(step < side_slabs)
    def _():
        side_out_ref[...] = side_in_ref[...].astype(BF16)

    x = x_ref[...]
    out = None
    for wg_ref, bg_ref, h_ref, wp_ref in ((wg0_ref, bg0_ref, ha_ref, wa_ref),
                                          (wg1_ref, bg1_ref, hb_ref, wb_ref),
                                          (wg2_ref, bg2_ref, hc_ref, wc_ref)):
        gate = jax.nn.sigmoid(lax.dot_general(x, wg_ref[...], _NT, preferred_element_type=F32)
                              + bg_ref[...])
        term = gate * _dot(h_ref[...], wp_ref[...])
        out = term if out is None else out + term
    o_ref[...] = out.astype(BF16)


MERGE_SIDE_ROWS = 16


def _merge(xb, wt_b, bg, ha, hb, hc, w_pa, w_pb, w_pc, side_w, l, *, bm=512, bn=256):
    m = xb.shape[0]
    _, sr, sc = side_w.shape
    side_slabs = sr // MERGE_SIDE_ROWS
    assert sr % MERGE_SIDE_ROWS == 0 and side_slabs <= (m // bm) * (D_MODEL // bn)
    nb = D_MODEL // bn
    gb = B_G // bn
    row = lambda width: pl.BlockSpec((bm, width), lambda i, j: (i, 0))
    colw = lambda kdim: pl.BlockSpec((None, kdim, bn), lambda i, j: (0, 0, j))
    gate_w = [pl.BlockSpec((None, bn, D_MODEL), lambda i, j, g=g: (0, gb + g * nb + j, 0))
              for g in range(N_BRANCH)]
    gate_b = [pl.BlockSpec((1, bn), lambda i, j, g=g: (0, g * nb + j)) for g in range(N_BRANCH)]
    slab = lambda i, j: jnp.minimum(i * nb + j, side_slabs - 1)
    return pl.pallas_call(
        functools.partial(_merge_kernel, side_slabs=side_slabs),
        out_shape=(jax.ShapeDtypeStruct((m, D_MODEL), BF16),
                   jax.ShapeDtypeStruct((1, sr, sc), BF16)),
        grid=(m // bm, nb),
        in_specs=[row(D_MODEL)] + gate_w + gate_b
                 + [row(MLSTM_W), row(SB_W), row(POOL_W), colw(MLSTM_W), colw(SB_W), colw(POOL_W),
                    pl.BlockSpec((None, MERGE_SIDE_ROWS, sc), lambda i, j: (l, slab(i, j), 0))],
        out_specs=(pl.BlockSpec((bm, bn), lambda i, j: (i, j)),
                   pl.BlockSpec((None, MERGE_SIDE_ROWS, sc), lambda i, j: (0, slab(i, j), 0))),
        compiler_params=_params("arbitrary", "arbitrary"),
        name="gated_merge",
    )(xb, wt_b, wt_b, wt_b, bg, bg, bg, ha, hb, hc, w_pa, w_pb, w_pc, side_w)


def _trunk_layer(x, xb, W, l, groups, kv_prev, bufs, last):
    b_in = W["b_in"][l]
    proj_a = functools.partial(_matmul, xb, W["wt_in"], l, wt=True, bn=512)
    group_rows = tuple(g["bsz"] * g["t"] for g in groups)

    k_scale = jnp.concatenate([jnp.ones((MLSTM_QK_W,), F32),
                               jnp.full((MLSTM_QK_W,), MLSTM_DK ** -0.5, F32),
                               jnp.ones((MLSTM_W,), F32)])
    qkv, wt_b = proj_a(n=OFF_AO, bias=b_in[:OFF_AO], scale=k_scale, out_dtype=BF16,
                       side_cast=[(W["wt_in"], l, OFF_BQ, IN_W - OFF_BQ, 256)], name="proj_qkv")
    proj_b = functools.partial(_matmul, xb, wt_b, 0, wt=True)
    whole = lambda name, slab: (W[name], l, 0, W[name].shape[1], slab)
    ao, w_out = proj_a(n=MLSTM_W, col_off=OFF_AO, bias=b_in[OFF_AO:OFF_AI], act="sigmoid",
                       side_cast=[whole("w_out", 128)], name="proj_ogate")
    b_if = jnp.pad(b_in[OFF_AI:OFF_BQ], (0, GATE_PAD - 2 * MLSTM_HEADS))
    gif, w_pa, w_pb, w_pc = proj_a(
        n=GATE_PAD, col_off=OFF_AI, bias=b_if, bn=GATE_PAD,
        side_cast=[whole("w_pa", 256), whole("w_pb", 128), whole("w_pc", 128)], name="proj_if")
    sec = lambda i: b_in[OFF_BQ + i * SB_W:OFF_BQ + (i + 1) * SB_W]
    bq, wq_x, wo_x = proj_b(n=SB_W, bias=sec(0), out_dtype=BF16,
                            side_cast=[whole("wq_x", 512), whole("wo_x", 128)], name="proj_sb_q")
    k8 = _proj_heads(xb, wt_b, l, sec(1), B_BK, group_rows, kv_prev[0], name="proj_sb_k")
    v8 = _proj_heads(xb, wt_b, l, sec(2), B_BV, group_rows, kv_prev[1], name="proj_sb_v")
    u = proj_b(n=POOL_W, col_off=B_CU, bias=sec(3), name="proj_pool_u")

    rows = xb.shape[0]
    if bufs is None:
        bufs = tuple(jnp.zeros((rows, w), BF16) for w in (MLSTM_W, SB_W, POOL_W, X_W))
    ha, hb, hc, ox = bufs
    states = []
    for gi, g in enumerate(groups):
        dims = dict(bsz=g["bsz"], t=g["t"], row_off=g["row_off"])
        ha, c_new, n_new, m_new = _mlstm(qkv, ao, gif, g["c"], g["n"], g["m"],
                                         W["mlstm_norm_g"][l], l, ha, **dims)
        hb = _stick_breaking(bq, k8[gi], v8[gi], g["sb_k"], g["sb_v"], l, hb, **dims)
        hc = _pool(u, g["pool_buf"][l], W["w_pool"], W["pool_scale"][l], l, hc,
                   pos0=g["pos0"], **dims)
        states.append((c_new, n_new, m_new))

    merged, w_up = _merge(xb, wt_b, b_in[OFF_G:].reshape(1, -1), ha, hb, hc,
                          w_pa, w_pb, w_pc, W["w_up"], l)
    y1 = _matmul(merged, w_out, 0, n=D_MODEL, res=x, name="w_out")
    ln1 = (W["ln1_g"][l], W["ln1_b"][l])
    mu1, rs1, xb = _layer_norm(y1, *ln1, emit_f32=False)

    qx = _matmul(xb, wq_x, 0, n=X_W, out_dtype=BF16, name="wq_x")
    for g in groups:
        ox = _cross_attend(qx, g["mk"], g["mv"], l, ox, bsz=g["bsz"], t=g["t"],
                           row_off=g["row_off"])
    y2 = _matmul(ox, wo_x, 0, n=D_MODEL, res=y1, res_ln=(mu1, rs1, *ln1), name="wo_x")
    ln2 = (W["ln2_g"][l], W["ln2_b"][l])
    mu2, rs2, xb = _layer_norm(y2, *ln2, emit_f32=False)

    hid, w_down = _matmul(xb, w_up, 0, n=D_FF, act="relu2", out_dtype=BF16,
                          side_cast=[whole("w_down", 128)], name="w_up")
    y = _matmul(hid, w_down, 0, n=D_MODEL, bk=D_MODEL, name="w_down")
    if last:
        x = tuple(_layer_norm(y2, W["ln3_g"][l], W["ln3_b"][l], res=y, x_ln=(mu2, rs2, *ln2),
                              in_off=g["row_off"], n_rows=g["bsz"] * g["t"], emit_bf16=False)
                  for g in groups)
        xb = None
    else:
        x, xb = _layer_norm(y2, W["ln3_g"][l], W["ln3_b"][l], res=y, x_ln=(mu2, rs2, *ln2))
    return x, xb, (k8, v8), u, states, (ha, hb, hc, ox)


def kernel(x_prompt, x_sample, cache_sb_k, cache_sb_v, state_mlstm_c, state_mlstm_n, state_mlstm_m, state_pool, cache_mem_k, cache_mem_v, mem_prompt, ln_in_g, ln_in_b, w_in, b_in, mlstm_norm_g, w_pool, pool_scale, w_pa, w_pb, w_pc, w_out, ln1_g, ln1_b, wq_x, wk_x, wv_x, wo_x, ln2_g, ln2_b, w_up, w_down, ln3_g, ln3_b):
    bf = lambda a: a.astype(BF16)
    wt_in = jnp.swapaxes(w_in, 1, 2)
    W = dict(wt_in=wt_in,
             b_in=b_in, mlstm_norm_g=mlstm_norm_g, w_pool=bf(w_pool),
             pool_scale=pool_scale, w_pa=w_pa, w_pb=w_pb, w_pc=w_pc, w_out=w_out,
             ln1_g=ln1_g, ln1_b=ln1_b, wq_x=wq_x, wo_x=wo_x, ln2_g=ln2_g, ln2_b=ln2_b,
             w_up=w_up, w_down=w_down, ln3_g=ln3_g, ln3_b=ln3_b)
    bp, tp, _ = x_prompt.shape
    bs, ts, _ = x_sample.shape
    rows_p, rows_s = bp * tp, bs * ts
    rows = rows_p + rows_s
    past = cache_sb_k.shape[2]
    mem_len = mem_prompt.shape[1]

    mem_b = mem_prompt.reshape(bp * mem_len, D_MODEL).astype(BF16)
    wk_b, wv_b = bf(wk_x), bf(wv_x)
    mem4 = lambda a, b: a.reshape(DEPTH, b, mem_len, X_W)
    mk_p = mem4(jnp.stack([_matmul(mem_b, wk_b, l, n=X_W, name="mem_k") for l in range(DEPTH)]), bp)
    mv_p = mem4(jnp.stack([_matmul(mem_b, wv_b, l, n=X_W, name="mem_v") for l in range(DEPTH)]), bp)

    lane_m = lambda m: jnp.broadcast_to(m[..., None], m.shape + (LANES,))
    halo = lambda b: jnp.pad(b, ((0, 0), (0, 0), (POOL_HALO - POOL_BUF, 0), (0, 0)))
    heads_rows = lambda c: c.reshape(DEPTH, bs, past * SB_HEADS, SB_DH)
    groups = [
        dict(bsz=bp, t=tp, row_off=0, pos0=0,
             c=jnp.zeros((DEPTH, bp, MLSTM_HEADS, MLSTM_DK, MLSTM_DV), F32),
             n=jnp.zeros((DEPTH, bp, MLSTM_HEADS, MLSTM_DK), F32),
             m=jnp.zeros((DEPTH, bp, MLSTM_HEADS, LANES), F32),
             sb_k=None, sb_v=None,
             pool_buf=jnp.zeros((DEPTH, bp, POOL_HALO, POOL_W), F32),
             mk=mk_p, mv=mv_p),
        dict(bsz=bs, t=ts, row_off=rows_p, pos0=past,
             c=state_mlstm_c, n=state_mlstm_n, m=lane_m(state_mlstm_m),
             sb_k=heads_rows(cache_sb_k), sb_v=heads_rows(cache_sb_v),
             pool_buf=halo(state_pool),
             mk=mem4(cache_mem_k, bs), mv=mem4(cache_mem_v, bs)),
    ]

    xs = None
    for g, xin in zip(groups, (x_prompt, x_sample)):
        xs = _layer_norm(xin.reshape(g["bsz"] * g["t"], D_MODEL), ln_in_g, ln_in_b,
                         out_off=g["row_off"], out_rows=rows, prev=xs)
    x, xb = xs

    kv = tuple(tuple(jnp.zeros((DEPTH, g["bsz"] * g["t"] * SB_HEADS, SB_DH), F32) for g in groups)
               for _ in range(2))
    per_layer = []
    bufs = None
    for l in range(DEPTH):
        x, xb, kv, u, states, bufs = _trunk_layer(x, xb, W, l, groups, kv, bufs,
                                                  last=l == DEPTH - 1)
        per_layer.append((u, states))

    def group_out(gi, g):
        bsz, t, r0 = g["bsz"], g["t"], g["row_off"]
        sbk = kv[0][gi].reshape(DEPTH, bsz, t, SB_HEADS, SB_DH)
        sbv = kv[1][gi].reshape(DEPTH, bsz, t, SB_HEADS, SB_DH)
        c = jnp.stack([st[gi][0] for _, st in per_layer])
        n = jnp.stack([st[gi][1] for _, st in per_layer])
        m = jnp.stack([st[gi][2][..., 0] for _, st in per_layer])
        pool = jnp.stack([u[r0:r0 + bsz * t].reshape(bsz, t, POOL_W)[:, t - POOL_BUF:, :]
                          for u, _ in per_layer])
        return sbk, sbv, c, n, m, pool

    out_p = group_out(0, groups[0])
    out_s = group_out(1, groups[1])
    y_prompt = x[0].reshape(bp, tp, D_MODEL)
    y_sample = x[1].reshape(bs, ts, D_MODEL)
    mem5 = lambda a: a.reshape(DEPTH, bp, mem_len, X_HEADS, X_DH)
    return (y_prompt, y_sample) + out_p + (mem5(mk_p), mem5(mv_p)) + out_s
```

```python
import functools

import jax
import jax.numpy as jnp
from jax import lax
from jax.experimental import pallas as pl
from jax.experimental.pallas import tpu as pltpu

F32 = jnp.float32
BF16 = jnp.bfloat16

D_MODEL = 4096
DEPTH = 2
MLSTM_CHUNK = 256
MLSTM_HEADS = 8
MLSTM_W = D_MODEL // 2
MLSTM_DV = MLSTM_W // MLSTM_HEADS
MLSTM_DK = MLSTM_DV // 2
MLSTM_QK_W = MLSTM_HEADS * MLSTM_DK
SB_HEADS = 8
SB_W = D_MODEL // 4
SB_DH = SB_W // SB_HEADS
POOL_WINDOWS = (2, 4, 8, 16)
POOL_W = D_MODEL // 4
POOL_G = POOL_W // len(POOL_WINDOWS)
POOL_BUF = max(POOL_WINDOWS) - 1
POOL_HALO = POOL_BUF + 1
X_HEADS = 4
X_W = D_MODEL // 4
X_DH = X_W // X_HEADS
D_FF = 4 * D_MODEL
N_BRANCH = 3
ALPHA = (2 * DEPTH) ** 0.25
LN_EPS = 1e-5
HEAD_NORM_EPS = 1e-6

OFF_AO = 2 * MLSTM_QK_W + MLSTM_W
OFF_AI = OFF_AO + MLSTM_W
OFF_BQ = OFF_AI + 2 * MLSTM_HEADS
OFF_G = OFF_BQ + 3 * SB_W + POOL_W
IN_W = OFF_G + N_BRANCH * D_MODEL

V7X_VMEM_LIMIT_BYTES = 56 * 1024 * 1024
LANES = 128
SUBLANES = 8
SB_TQ = 256
GATE_PAD = LANES

B_BK = SB_W
B_BV = 2 * SB_W
B_CU = 3 * SB_W
B_G = 3 * SB_W + POOL_W

_NT = (((1,), (1,)), ((), ()))
_TN = (((0,), (0,)), ((), ()))


def _params(*sem):
    return pltpu.CompilerParams(dimension_semantics=sem,
                                vmem_limit_bytes=V7X_VMEM_LIMIT_BYTES)


def _dot(a, b):
    return jnp.dot(a, b, preferred_element_type=F32)


def _split(x, terms):
    out = []
    for _ in range(terms - 1):
        h = x.astype(BF16)
        out.append(h)
        x = x - h.astype(F32)
    out.append(x.astype(BF16))
    return out


def _apply_ln(x, mu_ref, rs_ref, g_ref, b_ref):
    return (x - mu_ref[:, 0:1]) * rs_ref[:, 0:1] * g_ref[...] + b_ref[...]


def _alias_prev(prev, n_in):
    if prev is None:
        return [], [], {}
    prev = list(prev) if isinstance(prev, (tuple, list)) else [prev]
    specs = [pl.BlockSpec(memory_space=pl.ANY)] * len(prev)
    return prev, specs, {n_in + i: i for i in range(len(prev))}


def _mm_kernel(*refs, act, has_bias, has_scale, has_res, res_ln, nk, side_slabs, wt):
    it = iter(refs)
    x_ref, w_ref = next(it), next(it)
    b_ref = next(it) if has_bias else None
    s_ref = next(it) if has_scale else None
    r_ref = next(it) if has_res else None
    ln_refs = [next(it) for _ in range(4)] if res_ln else None
    side_in_refs = [next(it) for _ in side_slabs]
    o_ref = next(it)
    side_out_refs = [next(it) for _ in side_slabs]
    acc_ref = next(it) if nk > 1 else None

    if side_slabs:
        step = pl.program_id(0) * pl.num_programs(1) + pl.program_id(1)
        for n_slabs, in_ref, out_ref in zip(side_slabs, side_in_refs, side_out_refs):
            @pl.when(step < n_slabs)
            def _(in_ref=in_ref, out_ref=out_ref):
                out_ref[...] = in_ref[0].astype(BF16)

    def epilogue(y):
        if has_bias:
            y = y + b_ref[...]
        if has_scale:
            y = y * s_ref[...]
        if act == "sigmoid":
            y = jax.nn.sigmoid(y)
        elif act == "relu2":
            y = jnp.square(jnp.maximum(y, 0.0))
        if has_res:
            r = r_ref[...]
            if res_ln:
                r = _apply_ln(r, *ln_refs)
            y = ALPHA * r + y
        o_ref[...] = y.astype(o_ref.dtype)

    w = w_ref[...].astype(BF16)
    part = lax.dot_general(x_ref[...], w, _NT if wt else (((1,), (0,)), ((), ())),
                           preferred_element_type=F32)
    if nk == 1:
        epilogue(part)
    else:
        k = pl.program_id(2)

        @pl.when(k == 0)
        def _():
            acc_ref[...] = part

        @pl.when(k > 0)
        def _():
            acc_ref[...] += part

        @pl.when(k == nk - 1)
        def _():
            epilogue(acc_ref[...])


def _matmul(x, w, l, *, n, col_off=0, bias=None, scale=None, res=None, res_ln=None, act="none",
            out_dtype=F32, bm=1024, bn=1024, bk=None, side_cast=(), wt=False, name="matmul"):
    m, kdim = x.shape
    bm, bn = min(bm, m), min(bn, n)
    bk = kdim if bk is None else bk
    assert m % bm == 0 and n % bn == 0 and kdim % bk == 0 and col_off % bn == 0
    nk = kdim // bk
    cb = col_off // bn
    nj = n // bn
    w_spec = (pl.BlockSpec((None, bn, bk), lambda i, j, k: (l, cb + j, k)) if wt
              else pl.BlockSpec((None, bk, bn), lambda i, j, k: (l, k, cb + j)))
    in_specs = [pl.BlockSpec((bm, bk), lambda i, j, k: (i, k)), w_spec]
    args = [x, w]
    for v in (bias, scale):
        if v is not None:
            in_specs.append(pl.BlockSpec((1, bn), lambda i, j, k: (0, j)))
            args.append(v.reshape(1, n).astype(F32))
    if res is not None:
        in_specs.append(pl.BlockSpec((bm, bn), lambda i, j, k: (i, j)))
        args.append(res)
    if res_ln is not None:
        mu, rs, ln_g, ln_b = res_ln
        in_specs += [pl.BlockSpec((bm, LANES), lambda i, j, k: (i, 0))] * 2
        in_specs += [pl.BlockSpec((1, bn), lambda i, j, k: (0, j))] * 2
        args += [mu, rs, ln_g.reshape(1, n), ln_b.reshape(1, n)]
    out_shape = [jax.ShapeDtypeStruct((m, n), out_dtype)]
    out_specs = [pl.BlockSpec((bm, bn), lambda i, j, k: (i, j))]
    side_slabs = []
    for side_w, side_l, row_off, sr, slab_rows in side_cast:
        sc = side_w.shape[2]
        n_slabs = sr // slab_rows
        assert nk == 1 and sr % slab_rows == 0 and n_slabs <= (m // bm) * nj
        slab = lambda i, j, k, n_slabs=n_slabs: jnp.minimum(i * nj + j, n_slabs - 1)
        in_specs.append(pl.BlockSpec(
            (pl.Element(1), pl.Element(slab_rows), pl.Element(sc)),
            lambda i, j, k, slab=slab, side_l=side_l, row_off=row_off, slab_rows=slab_rows: (
                side_l, pl.multiple_of(row_off + slab(i, j, k) * slab_rows, SUBLANES), 0)))
        args.append(side_w)
        out_shape.append(jax.ShapeDtypeStruct((1, sr, sc), BF16))
        out_specs.append(pl.BlockSpec((None, slab_rows, sc),
                                      lambda i, j, k, slab=slab: (0, slab(i, j, k), 0)))
        side_slabs.append(n_slabs)
    kern = functools.partial(_mm_kernel, act=act, has_bias=bias is not None,
                             has_scale=scale is not None, has_res=res is not None,
                             res_ln=res_ln is not None, nk=nk,
                             side_slabs=tuple(side_slabs), wt=wt)
    out = pl.pallas_call(
        kern,
        out_shape=tuple(out_shape),
        grid=(m // bm, nj, nk),
        in_specs=in_specs,
        out_specs=tuple(out_specs),
        scratch_shapes=[pltpu.VMEM((bm, bn), F32)] if nk > 1 else [],
        compiler_params=(_params("arbitrary", "arbitrary", "arbitrary") if side_slabs
                         else _params("parallel", "parallel", "arbitrary")),
        name=name,
    )(*args)
    return out if side_slabs else out[0]


def _proj_heads_kernel(x_ref, w_ref, b_ref, *rest, n_first):
    first_ref, second_ref = rest[-2:]
    y = lax.dot_general(x_ref[...], w_ref[...], _NT, preferred_element_type=F32) + b_ref[...]
    rows = y.shape[0]

    def store(o_ref):
        for h in range(SUBLANES):
            o_ref[pl.ds(h, rows, stride=SUBLANES), :] = y[:, h * LANES:(h + 1) * LANES]

    i = pl.program_id(0)

    @pl.when(i < n_first)
    def _():
        store(first_ref)

    @pl.when(i >= n_first)
    def _():
        store(second_ref)


def _proj_heads(x, w, l, bias, col_off, group_rows, prev, *, bm=1024, name="proj_heads"):
    m, kdim = x.shape
    n = SUBLANES * LANES
    rows_a, rows_b = group_rows
    assert rows_a % bm == 0 and rows_b % bm == 0 and rows_a + rows_b == m and col_off % n == 0
    n_first = rows_a // bm
    cb = col_off // n
    args = [x, w, bias.reshape(1, n)]
    prev_args, prev_specs, aliases = _alias_prev(prev, len(args))
    blk = (None, bm * SUBLANES, LANES)
    return pl.pallas_call(
        functools.partial(_proj_heads_kernel, n_first=n_first),
        out_shape=tuple(jax.ShapeDtypeStruct((DEPTH, r * SUBLANES, LANES), F32) for r in group_rows),
        grid=(m // bm,),
        in_specs=[pl.BlockSpec((bm, kdim), lambda i: (i, 0)),
                  pl.BlockSpec((None, n, kdim), lambda i: (0, cb, 0)),
                  pl.BlockSpec((1, n), lambda i: (0, 0))] + prev_specs,
        out_specs=(pl.BlockSpec(blk, lambda i: (l, jnp.minimum(i, n_first - 1), 0)),
                   pl.BlockSpec(blk, lambda i: (l, jnp.maximum(i - n_first, 0), 0))),
        input_output_aliases=aliases,
        compiler_params=_params("arbitrary"),
        name=name,
    )(*args, *prev_args)


def _ln_kernel(*refs, x_ln, has_res, n_prev, emit_f32, emit_bf16, n_blocks):
    it = iter(refs)
    x_ref = next(it)
    x_ln_refs = [next(it) for _ in range(4)] if x_ln else None
    r_ref = next(it) if has_res else None
    g_ref, b_ref = next(it), next(it)
    for _ in range(n_prev):
        next(it)
    x = x_ref[...]
    if x_ln:
        x = _apply_ln(x, *x_ln_refs)
    if has_res:
        x = ALPHA * x + r_ref[...]
    mu = jnp.mean(x, axis=-1, keepdims=True)
    xc = x - mu
    var = jnp.mean(xc * xc, axis=-1, keepdims=True)
    rs = lax.rsqrt(var + LN_EPS)
    y = xc * rs * g_ref[...] + b_ref[...]
    if n_blocks is not None:
        y = jnp.where(pl.program_id(0) < n_blocks, y, 0.0)
    if emit_f32:
        next(it)[...] = y
    else:
        mu_ref, rs_ref = next(it), next(it)
        mu_ref[...] = jnp.broadcast_to(mu, mu_ref.shape)
        rs_ref[...] = jnp.broadcast_to(rs, rs_ref.shape)
    if emit_bf16:
        next(it)[...] = y.astype(BF16)


def _layer_norm(x, g, b, res=None, *, x_ln=None, in_off=0, n_rows=None, out_off=0, out_rows=None,
                prev=None, emit_f32=True, emit_bf16=True, rows=256):
    d = x.shape[1]
    n_rows = x.shape[0] - in_off if n_rows is None else n_rows
    out_rows = n_rows if out_rows is None else out_rows
    ib, ob = in_off // rows, out_off // rows
    n_blocks = n_rows // rows
    fill = prev is None and out_rows > n_rows
    assert not fill or out_off == 0
    in_spec = pl.BlockSpec((rows, d), lambda i: (ib + jnp.minimum(i, n_blocks - 1), 0))
    stat_in_spec = pl.BlockSpec((rows, LANES), lambda i: (ib + jnp.minimum(i, n_blocks - 1), 0))
    out_spec = pl.BlockSpec((rows, d), lambda i: (ob + i, 0))
    stat_spec = pl.BlockSpec((rows, LANES), lambda i: (ob + i, 0))
    vec_spec = pl.BlockSpec((1, d), lambda i: (0, 0))
    args, in_specs = [x], [in_spec]
    if x_ln is not None:
        args += [x_ln[0], x_ln[1], x_ln[2].reshape(1, d), x_ln[3].reshape(1, d)]
        in_specs += [stat_in_spec, stat_in_spec, vec_spec, vec_spec]
    if res is not None:
        args.append(res)
        in_specs.append(in_spec)
    args += [g.reshape(1, d), b.reshape(1, d)]
    in_specs += [vec_spec, vec_spec]
    prev_args, prev_specs, aliases = _alias_prev(prev, len(args))
    if emit_f32:
        out_shape, out_specs = [jax.ShapeDtypeStruct((out_rows, d), F32)], [out_spec]
    else:
        out_shape = [jax.ShapeDtypeStruct((out_rows, LANES), F32)] * 2
        out_specs = [stat_spec, stat_spec]
    if emit_bf16:
        out_shape.append(jax.ShapeDtypeStruct((out_rows, d), BF16))
        out_specs.append(out_spec)
    out = pl.pallas_call(
        functools.partial(_ln_kernel, x_ln=x_ln is not None, has_res=res is not None,
                          n_prev=len(prev_args), emit_f32=emit_f32, emit_bf16=emit_bf16,
                          n_blocks=n_blocks if fill else None),
        out_shape=tuple(out_shape),
        grid=(out_rows // rows if fill else n_blocks,),
        in_specs=in_specs + prev_specs,
        out_specs=tuple(out_specs),
        input_output_aliases=aliases,
        compiler_params=_params("parallel"),
        name="layer_norm",
    )(*args, *prev_args)
    return out if len(out) > 1 else out[0]


def _mlstm_kernel(q_ref, k_ref, v_ref, ao_ref, gif_ref, c0_ref, n0_ref, m0_ref, g_ref, *rest,
                  n_seq):
    ha_ref, c_ref, n_ref, m_ref = rest[-4:]
    L = q_ref.shape[0] // n_seq

    @pl.when(pl.program_id(1) == 0)
    def _():
        c_ref[...] = c0_ref[...]
        n_ref[...] = n0_ref[...]
        m_ref[...] = m0_ref[...]

    row = lax.broadcasted_iota(jnp.int32, (L, L), 0)
    col = lax.broadcasted_iota(jnp.int32, (L, L), 1)
    causal = col <= row
    tril = jnp.where(causal, 1.0, 0.0).astype(BF16)
    sel_r = lax.broadcasted_iota(jnp.int32, (2 * MLSTM_HEADS, GATE_PAD), 0)
    sel_c = lax.broadcasted_iota(jnp.int32, (2 * MLSTM_HEADS, GATE_PAD), 1)
    sel = jnp.where(sel_r == sel_c, 1.0, 0.0).astype(BF16)

    def rows_of(x):
        return sum(lax.dot_general(sel, t, _NT, preferred_element_type=F32) for t in _split(x, 3))

    for s_i in range(n_seq):
        rs = slice(s_i * L, (s_i + 1) * L)
        gif = gif_ref[rs, :]
        logf = -(jnp.maximum(-gif, 0.0) + jnp.log1p(jnp.exp(-jnp.abs(gif))))
        b_all = sum(_dot(tril, t) for t in _split(logf, 3))
        i_rows = rows_of(gif)
        b_rows = rows_of(b_all)

        for h in range(MLSTM_HEADS):
            qs = slice(h * MLSTM_DK, (h + 1) * MLSTM_DK)
            vs = slice(h * MLSTM_DV, (h + 1) * MLSTM_DV)
            q = q_ref[rs, qs]
            k = k_ref[rs, qs]
            v = v_ref[rs, vs]
            i_col = gif[:, h:h + 1]
            b_col = b_all[:, MLSTM_HEADS + h:MLSTM_HEADS + h + 1]
            i_row = i_rows[h:h + 1, :]
            b_row = b_rows[MLSTM_HEADS + h:MLSTM_HEADS + h + 1, :]
            c_old = c_ref[s_i, h]
            n_old = n_ref[s_i, h:h + 1, :]
            m_old = m_ref[s_i, h:h + 1, 0:1]

            dmat = jnp.where(causal, b_col - b_row + i_row, -jnp.inf)
            m_inter = b_col + m_old
            m_t = jnp.maximum(m_inter, jnp.max(dmat, axis=-1, keepdims=True))
            s = lax.dot_general(q, k, _NT, preferred_element_type=F32) * jnp.exp(dmat - m_t)
            w_inter = jnp.exp(m_inter - m_t)
            kf = k.astype(F32)
            num = _dot(s.astype(BF16), v) + w_inter * _dot(q, c_old.astype(BF16))
            qn = jnp.sum(q.astype(F32) * n_old, axis=-1, keepdims=True)
            den = jnp.sum(s, axis=-1, keepdims=True) + w_inter * qn
            hh = num * (1.0 / jnp.maximum(jnp.abs(den), jnp.exp(-m_t)))

            m_new = m_t[L - 1:L, :]
            b_last = b_col[L - 1:L, :]
            g_col = jnp.exp(b_last - b_col + i_col - m_new)
            decay = jnp.exp(b_last + m_old - m_new)
            kg = kf * g_col
            c_ref[s_i, h] = decay * c_old + lax.dot_general(kg.astype(BF16), v, _TN,
                                                            preferred_element_type=F32)
            n_ref[s_i, h:h + 1, :] = decay * n_old + jnp.sum(kg, axis=0, keepdims=True)
            m_ref[s_i, h:h + 1, :] = jnp.broadcast_to(m_new, (1, LANES))

            mu = jnp.mean(hh, axis=-1, keepdims=True)
            hc = hh - mu
            var = jnp.mean(hc * hc, axis=-1, keepdims=True)
            hn = hc * lax.rsqrt(var + HEAD_NORM_EPS) * g_ref[:, vs]
            ha_ref[rs, vs] = (ao_ref[rs, vs] * hn).astype(BF16)


def _mlstm(qkv, ao, gif, c0, n0, m0, norm_g, l, prev, *, bsz, t, row_off):
    chunk = min(MLSTM_CHUNK, t)
    nc = t // chunk
    n_seq = MLSTM_CHUNK // chunk if nc == 1 and bsz % (MLSTM_CHUNK // chunk) == 0 else 1
    blk = n_seq * chunk
    off = row_off // blk
    rows = lambda b, c: off + b * nc + c
    state4 = pl.BlockSpec((n_seq, MLSTM_HEADS, MLSTM_DK, MLSTM_DV), lambda b, c: (b, 0, 0, 0))
    state3 = pl.BlockSpec((n_seq, MLSTM_HEADS, LANES), lambda b, c: (b, 0, 0))
    init4 = pl.BlockSpec((None, n_seq, MLSTM_HEADS, MLSTM_DK, MLSTM_DV),
                         lambda b, c: (l, b, 0, 0, 0))
    init3 = pl.BlockSpec((None, n_seq, MLSTM_HEADS, LANES), lambda b, c: (l, b, 0, 0))
    args = [qkv, qkv, qkv, ao, gif, c0, n0, m0, norm_g.reshape(1, MLSTM_W)]
    prev_args, prev_specs, aliases = _alias_prev(prev, len(args))
    return pl.pallas_call(
        functools.partial(_mlstm_kernel, n_seq=n_seq),
        out_shape=(jax.ShapeDtypeStruct((qkv.shape[0], MLSTM_W), BF16),
                   jax.ShapeDtypeStruct(c0.shape[1:], F32),
                   jax.ShapeDtypeStruct(n0.shape[1:], F32),
                   jax.ShapeDtypeStruct(m0.shape[1:], F32)),
        grid=(bsz // n_seq, nc),
        in_specs=[
            pl.BlockSpec((blk, MLSTM_QK_W), lambda b, c: (rows(b, c), 0)),
            pl.BlockSpec((blk, MLSTM_QK_W), lambda b, c: (rows(b, c), 1)),
            pl.BlockSpec((blk, MLSTM_W), lambda b, c: (rows(b, c), 1)),
            pl.BlockSpec((blk, MLSTM_W), lambda b, c: (rows(b, c), 0)),
            pl.BlockSpec((blk, GATE_PAD), lambda b, c: (rows(b, c), 0)),
            init4, init3, init3,
            pl.BlockSpec((1, MLSTM_W), lambda b, c: (0, 0)),
        ] + prev_specs,
        out_specs=(pl.BlockSpec((blk, MLSTM_W), lambda b, c: (rows(b, c), 0)),
                   state4, state3, state3),
        input_output_aliases=aliases,
        compiler_params=_params("parallel", "arbitrary"),
        name="mlstm",
    )(*args, *prev_args)


SB_SCALE = SB_DH ** -0.5
SB_DEAD_CARRY = -110.0


def _lower_tri(n):
    return jnp.where(lax.broadcasted_iota(jnp.int32, (n, n), 0)
                     >= lax.broadcasted_iota(jnp.int32, (n, n), 1), 1.0, 0.0).astype(BF16)


def _sb_group(q_ref, load_k, load_v, heads, acc_ref, carry_ref, lower, *, diag):
    tq = q_ref.shape[0]
    tk = lower.shape[0]
    if diag:
        mask = (lax.broadcasted_iota(jnp.int32, (tq, tk), 1)
                < lax.broadcasted_iota(jnp.int32, (tq, tk), 0))
    zs, lks = [], []
    for h in heads:
        z = lax.dot_general(q_ref[:, h * SB_DH:(h + 1) * SB_DH], load_k(h), _NT,
                            preferred_element_type=F32) * SB_SCALE
        log_keep = -(jnp.maximum(z, 0.0) + jnp.log(1.0 + jnp.exp(-jnp.abs(z))))
        if diag:
            log_keep = jnp.where(mask, log_keep, 0.0)
        zs.append(z)
        lks.append(log_keep)
    stacked = jnp.concatenate(lks, axis=0)
    rc_all = sum(_dot(t, lower) for t in _split(stacked, 2))
    for i, h in enumerate(heads):
        rc = rc_all[i * tq:(i + 1) * tq]
        carry = carry_ref[h]
        attn = jnp.exp(zs[i] + rc + carry)
        if diag:
            attn = jnp.where(mask, attn, 0.0)
        acc_ref[h] += _dot(attn.astype(BF16), load_v(h))
        carry_ref[h] = carry + rc[:, 0:1]


def _sb_block(q_ref, k_ref, v_ref, base, tk, acc_ref, carry_ref, lower, *, group, diag):
    def loader(ref):
        return lambda h: ref[pl.ds(base + h, tk, stride=SB_HEADS), :].astype(BF16)
    for g in range(0, SB_HEADS, group):
        _sb_group(q_ref, loader(k_ref), loader(v_ref), range(g, g + group),
                  acc_ref, carry_ref, lower, diag=diag)


def _sb_alive(carry_ref):
    return jnp.max(carry_ref[...]) > SB_DEAD_CARRY


def _sb_store(o_ref, acc_ref):
    for h in range(SB_HEADS):
        o_ref[:, h * SB_DH:(h + 1) * SB_DH] = acc_ref[h].astype(BF16)


def _sb_prompt_kernel(q_ref, k_ref, v_ref, *rest, tq, group):
    o_ref, acc_ref, carry_ref = rest[-3:]
    qi = pl.program_id(1)
    acc_ref[...] = jnp.zeros_like(acc_ref)
    carry_ref[...] = jnp.zeros_like(carry_ref)
    lower = _lower_tri(tq)
    blk_rows = tq * SB_HEADS

    _sb_block(q_ref, k_ref, v_ref, pl.multiple_of(qi * blk_rows, blk_rows), tq,
              acc_ref, carry_ref, lower, group=group, diag=True)

    def more(state):
        j, alive = state
        return jnp.logical_and(j < qi, alive)

    def body(state):
        j, _ = state
        base = pl.multiple_of((qi - 1 - j) * blk_rows, blk_rows)
        _sb_block(q_ref, k_ref, v_ref, base, tq, acc_ref, carry_ref, lower,
                  group=group, diag=False)
        return j + 1, _sb_alive(carry_ref)

    lax.while_loop(more, body, (jnp.int32(0), _sb_alive(carry_ref)))
    _sb_store(o_ref, acc_ref)


def _sb_head_kernel(q_ref, k_ref, v_ref, kp_ref, vp_ref, *rest, tq):
    o_ref, acc_out_ref, carry_out_ref, acc_ref, carry_ref = rest[-5:]
    s = pl.program_id(1)

    @pl.when(s == 0)
    def _():
        acc_ref[...] = jnp.zeros_like(acc_ref)
        carry_ref[...] = jnp.zeros_like(carry_ref)
        _sb_block(q_ref, k_ref, v_ref, 0, tq, acc_ref, carry_ref, _lower_tri(tq),
                  group=SB_HEADS, diag=True)

    @pl.when(s == 1)
    def _():
        @pl.when(_sb_alive(carry_ref))
        def _():
            _sb_block(q_ref, kp_ref, vp_ref, 0, SB_TQ, acc_ref, carry_ref, _lower_tri(SB_TQ),
                      group=SB_HEADS, diag=False)

        _sb_store(o_ref, acc_ref)
        acc_out_ref[0] = acc_ref[...]
        carry_out_ref[0] = jnp.broadcast_to(carry_ref[...], carry_out_ref.shape[1:])


def _sb_tail_kernel(alive_ref, fetch_ref, q_ref, kp_ref, vp_ref, acc_in_ref, carry_in_ref, hb_ref,
                    o_ref, acc_ref, carry_ref, *, n_tail):
    del fetch_ref
    b = pl.program_id(0)

    @pl.when(alive_ref[b] == 0)
    def _():
        o_ref[...] = hb_ref[...]

    @pl.when(alive_ref[b] != 0)
    def _():
        acc_ref[...] = acc_in_ref[0]
        carry_ref[...] = carry_in_ref[0][:, :, 0:1]
        lower = _lower_tri(SB_TQ)
        blk_rows = SB_TQ * SB_HEADS

        def body(j, c):
            @pl.when(_sb_alive(carry_ref))
            def _():
                base = pl.multiple_of((n_tail - 1 - j) * blk_rows, blk_rows)
                _sb_block(q_ref, kp_ref, vp_ref, base, SB_TQ, acc_ref, carry_ref, lower,
                          group=SB_HEADS, diag=False)
            return c

        lax.fori_loop(0, n_tail, body, 0)
        _sb_store(o_ref, acc_ref)


def _stick_breaking(bq, k8, v8, past_k, past_v, l, prev, *, bsz, t, row_off):
    tq = min(SB_TQ, t)
    nq = t // tq
    off_q = row_off // tq
    scratch = [pltpu.VMEM((SB_HEADS, tq, SB_DH), F32), pltpu.VMEM((SB_HEADS, tq, 1), F32)]
    out_shape = jax.ShapeDtypeStruct((bq.shape[0], SB_W), BF16)
    q_spec = pl.BlockSpec((tq, SB_W), lambda b, i: (off_q + b * nq + i, 0))
    if past_k is None:
        seq_spec = pl.BlockSpec((None, t * SB_HEADS, SB_DH), lambda b, i: (l, b, 0))
        args = [bq, k8, v8]
        prev_args, prev_specs, aliases = _alias_prev(prev, len(args))
        return pl.pallas_call(
            functools.partial(_sb_prompt_kernel, tq=tq, group=SB_HEADS // 2),
            out_shape=out_shape,
            grid=(bsz, nq),
            in_specs=[q_spec, seq_spec, seq_spec] + prev_specs,
            out_specs=q_spec,
            scratch_shapes=scratch,
            input_output_aliases=aliases,
            compiler_params=_params("parallel", "arbitrary"),
            name="stick_breaking_prompt",
        )(*args, *prev_args)
    assert nq == 1
    blk = SB_TQ * SB_HEADS
    n_past = past_k.shape[2] // blk
    n_tail = n_past - 1
    state_shape = jax.ShapeDtypeStruct((bsz, SB_HEADS, tq, SB_DH), F32)
    state_blk = (1, SB_HEADS, tq, SB_DH)
    q_spec = pl.BlockSpec((tq, SB_W), lambda b, s: (off_q + b, 0))
    new_spec = pl.BlockSpec((None, t * SB_HEADS, SB_DH), lambda b, s: (l, b, 0))
    near_spec = pl.BlockSpec((None, None, blk, SB_DH), lambda b, s: (l, b, n_tail, 0))
    state_spec = pl.BlockSpec(state_blk, lambda b, s: (b, 0, 0, 0))
    args = [bq, k8, v8, past_k, past_v]
    prev_args, prev_specs, aliases = _alias_prev(prev, len(args))
    hb, acc, carry = pl.pallas_call(
        functools.partial(_sb_head_kernel, tq=tq),
        out_shape=(out_shape, state_shape, state_shape),
        grid=(bsz, 2),
        in_specs=[q_spec, new_spec, new_spec, near_spec, near_spec] + prev_specs,
        out_specs=(q_spec, state_spec, state_spec),
        scratch_shapes=scratch,
        input_output_aliases=aliases,
        compiler_params=_params("parallel", "arbitrary"),
        name="stick_breaking_sample",
    )(*args, *prev_args)

    alive = jnp.max(carry, axis=(1, 2, 3)) > SB_DEAD_CARRY
    fetch = jnp.maximum(lax.cummax(jnp.where(alive, jnp.arange(bsz), -1), axis=0), 0)
    tail_spec = pl.BlockSpec((None, None, n_tail * blk, SB_DH),
                             lambda b, alive_ref, fetch_ref: (l, fetch_ref[b], 0, 0))
    state_in_spec = pl.BlockSpec(state_blk, lambda b, alive_ref, fetch_ref: (fetch_ref[b], 0, 0, 0))
    rows_spec = pl.BlockSpec((tq, SB_W), lambda b, alive_ref, fetch_ref: (off_q + b, 0))
    return pl.pallas_call(
        functools.partial(_sb_tail_kernel, n_tail=n_tail),
        out_shape=out_shape,
        grid_spec=pltpu.PrefetchScalarGridSpec(
            num_scalar_prefetch=2,
            grid=(bsz,),
            in_specs=[
                pl.BlockSpec((tq, SB_W), lambda b, alive_ref, fetch_ref: (off_q + fetch_ref[b], 0)),
                tail_spec, tail_spec, state_in_spec, state_in_spec,
                rows_spec,
            ],
            out_specs=rows_spec,
            scratch_shapes=scratch,
        ),
        input_output_aliases={7: 0},
        compiler_params=_params("arbitrary"),
        name="stick_breaking_tail",
    )(alive.astype(jnp.int32), fetch.astype(jnp.int32), bq, past_k, past_v, acc, carry, hb)


def _pool_kernel(u_ref, buf_ref, w_ref, s_ref, *rest, tt, pos0):
    o_ref, ext_ref = rest[-2:]
    i = pl.program_id(1)

    @pl.when(i == 0)
    def _():
        ext_ref[0:POOL_HALO, :] = buf_ref[0]

    @pl.when(i > 0)
    def _():
        ext_ref[0:POOL_HALO, :] = ext_ref[tt:tt + POOL_HALO, :]

    ext_ref[POOL_HALO:POOL_HALO + tt, :] = u_ref[...]
    pos = pos0 + i * tt + lax.broadcasted_iota(jnp.int32, (tt, 1), 0)
    for gi, w in enumerate(POOL_WINDOWS):
        cs = slice(gi * POOL_G, (gi + 1) * POOL_G)
        cur = ext_ref[POOL_HALO:POOL_HALO + tt, cs]
        wsum = cur
        for j in range(1, w):
            wsum = wsum + ext_ref[POOL_HALO - j:POOL_HALO - j + tt, cs]
        cnt = jnp.minimum(pos + 1, w).astype(F32)
        d = (wsum / cnt - cur).astype(BF16)
        y = _dot(d, w_ref[gi]) * s_ref[:, cs]
        o_ref[:, cs] = y.astype(BF16)


def _pool(u, buf, w_pool, scale, l, prev, *, bsz, t, row_off, pos0):
    tt = min(256, t)
    nt = t // tt
    off = row_off // tt
    row_spec = pl.BlockSpec((tt, POOL_W), lambda b, i: (off + b * nt + i, 0))
    args = [u, buf, w_pool, scale.reshape(1, POOL_W)]
    prev_args, prev_specs, aliases = _alias_prev(prev, len(args))
    return pl.pallas_call(
        functools.partial(_pool_kernel, tt=tt, pos0=pos0),
        out_shape=jax.ShapeDtypeStruct((u.shape[0], POOL_W), BF16),
        grid=(bsz, nt),
        in_specs=[
            row_spec,
            pl.BlockSpec((1, POOL_HALO, POOL_W), lambda b, i: (b, 0, 0)),
            pl.BlockSpec((None, len(POOL_WINDOWS), POOL_G, POOL_G), lambda b, i: (l, 0, 0, 0)),
            pl.BlockSpec((1, POOL_W), lambda b, i: (0, 0)),
        ] + prev_specs,
        out_specs=row_spec,
        scratch_shapes=[pltpu.VMEM((POOL_HALO + tt, POOL_W), F32)],
        input_output_aliases=aliases,
        compiler_params=_params("parallel", "arbitrary"),
        name="pool_mix",
    )(*args, *prev_args)


X_SCALE = X_DH ** -0.5


def _cross_kernel(q_ref, k_ref, v_ref, *rest):
    o_ref = rest[-1]
    for h in range(X_HEADS):
        cs = slice(h * X_DH, (h + 1) * X_DH)
        k = k_ref[0, :, cs].astype(BF16)
        v = v_ref[0, :, cs].astype(BF16)
        s = lax.dot_general(q_ref[:, cs], k, _NT, preferred_element_type=F32) * X_SCALE
        e = jnp.exp(s - jnp.max(s, axis=-1, keepdims=True))
        p = e * (1.0 / jnp.sum(e, axis=-1, keepdims=True))
        o_ref[:, cs] = _dot(p.astype(BF16), v).astype(BF16)


def _cross_attend(qx, mk, mv, l, prev, *, bsz, t, row_off):
    tq = min(256, t)
    nq = t // tq
    off = row_off // tq
    row_spec = pl.BlockSpec((tq, X_W), lambda b, i: (off + b * nq + i, 0))
    mem_spec = pl.BlockSpec((None, 1, mk.shape[2], X_W), lambda b, i: (l, b, 0, 0))
    args = [qx, mk, mv]
    prev_args, prev_specs, aliases = _alias_prev(prev, len(args))
    return pl.pallas_call(
        _cross_kernel,
        out_shape=jax.ShapeDtypeStruct((qx.shape[0], X_W), BF16),
        grid=(bsz, nq),
        in_specs=[row_spec, mem_spec, mem_spec] + prev_specs,
        out_specs=row_spec,
        input_output_aliases=aliases,
        compiler_params=_params("parallel", "arbitrary"),
        name="cross_attend",
    )(*args, *prev_args)


def _merge_kernel(x_ref, wg0_ref, wg1_ref, wg2_ref, bg0_ref, bg1_ref, bg2_ref,
                  ha_ref, hb_ref, hc_ref, wa_ref, wb_ref, wc_ref, side_in_ref,
                  o_ref, side_out_ref, *, side_slabs):
    step = pl.program_id(0) * pl.num_programs(1) + pl.program_id(1)

    @pl.when(step < side_slabs)
    def _():
        side_out_ref[...] = side_in_ref[...].astype(BF16)

    x = x_ref[...]
    out = None
    for wg_ref, bg_ref, h_ref, wp_ref in ((wg0_ref, bg0_ref, ha_ref, wa_ref),
                                          (wg1_ref, bg1_ref, hb_ref, wb_ref),
                                          (wg2_ref, bg2_ref, hc_ref, wc_ref)):
        gate = jax.nn.sigmoid(lax.dot_general(x, wg_ref[...], _NT, preferred_element_type=F32)
                              + bg_ref[...])
        term = gate * _dot(h_ref[...], wp_ref[...])
        out = term if out is None else out + term
    o_ref[...] = out.astype(BF16)


MERGE_SIDE_ROWS = 16


def _merge(xb, wt_b, bg, ha, hb, hc, w_pa, w_pb, w_pc, side_w, l, *, bm=512, bn=256):
    m = xb.shape[0]
    _, sr, sc = side_w.shape
    side_slabs = sr // MERGE_SIDE_ROWS
    assert sr % MERGE_SIDE_ROWS == 0 and side_slabs <= (m // bm) * (D_MODEL // bn)
    nb = D_MODEL // bn
    gb = B_G // bn
    row = lambda width: pl.BlockSpec((bm, width), lambda i, j: (i, 0))
    colw = lambda kdim: pl.BlockSpec((None, kdim, bn), lambda i, j: (0, 0, j))
    gate_w = [pl.BlockSpec((None, bn, D_MODEL), lambda i, j, g=g: (0, gb + g * nb + j, 0))
              for g in range(N_BRANCH)]
    gate_b = [pl.BlockSpec((1, bn), lambda i, j, g=g: (0, g * nb + j)) for g in range(N_BRANCH)]
    slab = lambda i, j: jnp.minimum(i * nb + j, side_slabs - 1)
    return pl.pallas_call(
        functools.partial(_merge_kernel, side_slabs=side_slabs),
        out_shape=(jax.ShapeDtypeStruct((m, D_MODEL), BF16),
                   jax.ShapeDtypeStruct((1, sr, sc), BF16)),
        grid=(m // bm, nb),
        in_specs=[row(D_MODEL)] + gate_w + gate_b
                 + [row(MLSTM_W), row(SB_W), row(POOL_W), colw(MLSTM_W), colw(SB_W), colw(POOL_W),
                    pl.BlockSpec((None, MERGE_SIDE_ROWS, sc), lambda i, j: (l, slab(i, j), 0))],
        out_specs=(pl.BlockSpec((bm, bn), lambda i, j: (i, j)),
                   pl.BlockSpec((None, MERGE_SIDE_ROWS, sc), lambda i, j: (0, slab(i, j), 0))),
        compiler_params=_params("arbitrary", "arbitrary"),
        name="gated_merge",
    )(xb, wt_b, wt_b, wt_b, bg, bg, bg, ha, hb, hc, w_pa, w_pb, w_pc, side_w)


def _trunk_layer(x, xb, W, l, groups, kv_prev, bufs, last):
    b_in = W["b_in"][l]
    proj_a = functools.partial(_matmul, xb, W["wt_in"], l, wt=True, bn=512)
    group_rows = tuple(g["bsz"] * g["t"] for g in groups)

    k_scale = jnp.concatenate([jnp.ones((MLSTM_QK_W,), F32),
                               jnp.full((MLSTM_QK_W,), MLSTM_DK ** -0.5, F32),
                               jnp.ones((MLSTM_W,), F32)])
    qkv, wt_b = proj_a(n=OFF_AO, bias=b_in[:OFF_AO], scale=k_scale, out_dtype=BF16,
                       side_cast=[(W["wt_in"], l, OFF_BQ, IN_W - OFF_BQ, 256)], name="proj_qkv")
    proj_b = functools.partial(_matmul, xb, wt_b, 0, wt=True)
    whole = lambda name, slab: (W[name], l, 0, W[name].shape[1], slab)
    ao, w_out = proj_a(n=MLSTM_W, col_off=OFF_AO, bias=b_in[OFF_AO:OFF_AI], act="sigmoid",
                       side_cast=[whole("w_out", 128)], name="proj_ogate")
    b_if = jnp.pad(b_in[OFF_AI:OFF_BQ], (0, GATE_PAD - 2 * MLSTM_HEADS))
    gif, w_pa, w_pb, w_pc = proj_a(
        n=GATE_PAD, col_off=OFF_AI, bias=b_if, bn=GATE_PAD,
        side_cast=[whole("w_pa", 256), whole("w_pb", 128), whole("w_pc", 128)], name="proj_if")
    sec = lambda i: b_in[OFF_BQ + i * SB_W:OFF_BQ + (i + 1) * SB_W]
    bq, wq_x, wo_x = proj_b(n=SB_W, bias=sec(0), out_dtype=BF16,
                            side_cast=[whole("wq_x", 512), whole("wo_x", 128)], name="proj_sb_q")
    k8 = _proj_heads(xb, wt_b, l, sec(1), B_BK, group_rows, kv_prev[0], name="proj_sb_k")
    v8 = _proj_heads(xb, wt_b, l, sec(2), B_BV, group_rows, kv_prev[1], name="proj_sb_v")
    u = proj_b(n=POOL_W, col_off=B_CU, bias=sec(3), name="proj_pool_u")

    rows = xb.shape[0]
    if bufs is None:
        bufs = tuple(jnp.zeros((rows, w), BF16) for w in (MLSTM_W, SB_W, POOL_W, X_W))
    ha, hb, hc, ox = bufs
    states = []
    for gi, g in enumerate(groups):
        dims = dict(bsz=g["bsz"], t=g["t"], row_off=g["row_off"])
        ha, c_new, n_new, m_new = _mlstm(qkv, ao, gif, g["c"], g["n"], g["m"],
                                         W["mlstm_norm_g"][l], l, ha, **dims)
        hb = _stick_breaking(bq, k8[gi], v8[gi], g["sb_k"], g["sb_v"], l, hb, **dims)
        hc = _pool(u, g["pool_buf"][l], W["w_pool"], W["pool_scale"][l], l, hc,
                   pos0=g["pos0"], **dims)
        states.append((c_new, n_new, m_new))

    merged, w_up = _merge(xb, wt_b, b_in[OFF_G:].reshape(1, -1), ha, hb, hc,
                          w_pa, w_pb, w_pc, W["w_up"], l)
    y1 = _matmul(merged, w_out, 0, n=D_MODEL, res=x, name="w_out")
    ln1 = (W["ln1_g"][l], W["ln1_b"][l])
    mu1, rs1, xb = _layer_norm(y1, *ln1, emit_f32=False)

    qx = _matmul(xb, wq_x, 0, n=X_W, out_dtype=BF16, name="wq_x")
    for g in groups:
        ox = _cross_attend(qx, g["mk"], g["mv"], l, ox, bsz=g["bsz"], t=g["t"],
                           row_off=g["row_off"])
    y2 = _matmul(ox, wo_x, 0, n=D_MODEL, res=y1, res_ln=(mu1, rs1, *ln1), name="wo_x")
    ln2 = (W["ln2_g"][l], W["ln2_b"][l])
    mu2, rs2, xb = _layer_norm(y2, *ln2, emit_f32=False)

    hid, w_down = _matmul(xb, w_up, 0, n=D_FF, act="relu2", out_dtype=BF16,
                          side_cast=[whole("w_down", 128)], name="w_up")
    y = _matmul(hid, w_down, 0, n=D_MODEL, bk=D_MODEL, name="w_down")
    if last:
        x = tuple(_layer_norm(y2, W["ln3_g"][l], W["ln3_b"][l], res=y, x_ln=(mu2, rs2, *ln2),
                              in_off=g["row_off"], n_rows=g["bsz"] * g["t"], emit_bf16=False)
                  for g in groups)
        xb = None
    else:
        x, xb = _layer_norm(y2, W["ln3_g"][l], W["ln3_b"][l], res=y, x_ln=(mu2, rs2, *ln2))
    return x, xb, (k8, v8), u, states, (ha, hb, hc, ox)


def kernel(x_prompt, x_sample, cache_sb_k, cache_sb_v, state_mlstm_c, state_mlstm_n, state_mlstm_m, state_pool, cache_mem_k, cache_mem_v, mem_prompt, ln_in_g, ln_in_b, w_in, b_in, mlstm_norm_g, w_pool, pool_scale, w_pa, w_pb, w_pc, w_out, ln1_g, ln1_b, wq_x, wk_x, wv_x, wo_x, ln2_g, ln2_b, w_up, w_down, ln3_g, ln3_b):
    bf = lambda a: a.astype(BF16)
    wt_in = jnp.swapaxes(w_in, 1, 2)
    W = dict(wt_in=wt_in,
             b_in=b_in, mlstm_norm_g=mlstm_norm_g, w_pool=bf(w_pool),
             pool_scale=pool_scale, w_pa=w_pa, w_pb=w_pb, w_pc=w_pc, w_out=w_out,
             ln1_g=ln1_g, ln1_b=ln1_b, wq_x=wq_x, wo_x=wo_x, ln2_g=ln2_g, ln2_b=ln2_b,
             w_up=w_up, w_down=w_down, ln3_g=ln3_g, ln3_b=ln3_b)
    bp, tp, _ = x_prompt.shape
    bs, ts, _ = x_sample.shape
    rows_p, rows_s = bp * tp, bs * ts
    rows = rows_p + rows_s
    past = cache_sb_k.shape[2]
    mem_len = mem_prompt.shape[1]

    mem_b = mem_prompt.reshape(bp * mem_len, D_MODEL).astype(BF16)
    mem4 = lambda a, b: a.reshape(DEPTH, b, mem_len, X_W)
    mem_proj = lambda w, name: mem4(jnp.stack(
        [_matmul(mem_b, w, l, n=X_W, bn=512, name=name) for l in range(DEPTH)]), bp)
    mk_p, mv_p = mem_proj(wk_x, "mem_k"), mem_proj(wv_x, "mem_v")

    lane_m = lambda m: jnp.broadcast_to(m[..., None], m.shape + (LANES,))
    halo = lambda b: jnp.pad(b, ((0, 0), (0, 0), (POOL_HALO - POOL_BUF, 0), (0, 0)))
    heads_rows = lambda c: c.reshape(DEPTH, bs, past * SB_HEADS, SB_DH)
    groups = [
        dict(bsz=bp, t=tp, row_off=0, pos0=0,
             c=jnp.zeros((DEPTH, bp, MLSTM_HEADS, MLSTM_DK, MLSTM_DV), F32),
             n=jnp.zeros((DEPTH, bp, MLSTM_HEADS, MLSTM_DK), F32),
             m=jnp.zeros((DEPTH, bp, MLSTM_HEADS, LANES), F32),
             sb_k=None, sb_v=None,
             pool_buf=jnp.zeros((DEPTH, bp, POOL_HALO, POOL_W), F32),
             mk=mk_p, mv=mv_p),
        dict(bsz=bs, t=ts, row_off=rows_p, pos0=past,
             c=state_mlstm_c, n=state_mlstm_n, m=lane_m(state_mlstm_m),
             sb_k=heads_rows(cache_sb_k), sb_v=heads_rows(cache_sb_v),
             pool_buf=halo(state_pool),
             mk=mem4(cache_mem_k, bs), mv=mem4(cache_mem_v, bs)),
    ]

    xs = None
    for g, xin in zip(groups, (x_prompt, x_sample)):
        xs = _layer_norm(xin.reshape(g["bsz"] * g["t"], D_MODEL), ln_in_g, ln_in_b,
                         out_off=g["row_off"], out_rows=rows, prev=xs)
    x, xb = xs

    kv = tuple(tuple(jnp.zeros((DEPTH, g["bsz"] * g["t"] * SB_HEADS, SB_DH), F32) for g in groups)
               for _ in range(2))
    per_layer = []
    bufs = None
    for l in range(DEPTH):
        x, xb, kv, u, states, bufs = _trunk_layer(x, xb, W, l, groups, kv, bufs,
                                                  last=l == DEPTH - 1)
        per_layer.append((u, states))

    def group_out(gi, g):
        bsz, t, r0 = g["bsz"], g["t"], g["row_off"]
        sbk = kv[0][gi].reshape(DEPTH, bsz, t, SB_HEADS, SB_DH)
        sbv = kv[1][gi].reshape(DEPTH, bsz, t, SB_HEADS, SB_DH)
        c = jnp.stack([st[gi][0] for _, st in per_layer])
        n = jnp.stack([st[gi][1] for _, st in per_layer])
        m = jnp.stack([st[gi][2][..., 0] for _, st in per_layer])
        pool = jnp.stack([u[r0:r0 + bsz * t].reshape(bsz, t, POOL_W)[:, t - POOL_BUF:, :]
                          for u, _ in per_layer])
        return sbk, sbv, c, n, m, pool

    out_p = group_out(0, groups[0])
    out_s = group_out(1, groups[1])
    y_prompt = x[0].reshape(bp, tp, D_MODEL)
    y_sample = x[1].reshape(bs, ts, D_MODEL)
    mem5 = lambda a: a.reshape(DEPTH, bp, mem_len, X_HEADS, X_DH)
    return (y_prompt, y_sample) + out_p + (mem5(mk_p), mem5(mv_p)) + out_s
```

```python
import functools

import jax
import jax.numpy as jnp
from jax import lax
from jax.experimental import pallas as pl
from jax.experimental.pallas import tpu as pltpu

F32 = jnp.float32
BF16 = jnp.bfloat16

D_MODEL = 4096
DEPTH = 2
MLSTM_CHUNK = 256
MLSTM_HEADS = 8
MLSTM_W = D_MODEL // 2
MLSTM_DV = MLSTM_W // MLSTM_HEADS
MLSTM_DK = MLSTM_DV // 2
MLSTM_QK_W = MLSTM_HEADS * MLSTM_DK
SB_HEADS = 8
SB_W = D_MODEL // 4
SB_DH = SB_W // SB_HEADS
POOL_WINDOWS = (2, 4, 8, 16)
POOL_W = D_MODEL // 4
POOL_G = POOL_W // len(POOL_WINDOWS)
POOL_BUF = max(POOL_WINDOWS) - 1
POOL_HALO = POOL_BUF + 1
X_HEADS = 4
X_W = D_MODEL // 4
X_DH = X_W // X_HEADS
D_FF = 4 * D_MODEL
N_BRANCH = 3
ALPHA = (2 * DEPTH) ** 0.25
LN_EPS = 1e-5
HEAD_NORM_EPS = 1e-6

OFF_AO = 2 * MLSTM_QK_W + MLSTM_W
OFF_AI = OFF_AO + MLSTM_W
OFF_BQ = OFF_AI + 2 * MLSTM_HEADS
OFF_G = OFF_BQ + 3 * SB_W + POOL_W
IN_W = OFF_G + N_BRANCH * D_MODEL

V7X_VMEM_LIMIT_BYTES = 56 * 1024 * 1024
LANES = 128
SUBLANES = 8
SB_TQ = 256
GATE_PAD = LANES

B_BK = SB_W
B_BV = 2 * SB_W
B_CU = 3 * SB_W
B_G = 3 * SB_W + POOL_W

_NT = (((1,), (1,)), ((), ()))
_TN = (((0,), (0,)), ((), ()))


def _params(*sem):
    return pltpu.CompilerParams(dimension_semantics=sem,
                                vmem_limit_bytes=V7X_VMEM_LIMIT_BYTES)


def _dot(a, b):
    return jnp.dot(a, b, preferred_element_type=F32)


def _split(x, terms):
    out = []
    for _ in range(terms - 1):
        h = x.astype(BF16)
        out.append(h)
        x = x - h.astype(F32)
    out.append(x.astype(BF16))
    return out


def _apply_ln(x, mu_ref, rs_ref, g_ref, b_ref):
    return (x - mu_ref[:, 0:1]) * rs_ref[:, 0:1] * g_ref[...] + b_ref[...]


def _alias_prev(prev, n_in):
    if prev is None:
        return [], [], {}
    prev = list(prev) if isinstance(prev, (tuple, list)) else [prev]
    specs = [pl.BlockSpec(memory_space=pl.ANY)] * len(prev)
    return prev, specs, {n_in + i: i for i in range(len(prev))}


def _mm_kernel(*refs, act, has_bias, has_scale, has_res, res_ln, nk, side_slabs, wt):
    it = iter(refs)
    x_ref, w_ref = next(it), next(it)
    b_ref = next(it) if has_bias else None
    s_ref = next(it) if has_scale else None
    r_ref = next(it) if has_res else None
    ln_refs = [next(it) for _ in range(4)] if res_ln else None
    side_in_refs = [next(it) for _ in side_slabs]
    o_ref = next(it)
    side_out_refs = [next(it) for _ in side_slabs]
    acc_ref = next(it) if nk > 1 else None

    if side_slabs:
        step = pl.program_id(0) * pl.num_programs(1) + pl.program_id(1)
        for n_slabs, in_ref, out_ref in zip(side_slabs, side_in_refs, side_out_refs):
            @pl.when(step < n_slabs)
            def _(in_ref=in_ref, out_ref=out_ref):
                out_ref[...] = in_ref[0].astype(BF16)

    def epilogue(y):
        if has_bias:
            y = y + b_ref[...]
        if has_scale:
            y = y * s_ref[...]
        if act == "sigmoid":
            y = jax.nn.sigmoid(y)
        elif act == "relu2":
            y = jnp.square(jnp.maximum(y, 0.0))
        if has_res:
            r = r_ref[...]
            if res_ln:
                r = _apply_ln(r, *ln_refs)
            y = ALPHA * r + y
        o_ref[...] = y.astype(o_ref.dtype)

    w = w_ref[...].astype(BF16)
    part = lax.dot_general(x_ref[...], w, _NT if wt else (((1,), (0,)), ((), ())),
                           preferred_element_type=F32)
    if nk == 1:
        epilogue(part)
    else:
        k = pl.program_id(2)

        @pl.when(k == 0)
        def _():
            acc_ref[...] = part

        @pl.when(k > 0)
        def _():
            acc_ref[...] += part

        @pl.when(k == nk - 1)
        def _():
            epilogue(acc_ref[...])


def _matmul(x, w, l, *, n, col_off=0, bias=None, scale=None, res=None, res_ln=None, act="none",
            out_dtype=F32, bm=1024, bn=1024, bk=None, side_cast=(), wt=False, name="matmul"):
    m, kdim = x.shape
    bm, bn = min(bm, m), min(bn, n)
    bk = kdim if bk is None else bk
    assert m % bm == 0 and n % bn == 0 and kdim % bk == 0 and col_off % bn == 0
    nk = kdim // bk
    cb = col_off // bn
    nj = n // bn
    w_spec = (pl.BlockSpec((None, bn, bk), lambda i, j, k: (l, cb + j, k)) if wt
              else pl.BlockSpec((None, bk, bn), lambda i, j, k: (l, k, cb + j)))
    in_specs = [pl.BlockSpec((bm, bk), lambda i, j, k: (i, k)), w_spec]
    args = [x, w]
    for v in (bias, scale):
        if v is not None:
            in_specs.append(pl.BlockSpec((1, bn), lambda i, j, k: (0, j)))
            args.append(v.reshape(1, n).astype(F32))
    if res is not None:
        in_specs.append(pl.BlockSpec((bm, bn), lambda i, j, k: (i, j)))
        args.append(res)
    if res_ln is not None:
        mu, rs, ln_g, ln_b = res_ln
        in_specs += [pl.BlockSpec((bm, LANES), lambda i, j, k: (i, 0))] * 2
        in_specs += [pl.BlockSpec((1, bn), lambda i, j, k: (0, j))] * 2
        args += [mu, rs, ln_g.reshape(1, n), ln_b.reshape(1, n)]
    out_shape = [jax.ShapeDtypeStruct((m, n), out_dtype)]
    out_specs = [pl.BlockSpec((bm, bn), lambda i, j, k: (i, j))]
    side_slabs = []
    for side_w, side_l, row_off, sr, slab_rows in side_cast:
        sc = side_w.shape[2]
        n_slabs = sr // slab_rows
        assert nk == 1 and sr % slab_rows == 0 and n_slabs <= (m // bm) * nj
        slab = lambda i, j, k, n_slabs=n_slabs: jnp.minimum(i * nj + j, n_slabs - 1)
        in_specs.append(pl.BlockSpec(
            (pl.Element(1), pl.Element(slab_rows), pl.Element(sc)),
            lambda i, j, k, slab=slab, side_l=side_l, row_off=row_off, slab_rows=slab_rows: (
                side_l, pl.multiple_of(row_off + slab(i, j, k) * slab_rows, SUBLANES), 0)))
        args.append(side_w)
        out_shape.append(jax.ShapeDtypeStruct((1, sr, sc), BF16))
        out_specs.append(pl.BlockSpec((None, slab_rows, sc),
                                      lambda i, j, k, slab=slab: (0, slab(i, j, k), 0)))
        side_slabs.append(n_slabs)
    kern = functools.partial(_mm_kernel, act=act, has_bias=bias is not None,
                             has_scale=scale is not None, has_res=res is not None,
                             res_ln=res_ln is not None, nk=nk,
                             side_slabs=tuple(side_slabs), wt=wt)
    out = pl.pallas_call(
        kern,
        out_shape=tuple(out_shape),
        grid=(m // bm, nj, nk),
        in_specs=in_specs,
        out_specs=tuple(out_specs),
        scratch_shapes=[pltpu.VMEM((bm, bn), F32)] if nk > 1 else [],
        compiler_params=(_params("arbitrary", "arbitrary", "arbitrary") if side_slabs
                         else _params("parallel", "parallel", "arbitrary")),
        name=name,
    )(*args)
    return out if side_slabs else out[0]


def _proj_heads_kernel(x_ref, w_ref, b_ref, *rest, n_first):
    first_ref, second_ref = rest[-2:]
    y = lax.dot_general(x_ref[...], w_ref[...], _NT, preferred_element_type=F32) + b_ref[...]
    rows = y.shape[0]

    def store(o_ref):
        for h in range(SUBLANES):
            o_ref[pl.ds(h, rows, stride=SUBLANES), :] = y[:, h * LANES:(h + 1) * LANES]

    i = pl.program_id(0)

    @pl.when(i < n_first)
    def _():
        store(first_ref)

    @pl.when(i >= n_first)
    def _():
        store(second_ref)


def _proj_heads(x, w, l, bias, col_off, group_rows, prev, *, bm=1024, name="proj_heads"):
    m, kdim = x.shape
    n = SUBLANES * LANES
    rows_a, rows_b = group_rows
    assert rows_a % bm == 0 and rows_b % bm == 0 and rows_a + rows_b == m and col_off % n == 0
    n_first = rows_a // bm
    cb = col_off // n
    args = [x, w, bias.reshape(1, n)]
    prev_args, prev_specs, aliases = _alias_prev(prev, len(args))
    blk = (None, bm * SUBLANES, LANES)
    return pl.pallas_call(
        functools.partial(_proj_heads_kernel, n_first=n_first),
        out_shape=tuple(jax.ShapeDtypeStruct((DEPTH, r * SUBLANES, LANES), F32) for r in group_rows),
        grid=(m // bm,),
        in_specs=[pl.BlockSpec((bm, kdim), lambda i: (i, 0)),
                  pl.BlockSpec((None, n, kdim), lambda i: (0, cb, 0)),
                  pl.BlockSpec((1, n), lambda i: (0, 0))] + prev_specs,
        out_specs=(pl.BlockSpec(blk, lambda i: (l, jnp.minimum(i, n_first - 1), 0)),
                   pl.BlockSpec(blk, lambda i: (l, jnp.maximum(i - n_first, 0), 0))),
        input_output_aliases=aliases,
        compiler_params=_params("arbitrary"),
        name=name,
    )(*args, *prev_args)


def _ln_kernel(*refs, x_ln, has_res, n_prev, emit_f32, emit_bf16, n_blocks):
    it = iter(refs)
    x_ref = next(it)
    x_ln_refs = [next(it) for _ in range(4)] if x_ln else None
    r_ref = next(it) if has_res else None
    g_ref, b_ref = next(it), next(it)
    for _ in range(n_prev):
        next(it)
    x = x_ref[...]
    if x_ln:
        x = _apply_ln(x, *x_ln_refs)
    if has_res:
        x = ALPHA * x + r_ref[...]
    mu = jnp.mean(x, axis=-1, keepdims=True)
    xc = x - mu
    var = jnp.mean(xc * xc, axis=-1, keepdims=True)
    rs = lax.rsqrt(var + LN_EPS)
    y = xc * rs * g_ref[...] + b_ref[...]
    if n_blocks is not None:
        y = jnp.where(pl.program_id(0) < n_blocks, y, 0.0)
    if emit_f32:
        next(it)[...] = y
    else:
        mu_ref, rs_ref = next(it), next(it)
        mu_ref[...] = jnp.broadcast_to(mu, mu_ref.shape)
        rs_ref[...] = jnp.broadcast_to(rs, rs_ref.shape)
    if emit_bf16:
        next(it)[...] = y.astype(BF16)


def _layer_norm(x, g, b, res=None, *, x_ln=None, in_off=0, n_rows=None, out_off=0, out_rows=None,
                prev=None, emit_f32=True, emit_bf16=True, rows=256):
    d = x.shape[1]
    n_rows = x.shape[0] - in_off if n_rows is None else n_rows
    out_rows = n_rows if out_rows is None else out_rows
    ib, ob = in_off // rows, out_off // rows
    n_blocks = n_rows // rows
    fill = prev is None and out_rows > n_rows
    assert not fill or out_off == 0
    in_spec = pl.BlockSpec((rows, d), lambda i: (ib + jnp.minimum(i, n_blocks - 1), 0))
    stat_in_spec = pl.BlockSpec((rows, LANES), lambda i: (ib + jnp.minimum(i, n_blocks - 1), 0))
    out_spec = pl.BlockSpec((rows, d), lambda i: (ob + i, 0))
    stat_spec = pl.BlockSpec((rows, LANES), lambda i: (ob + i, 0))
    vec_spec = pl.BlockSpec((1, d), lambda i: (0, 0))
    args, in_specs = [x], [in_spec]
    if x_ln is not None:
        args += [x_ln[0], x_ln[1], x_ln[2].reshape(1, d), x_ln[3].reshape(1, d)]
        in_specs += [stat_in_spec, stat_in_spec, vec_spec, vec_spec]
    if res is not None:
        args.append(res)
        in_specs.append(in_spec)
    args += [g.reshape(1, d), b.reshape(1, d)]
    in_specs += [vec_spec, vec_spec]
    prev_args, prev_specs, aliases = _alias_prev(prev, len(args))
    if emit_f32:
        out_shape, out_specs = [jax.ShapeDtypeStruct((out_rows, d), F32)], [out_spec]
    else:
        out_shape = [jax.ShapeDtypeStruct((out_rows, LANES), F32)] * 2
        out_specs = [stat_spec, stat_spec]
    if emit_bf16:
        out_shape.append(jax.ShapeDtypeStruct((out_rows, d), BF16))
        out_specs.append(out_spec)
    out = pl.pallas_call(
        functools.partial(_ln_kernel, x_ln=x_ln is not None, has_res=res is not None,
                          n_prev=len(prev_args), emit_f32=emit_f32, emit_bf16=emit_bf16,
                          n_blocks=n_blocks if fill else None),
        out_shape=tuple(out_shape),
        grid=(out_rows // rows if fill else n_blocks,),
        in_specs=in_specs + prev_specs,
        out_specs=tuple(out_specs),
        input_output_aliases=aliases,
        compiler_params=_params("parallel"),
        name="layer_norm",
    )(*args, *prev_args)
    return out if len(out) > 1 else out[0]


def _mlstm_kernel(q_ref, k_ref, v_ref, ao_ref, gif_ref, c0_ref, n0_ref, m0_ref, g_ref, *rest,
                  n_seq):
    ha_ref, c_ref, n_ref, m_ref = rest[-4:]
    L = q_ref.shape[0] // n_seq

    @pl.when(pl.program_id(1) == 0)
    def _():
        c_ref[...] = c0_ref[...]
        n_ref[...] = n0_ref[...]
        m_ref[...] = m0_ref[...]

    row = lax.broadcasted_iota(jnp.int32, (L, L), 0)
    col = lax.broadcasted_iota(jnp.int32, (L, L), 1)
    causal = col <= row
    tril = jnp.where(causal, 1.0, 0.0).astype(BF16)
    sel_r = lax.broadcasted_iota(jnp.int32, (2 * MLSTM_HEADS, GATE_PAD), 0)
    sel_c = lax.broadcasted_iota(jnp.int32, (2 * MLSTM_HEADS, GATE_PAD), 1)
    sel = jnp.where(sel_r == sel_c, 1.0, 0.0).astype(BF16)

    def rows_of(x):
        return sum(lax.dot_general(sel, t, _NT, preferred_element_type=F32) for t in _split(x, 3))

    for s_i in range(n_seq):
        rs = slice(s_i * L, (s_i + 1) * L)
        gif = gif_ref[rs, :]
        logf = -(jnp.maximum(-gif, 0.0) + jnp.log1p(jnp.exp(-jnp.abs(gif))))
        b_all = sum(_dot(tril, t) for t in _split(logf, 3))
        i_rows = rows_of(gif)
        b_rows = rows_of(b_all)

        for h in range(MLSTM_HEADS):
            qs = slice(h * MLSTM_DK, (h + 1) * MLSTM_DK)
            vs = slice(h * MLSTM_DV, (h + 1) * MLSTM_DV)
            q = q_ref[rs, qs]
            k = k_ref[rs, qs]
            v = v_ref[rs, vs]
            i_col = gif[:, h:h + 1]
            b_col = b_all[:, MLSTM_HEADS + h:MLSTM_HEADS + h + 1]
            i_row = i_rows[h:h + 1, :]
            b_row = b_rows[MLSTM_HEADS + h:MLSTM_HEADS + h + 1, :]
            c_old = c_ref[s_i, h]
            n_old = n_ref[s_i, h:h + 1, :]
            m_old = m_ref[s_i, h:h + 1, 0:1]

            dmat = jnp.where(causal, b_col - b_row + i_row, -jnp.inf)
            m_inter = b_col + m_old
            m_t = jnp.maximum(m_inter, jnp.max(dmat, axis=-1, keepdims=True))
            s = lax.dot_general(q, k, _NT, preferred_element_type=F32) * jnp.exp(dmat - m_t)
            w_inter = jnp.exp(m_inter - m_t)
            kf = k.astype(F32)
            num = _dot(s.astype(BF16), v) + w_inter * _dot(q, c_old.astype(BF16))
            qn = jnp.sum(q.astype(F32) * n_old, axis=-1, keepdims=True)
            den = jnp.sum(s, axis=-1, keepdims=True) + w_inter * qn
            hh = num * (1.0 / jnp.maximum(jnp.abs(den), jnp.exp(-m_t)))

            m_new = m_t[L - 1:L, :]
            b_last = b_col[L - 1:L, :]
            g_col = jnp.exp(b_last - b_col + i_col - m_new)
            decay = jnp.exp(b_last + m_old - m_new)
            kg = kf * g_col
            c_ref[s_i, h] = decay * c_old + lax.dot_general(kg.astype(BF16), v, _TN,
                                                            preferred_element_type=F32)
            n_ref[s_i, h:h + 1, :] = decay * n_old + jnp.sum(kg, axis=0, keepdims=True)
            m_ref[s_i, h:h + 1, :] = jnp.broadcast_to(m_new, (1, LANES))

            mu = jnp.mean(hh, axis=-1, keepdims=True)
            hc = hh - mu
            var = jnp.mean(hc * hc, axis=-1, keepdims=True)
            hn = hc * lax.rsqrt(var + HEAD_NORM_EPS) * g_ref[:, vs]
            ha_ref[rs, vs] = (ao_ref[rs, vs] * hn).astype(BF16)


def _mlstm(qkv, ao, gif, c0, n0, m0, norm_g, l, prev, *, bsz, t, row_off):
    chunk = min(MLSTM_CHUNK, t)
    nc = t // chunk
    n_seq = MLSTM_CHUNK // chunk if nc == 1 and bsz % (MLSTM_CHUNK // chunk) == 0 else 1
    blk = n_seq * chunk
    off = row_off // blk
    rows = lambda b, c: off + b * nc + c
    state4 = pl.BlockSpec((n_seq, MLSTM_HEADS, MLSTM_DK, MLSTM_DV), lambda b, c: (b, 0, 0, 0))
    state3 = pl.BlockSpec((n_seq, MLSTM_HEADS, LANES), lambda b, c: (b, 0, 0))
    init4 = pl.BlockSpec((None, n_seq, MLSTM_HEADS, MLSTM_DK, MLSTM_DV),
                         lambda b, c: (l, b, 0, 0, 0))
    init3 = pl.BlockSpec((None, n_seq, MLSTM_HEADS, LANES), lambda b, c: (l, b, 0, 0))
    args = [qkv, qkv, qkv, ao, gif, c0, n0, m0, norm_g.reshape(1, MLSTM_W)]
    prev_args, prev_specs, aliases = _alias_prev(prev, len(args))
    return pl.pallas_call(
        functools.partial(_mlstm_kernel, n_seq=n_seq),
        out_shape=(jax.ShapeDtypeStruct((qkv.shape[0], MLSTM_W), BF16),
                   jax.ShapeDtypeStruct(c0.shape[1:], F32),
                   jax.ShapeDtypeStruct(n0.shape[1:], F32),
                   jax.ShapeDtypeStruct(m0.shape[1:], F32)),
        grid=(bsz // n_seq, nc),
        in_specs=[
            pl.BlockSpec((blk, MLSTM_QK_W), lambda b, c: (rows(b, c), 0)),
            pl.BlockSpec((blk, MLSTM_QK_W), lambda b, c: (rows(b, c), 1)),
            pl.BlockSpec((blk, MLSTM_W), lambda b, c: (rows(b, c), 1)),
            pl.BlockSpec((blk, MLSTM_W), lambda b, c: (rows(b, c), 0)),
            pl.BlockSpec((blk, GATE_PAD), lambda b, c: (rows(b, c), 0)),
            init4, init3, init3,
            pl.BlockSpec((1, MLSTM_W), lambda b, c: (0, 0)),
        ] + prev_specs,
        out_specs=(pl.BlockSpec((blk, MLSTM_W), lambda b, c: (rows(b, c), 0)),
                   state4, state3, state3),
        input_output_aliases=aliases,
        compiler_params=_params("parallel", "arbitrary"),
        name="mlstm",
    )(*args, *prev_args)


SB_SCALE = SB_DH ** -0.5
SB_DEAD_CARRY = -110.0


def _lower_tri(n):
    return jnp.where(lax.broadcasted_iota(jnp.int32, (n, n), 0)
                     >= lax.broadcasted_iota(jnp.int32, (n, n), 1), 1.0, 0.0).astype(BF16)


def _sb_group(q_ref, load_k, load_v, heads, acc_ref, carry_ref, lower, *, diag):
    tq = q_ref.shape[0]
    tk = lower.shape[0]
    if diag:
        mask = (lax.broadcasted_iota(jnp.int32, (tq, tk), 1)
                < lax.broadcasted_iota(jnp.int32, (tq, tk), 0))
    zs, lks = [], []
    for h in heads:
        z = lax.dot_general(q_ref[:, h * SB_DH:(h + 1) * SB_DH], load_k(h), _NT,
                            preferred_element_type=F32) * SB_SCALE
        log_keep = -(jnp.maximum(z, 0.0) + jnp.log(1.0 + jnp.exp(-jnp.abs(z))))
        if diag:
            log_keep = jnp.where(mask, log_keep, 0.0)
        zs.append(z)
        lks.append(log_keep)
    stacked = jnp.concatenate(lks, axis=0)
    rc_all = sum(_dot(t, lower) for t in _split(stacked, 2))
    for i, h in enumerate(heads):
        rc = rc_all[i * tq:(i + 1) * tq]
        carry = carry_ref[h]
        attn = jnp.exp(zs[i] + rc + carry)
        if diag:
            attn = jnp.where(mask, attn, 0.0)
        acc_ref[h] += _dot(attn.astype(BF16), load_v(h))
        carry_ref[h] = carry + rc[:, 0:1]


def _sb_block(q_ref, k_ref, v_ref, base, tk, acc_ref, carry_ref, lower, *, group, diag):
    def loader(ref):
        return lambda h: ref[pl.ds(base + h, tk, stride=SB_HEADS), :].astype(BF16)
    for g in range(0, SB_HEADS, group):
        _sb_group(q_ref, loader(k_ref), loader(v_ref), range(g, g + group),
                  acc_ref, carry_ref, lower, diag=diag)


def _sb_alive(carry_ref):
    return jnp.max(carry_ref[...]) > SB_DEAD_CARRY


def _sb_store(o_ref, acc_ref):
    for h in range(SB_HEADS):
        o_ref[:, h * SB_DH:(h + 1) * SB_DH] = acc_ref[h].astype(BF16)


def _sb_prompt_kernel(q_ref, k_ref, v_ref, *rest, tq, group):
    o_ref, acc_ref, carry_ref = rest[-3:]
    qi = pl.program_id(1)
    acc_ref[...] = jnp.zeros_like(acc_ref)
    carry_ref[...] = jnp.zeros_like(carry_ref)
    lower = _lower_tri(tq)
    blk_rows = tq * SB_HEADS

    _sb_block(q_ref, k_ref, v_ref, pl.multiple_of(qi * blk_rows, blk_rows), tq,
              acc_ref, carry_ref, lower, group=group, diag=True)

    def more(state):
        j, alive = state
        return jnp.logical_and(j < qi, alive)

    def body(state):
        j, _ = state
        base = pl.multiple_of((qi - 1 - j) * blk_rows, blk_rows)
        _sb_block(q_ref, k_ref, v_ref, base, tq, acc_ref, carry_ref, lower,
                  group=group, diag=False)
        return j + 1, _sb_alive(carry_ref)

    lax.while_loop(more, body, (jnp.int32(0), _sb_alive(carry_ref)))
    _sb_store(o_ref, acc_ref)


def _sb_head_kernel(q_ref, k_ref, v_ref, kp_ref, vp_ref, *rest, tq):
    o_ref, acc_out_ref, carry_out_ref, acc_ref, carry_ref = rest[-5:]
    s = pl.program_id(1)

    @pl.when(s == 0)
    def _():
        acc_ref[...] = jnp.zeros_like(acc_ref)
        carry_ref[...] = jnp.zeros_like(carry_ref)
        _sb_block(q_ref, k_ref, v_ref, 0, tq, acc_ref, carry_ref, _lower_tri(tq),
                  group=SB_HEADS, diag=True)

    @pl.when(s == 1)
    def _():
        @pl.when(_sb_alive(carry_ref))
        def _():
            _sb_block(q_ref, kp_ref, vp_ref, 0, SB_TQ, acc_ref, carry_ref, _lower_tri(SB_TQ),
                      group=SB_HEADS, diag=False)

        _sb_store(o_ref, acc_ref)
        acc_out_ref[0] = acc_ref[...]
        carry_out_ref[0] = jnp.broadcast_to(carry_ref[...], carry_out_ref.shape[1:])


def _sb_tail_kernel(alive_ref, fetch_ref, q_ref, kp_ref, vp_ref, acc_in_ref, carry_in_ref, hb_ref,
                    o_ref, acc_ref, carry_ref, *, n_tail):
    del fetch_ref
    b = pl.program_id(0)

    @pl.when(alive_ref[b] == 0)
    def _():
        o_ref[...] = hb_ref[...]

    @pl.when(alive_ref[b] != 0)
    def _():
        acc_ref[...] = acc_in_ref[0]
        carry_ref[...] = carry_in_ref[0][:, :, 0:1]
        lower = _lower_tri(SB_TQ)
        blk_rows = SB_TQ * SB_HEADS

        def body(j, c):
            @pl.when(_sb_alive(carry_ref))
            def _():
                base = pl.multiple_of((n_tail - 1 - j) * blk_rows, blk_rows)
                _sb_block(q_ref, kp_ref, vp_ref, base, SB_TQ, acc_ref, carry_ref, lower,
                          group=SB_HEADS, diag=False)
            return c

        lax.fori_loop(0, n_tail, body, 0)
        _sb_store(o_ref, acc_ref)


def _stick_breaking(bq, k8, v8, past_k, past_v, l, prev, *, bsz, t, row_off):
    tq = min(SB_TQ, t)
    nq = t // tq
    off_q = row_off // tq
    scratch = [pltpu.VMEM((SB_HEADS, tq, SB_DH), F32), pltpu.VMEM((SB_HEADS, tq, 1), F32)]
    out_shape = jax.ShapeDtypeStruct((bq.shape[0], SB_W), BF16)
    q_spec = pl.BlockSpec((tq, SB_W), lambda b, i: (off_q + b * nq + i, 0))
    if past_k is None:
        seq_spec = pl.BlockSpec((None, t * SB_HEADS, SB_DH), lambda b, i: (l, b, 0))
        args = [bq, k8, v8]
        prev_args, prev_specs, aliases = _alias_prev(prev, len(args))
        return pl.pallas_call(
            functools.partial(_sb_prompt_kernel, tq=tq, group=SB_HEADS // 2),
            out_shape=out_shape,
            grid=(bsz, nq),
            in_specs=[q_spec, seq_spec, seq_spec] + prev_specs,
            out_specs=q_spec,
            scratch_shapes=scratch,
            input_output_aliases=aliases,
            compiler_params=_params("parallel", "arbitrary"),
            name="stick_breaking_prompt",
        )(*args, *prev_args)
    assert nq == 1
    blk = SB_TQ * SB_HEADS
    n_past = past_k.shape[2] // blk
    n_tail = n_past - 1
    state_shape = jax.ShapeDtypeStruct((bsz, SB_HEADS, tq, SB_DH), F32)
    state_blk = (1, SB_HEADS, tq, SB_DH)
    q_spec = pl.BlockSpec((tq, SB_W), lambda b, s: (off_q + b, 0))
    new_spec = pl.BlockSpec((None, t * SB_HEADS, SB_DH), lambda b, s: (l, b, 0))
    near_spec = pl.BlockSpec((None, None, blk, SB_DH), lambda b, s: (l, b, n_tail, 0))
    state_spec = pl.BlockSpec(state_blk, lambda b, s: (b, 0, 0, 0))
    args = [bq, k8, v8, past_k, past_v]
    prev_args, prev_specs, aliases = _alias_prev(prev, len(args))
    hb, acc, carry = pl.pallas_call(
        functools.partial(_sb_head_kernel, tq=tq),
        out_shape=(out_shape, state_shape, state_shape),
        grid=(bsz, 2),
        in_specs=[q_spec, new_spec, new_spec, near_spec, near_spec] + prev_specs,
        out_specs=(q_spec, state_spec, state_spec),
        scratch_shapes=scratch,
        input_output_aliases=aliases,
        compiler_params=_params("parallel", "arbitrary"),
        name="stick_breaking_sample",
    )(*args, *prev_args)

    alive = jnp.max(carry, axis=(1, 2, 3)) > SB_DEAD_CARRY
    fetch = jnp.maximum(lax.cummax(jnp.where(alive, jnp.arange(bsz), -1), axis=0), 0)
    tail_spec = pl.BlockSpec((None, None, n_tail * blk, SB_DH),
                             lambda b, alive_ref, fetch_ref: (l, fetch_ref[b], 0, 0))
    state_in_spec = pl.BlockSpec(state_blk, lambda b, alive_ref, fetch_ref: (fetch_ref[b], 0, 0, 0))
    rows_spec = pl.BlockSpec((tq, SB_W), lambda b, alive_ref, fetch_ref: (off_q + b, 0))
    return pl.pallas_call(
        functools.partial(_sb_tail_kernel, n_tail=n_tail),
        out_shape=out_shape,
        grid_spec=pltpu.PrefetchScalarGridSpec(
            num_scalar_prefetch=2,
            grid=(bsz,),
            in_specs=[
                pl.BlockSpec((tq, SB_W), lambda b, alive_ref, fetch_ref: (off_q + fetch_ref[b], 0)),
                tail_spec, tail_spec, state_in_spec, state_in_spec,
                rows_spec,
            ],
            out_specs=rows_spec,
            scratch_shapes=scratch,
        ),
        input_output_aliases={7: 0},
        compiler_params=_params("arbitrary"),
        name="stick_breaking_tail",
    )(alive.astype(jnp.int32), fetch.astype(jnp.int32), bq, past_k, past_v, acc, carry, hb)


def _pool_kernel(u_ref, buf_ref, w_ref, s_ref, *rest, tt, pos0):
    o_ref, ext_ref = rest[-2:]
    i = pl.program_id(1)

    @pl.when(i == 0)
    def _():
        ext_ref[0:POOL_HALO, :] = buf_ref[0]

    @pl.when(i > 0)
    def _():
        ext_ref[0:POOL_HALO, :] = ext_ref[tt:tt + POOL_HALO, :]

    ext_ref[POOL_HALO:POOL_HALO + tt, :] = u_ref[...]
    pos = pos0 + i * tt + lax.broadcasted_iota(jnp.int32, (tt, 1), 0)
    for gi, w in enumerate(POOL_WINDOWS):
        cs = slice(gi * POOL_G, (gi + 1) * POOL_G)
        cur = ext_ref[POOL_HALO:POOL_HALO + tt, cs]
        wsum = cur
        for j in range(1, w):
            wsum = wsum + ext_ref[POOL_HALO - j:POOL_HALO - j + tt, cs]
        cnt = jnp.minimum(pos + 1, w).astype(F32)
        d = (wsum / cnt - cur).astype(BF16)
        y = _dot(d, w_ref[gi]) * s_ref[:, cs]
        o_ref[:, cs] = y.astype(BF16)


def _pool(u, buf, w_pool, scale, l, prev, *, bsz, t, row_off, pos0):
    tt = min(256, t)
    nt = t // tt
    off = row_off // tt
    row_spec = pl.BlockSpec((tt, POOL_W), lambda b, i: (off + b * nt + i, 0))
    args = [u, buf, w_pool, scale.reshape(1, POOL_W)]
    prev_args, prev_specs, aliases = _alias_prev(prev, len(args))
    return pl.pallas_call(
        functools.partial(_pool_kernel, tt=tt, pos0=pos0),
        out_shape=jax.ShapeDtypeStruct((u.shape[0], POOL_W), BF16),
        grid=(bsz, nt),
        in_specs=[
            row_spec,
            pl.BlockSpec((1, POOL_HALO, POOL_W), lambda b, i: (b, 0, 0)),
            pl.BlockSpec((None, len(POOL_WINDOWS), POOL_G, POOL_G), lambda b, i: (l, 0, 0, 0)),
            pl.BlockSpec((1, POOL_W), lambda b, i: (0, 0)),
        ] + prev_specs,
        out_specs=row_spec,
        scratch_shapes=[pltpu.VMEM((POOL_HALO + tt, POOL_W), F32)],
        input_output_aliases=aliases,
        compiler_params=_params("parallel", "arbitrary"),
        name="pool_mix",
    )(*args, *prev_args)


X_SCALE = X_DH ** -0.5


def _cross_kernel(q_ref, k_ref, v_ref, *rest):
    o_ref = rest[-1]
    for h in range(X_HEADS):
        cs = slice(h * X_DH, (h + 1) * X_DH)
        k = k_ref[0, :, cs].astype(BF16)
        v = v_ref[0, :, cs].astype(BF16)
        s = lax.dot_general(q_ref[:, cs], k, _NT, preferred_element_type=F32) * X_SCALE
        e = jnp.exp(s - jnp.max(s, axis=-1, keepdims=True))
        p = e * (1.0 / jnp.sum(e, axis=-1, keepdims=True))
        o_ref[:, cs] = _dot(p.astype(BF16), v).astype(BF16)


def _cross_attend(qx, mk, mv, l, prev, *, bsz, t, row_off):
    tq = min(256, t)
    nq = t // tq
    off = row_off // tq
    row_spec = pl.BlockSpec((tq, X_W), lambda b, i: (off + b * nq + i, 0))
    mem_spec = pl.BlockSpec((None, 1, mk.shape[2], X_W), lambda b, i: (l, b, 0, 0))
    args = [qx, mk, mv]
    prev_args, prev_specs, aliases = _alias_prev(prev, len(args))
    return pl.pallas_call(
        _cross_kernel,
        out_shape=jax.ShapeDtypeStruct((qx.shape[0], X_W), BF16),
        grid=(bsz, nq),
        in_specs=[row_spec, mem_spec, mem_spec] + prev_specs,
        out_specs=row_spec,
        input_output_aliases=aliases,
        compiler_params=_params("parallel", "arbitrary"),
        name="cross_attend",
    )(*args, *prev_args)


def _merge_kernel(x_ref, wg0_ref, wg1_ref, wg2_ref, bg0_ref, bg1_ref, bg2_ref,
                  ha_ref, hb_ref, hc_ref, wa_ref, wb_ref, wc_ref, side_in_ref,
                  o_ref, side_out_ref, *, side_slabs):
    step = pl.program_id(0) * pl.num_programs(1) + pl.program_id(1)

    @pl.when(step < side_slabs)
    def _():
        side_out_ref[...] = side_in_ref[...].astype(BF16)

    x = x_ref[...]
    out = None
    for wg_ref, bg_ref, h_ref, wp_ref in ((wg0_ref, bg0_ref, ha_ref, wa_ref),
                                          (wg1_ref, bg1_ref, hb_ref, wb_ref),
                                          (wg2_ref, bg2_ref, hc_ref, wc_ref)):
        gate = jax.nn.sigmoid(lax.dot_general(x, wg_ref[...], _NT, preferred_element_type=F32)
                              + bg_ref[...])
        term = gate * _dot(h_ref[...], wp_ref[...])
        out = term if out is None else out + term
    o_ref[...] = out.astype(BF16)


MERGE_SIDE_ROWS = 32


def _merge(xb, wt_b, bg, ha, hb, hc, w_pa, w_pb, w_pc, side_w, l, *, bm=1024, bn=256):
    m = xb.shape[0]
    _, sr, sc = side_w.shape
    side_slabs = sr // MERGE_SIDE_ROWS
    assert sr % MERGE_SIDE_ROWS == 0 and side_slabs <= (m // bm) * (D_MODEL // bn)
    nb = D_MODEL // bn
    gb = B_G // bn
    row = lambda width: pl.BlockSpec((bm, width), lambda i, j: (i, 0))
    row1 = lambda width: pl.BlockSpec((bm, width), lambda i, j: (i, 0),
                                      pipeline_mode=pl.Buffered(1))
    colw = lambda kdim: pl.BlockSpec((None, kdim, bn), lambda i, j: (0, 0, j))
    gate_w = [pl.BlockSpec((None, bn, D_MODEL), lambda i, j, g=g: (0, gb + g * nb + j, 0))
              for g in range(N_BRANCH)]
    gate_b = [pl.BlockSpec((1, bn), lambda i, j, g=g: (0, g * nb + j)) for g in range(N_BRANCH)]
    slab = lambda i, j: jnp.minimum(i * nb + j, side_slabs - 1)
    return pl.pallas_call(
        functools.partial(_merge_kernel, side_slabs=side_slabs),
        out_shape=(jax.ShapeDtypeStruct((m, D_MODEL), BF16),
                   jax.ShapeDtypeStruct((1, sr, sc), BF16)),
        grid=(m // bm, nb),
        in_specs=[row(D_MODEL)] + gate_w + gate_b
                 + [row(MLSTM_W), row1(SB_W), row1(POOL_W), colw(MLSTM_W), colw(SB_W), colw(POOL_W),
                    pl.BlockSpec((None, MERGE_SIDE_ROWS, sc), lambda i, j: (l, slab(i, j), 0))],
        out_specs=(pl.BlockSpec((bm, bn), lambda i, j: (i, j)),
                   pl.BlockSpec((None, MERGE_SIDE_ROWS, sc), lambda i, j: (0, slab(i, j), 0))),
        compiler_params=pltpu.CompilerParams(dimension_semantics=("arbitrary", "arbitrary"),
                                             vmem_limit_bytes=62 * 1024 * 1024),
        name="gated_merge",
    )(xb, wt_b, wt_b, wt_b, bg, bg, bg, ha, hb, hc, w_pa, w_pb, w_pc, side_w)


def _trunk_layer(x, xb, W, l, groups, kv_prev, bufs, last):
    b_in = W["b_in"][l]
    proj_a = functools.partial(_matmul, xb, W["wt_in"], l, wt=True, bn=512)
    group_rows = tuple(g["bsz"] * g["t"] for g in groups)

    k_scale = jnp.concatenate([jnp.ones((MLSTM_QK_W,), F32),
                               jnp.full((MLSTM_QK_W,), MLSTM_DK ** -0.5, F32),
                               jnp.ones((MLSTM_W,), F32)])
    qkv, wt_b = proj_a(n=OFF_AO, bias=b_in[:OFF_AO], scale=k_scale, out_dtype=BF16,
                       side_cast=[(W["wt_in"], l, OFF_BQ, IN_W - OFF_BQ, 256)], name="proj_qkv")
    proj_b = functools.partial(_matmul, xb, wt_b, 0, wt=True)
    whole = lambda name, slab: (W[name], l, 0, W[name].shape[1], slab)
    ao, w_out = proj_a(n=MLSTM_W, col_off=OFF_AO, bias=b_in[OFF_AO:OFF_AI], act="sigmoid",
                       side_cast=[whole("w_out", 128)], name="proj_ogate")
    b_if = jnp.pad(b_in[OFF_AI:OFF_BQ], (0, GATE_PAD - 2 * MLSTM_HEADS))
    gif, w_pa, w_pb, w_pc = proj_a(
        n=GATE_PAD, col_off=OFF_AI, bias=b_if, bn=GATE_PAD,
        side_cast=[whole("w_pa", 256), whole("w_pb", 128), whole("w_pc", 128)], name="proj_if")
    sec = lambda i: b_in[OFF_BQ + i * SB_W:OFF_BQ + (i + 1) * SB_W]
    bq, wq_x, wo_x = proj_b(n=SB_W, bias=sec(0), out_dtype=BF16,
                            side_cast=[whole("wq_x", 512), whole("wo_x", 128)], name="proj_sb_q")
    k8 = _proj_heads(xb, wt_b, l, sec(1), B_BK, group_rows, kv_prev[0], name="proj_sb_k")
    v8 = _proj_heads(xb, wt_b, l, sec(2), B_BV, group_rows, kv_prev[1], name="proj_sb_v")
    u = proj_b(n=POOL_W, col_off=B_CU, bias=sec(3), name="proj_pool_u")

    rows = xb.shape[0]
    if bufs is None:
        bufs = tuple(jnp.zeros((rows, w), BF16) for w in (MLSTM_W, SB_W, POOL_W, X_W))
    ha, hb, hc, ox = bufs
    states = []
    for gi, g in enumerate(groups):
        dims = dict(bsz=g["bsz"], t=g["t"], row_off=g["row_off"])
        ha, c_new, n_new, m_new = _mlstm(qkv, ao, gif, g["c"], g["n"], g["m"],
                                         W["mlstm_norm_g"][l], l, ha, **dims)
        hb = _stick_breaking(bq, k8[gi], v8[gi], g["sb_k"], g["sb_v"], l, hb, **dims)
        hc = _pool(u, g["pool_buf"][l], W["w_pool"], W["pool_scale"][l], l, hc,
                   pos0=g["pos0"], **dims)
        states.append((c_new, n_new, m_new))

    merged, w_up = _merge(xb, wt_b, b_in[OFF_G:].reshape(1, -1), ha, hb, hc,
                          w_pa, w_pb, w_pc, W["w_up"], l)
    y1 = _matmul(merged, w_out, 0, n=D_MODEL, res=x, name="w_out")
    ln1 = (W["ln1_g"][l], W["ln1_b"][l])
    mu1, rs1, xb = _layer_norm(y1, *ln1, emit_f32=False)

    qx = _matmul(xb, wq_x, 0, n=X_W, out_dtype=BF16, name="wq_x")
    for g in groups:
        ox = _cross_attend(qx, g["mk"], g["mv"], l, ox, bsz=g["bsz"], t=g["t"],
                           row_off=g["row_off"])
    y2 = _matmul(ox, wo_x, 0, n=D_MODEL, res=y1, res_ln=(mu1, rs1, *ln1), name="wo_x")
    ln2 = (W["ln2_g"][l], W["ln2_b"][l])
    mu2, rs2, xb = _layer_norm(y2, *ln2, emit_f32=False)

    hid, w_down = _matmul(xb, w_up, 0, n=D_FF, act="relu2", out_dtype=BF16,
                          side_cast=[whole("w_down", 128)], name="w_up")
    y = _matmul(hid, w_down, 0, n=D_MODEL, bk=D_MODEL, name="w_down")
    if last:
        x = tuple(_layer_norm(y2, W["ln3_g"][l], W["ln3_b"][l], res=y, x_ln=(mu2, rs2, *ln2),
                              in_off=g["row_off"], n_rows=g["bsz"] * g["t"], emit_bf16=False)
                  for g in groups)
        xb = None
    else:
        x, xb = _layer_norm(y2, W["ln3_g"][l], W["ln3_b"][l], res=y, x_ln=(mu2, rs2, *ln2))
    return x, xb, (k8, v8), u, states, (ha, hb, hc, ox)


def kernel(x_prompt, x_sample, cache_sb_k, cache_sb_v, state_mlstm_c, state_mlstm_n, state_mlstm_m, state_pool, cache_mem_k, cache_mem_v, mem_prompt, ln_in_g, ln_in_b, w_in, b_in, mlstm_norm_g, w_pool, pool_scale, w_pa, w_pb, w_pc, w_out, ln1_g, ln1_b, wq_x, wk_x, wv_x, wo_x, ln2_g, ln2_b, w_up, w_down, ln3_g, ln3_b):
    bf = lambda a: a.astype(BF16)
    wt_in = jnp.swapaxes(w_in, 1, 2)
    W = dict(wt_in=wt_in,
             b_in=b_in, mlstm_norm_g=mlstm_norm_g, w_pool=bf(w_pool),
             pool_scale=pool_scale, w_pa=w_pa, w_pb=w_pb, w_pc=w_pc, w_out=w_out,
             ln1_g=ln1_g, ln1_b=ln1_b, wq_x=wq_x, wo_x=wo_x, ln2_g=ln2_g, ln2_b=ln2_b,
             w_up=w_up, w_down=w_down, ln3_g=ln3_g, ln3_b=ln3_b)
    bp, tp, _ = x_prompt.shape
    bs, ts, _ = x_sample.shape
    rows_p, rows_s = bp * tp, bs * ts
    rows = rows_p + rows_s
    past = cache_sb_k.shape[2]
    mem_len = mem_prompt.shape[1]

    mem_b = mem_prompt.reshape(bp * mem_len, D_MODEL).astype(BF16)
    mem4 = lambda a, b: a.reshape(DEPTH, b, mem_len, X_W)
    mem_proj = lambda w, name: mem4(jnp.stack(
        [_matmul(mem_b, w, l, n=X_W, bn=512, name=name) for l in range(DEPTH)]), bp)
    mk_p, mv_p = mem_proj(wk_x, "mem_k"), mem_proj(wv_x, "mem_v")

    lane_m = lambda m: jnp.broadcast_to(m[..., None], m.shape + (LANES,))
    halo = lambda b: jnp.pad(b, ((0, 0), (0, 0), (POOL_HALO - POOL_BUF, 0), (0, 0)))
    heads_rows = lambda c: c.reshape(DEPTH, bs, past * SB_HEADS, SB_DH)
    groups = [
        dict(bsz=bp, t=tp, row_off=0, pos0=0,
             c=jnp.zeros((DEPTH, bp, MLSTM_HEADS, MLSTM_DK, MLSTM_DV), F32),
             n=jnp.zeros((DEPTH, bp, MLSTM_HEADS, MLSTM_DK), F32),
             m=jnp.zeros((DEPTH, bp, MLSTM_HEADS, LANES), F32),
             sb_k=None, sb_v=None,
             pool_buf=jnp.zeros((DEPTH, bp, POOL_HALO, POOL_W), F32),
             mk=mk_p, mv=mv_p),
        dict(bsz=bs, t=ts, row_off=rows_p, pos0=past,
             c=state_mlstm_c, n=state_mlstm_n, m=lane_m(state_mlstm_m),
             sb_k=heads_rows(cache_sb_k), sb_v=heads_rows(cache_sb_v),
             pool_buf=halo(state_pool),
             mk=mem4(cache_mem_k, bs), mv=mem4(cache_mem_v, bs)),
    ]

    xs = None
    for g, xin in zip(groups, (x_prompt, x_sample)):
        xs = _layer_norm(xin.reshape(g["bsz"] * g["t"], D_MODEL), ln_in_g, ln_in_b,
                         out_off=g["row_off"], out_rows=rows, prev=xs)
    x, xb = xs

    kv = tuple(tuple(jnp.zeros((DEPTH, g["bsz"] * g["t"] * SB_HEADS, SB_DH), F32) for g in groups)
               for _ in range(2))
    per_layer = []
    bufs = None
    for l in range(DEPTH):
        x, xb, kv, u, states, bufs = _trunk_layer(x, xb, W, l, groups, kv, bufs,
                                                  last=l == DEPTH - 1)
        per_layer.append((u, states))

    def group_out(gi, g):
        bsz, t, r0 = g["bsz"], g["t"], g["row_off"]
        sbk = kv[0][gi].reshape(DEPTH, bsz, t, SB_HEADS, SB_DH)
        sbv = kv[1][gi].reshape(DEPTH, bsz, t, SB_HEADS, SB_DH)
        c = jnp.stack([st[gi][0] for _, st in per_layer])
        n = jnp.stack([st[gi][1] for _, st in per_layer])
        m = jnp.stack([st[gi][2][..., 0] for _, st in per_layer])
        pool = jnp.stack([u[r0:r0 + bsz * t].reshape(bsz, t, POOL_W)[:, t - POOL_BUF:, :]
                          for u, _ in per_layer])
        return sbk, sbv, c, n, m, pool

    out_p = group_out(0, groups[0])
    out_s = group_out(1, groups[1])
    y_prompt = x[0].reshape(bp, tp, D_MODEL)
    y_sample = x[1].reshape(bs, ts, D_MODEL)
    mem5 = lambda a: a.reshape(DEPTH, bp, mem_len, X_HEADS, X_DH)
    return (y_prompt, y_sample) + out_p + (mem5(mk_p), mem5(mv_p)) + out_s
```

```python
import functools

import jax
import jax.numpy as jnp
from jax import lax
from jax.experimental import pallas as pl
from jax.experimental.pallas import tpu as pltpu

F32 = jnp.float32
BF16 = jnp.bfloat16

D_MODEL = 4096
DEPTH = 2
MLSTM_CHUNK = 256
MLSTM_HEADS = 8
MLSTM_W = D_MODEL // 2
MLSTM_DV = MLSTM_W // MLSTM_HEADS
MLSTM_DK = MLSTM_DV // 2
MLSTM_QK_W = MLSTM_HEADS * MLSTM_DK
SB_HEADS = 8
SB_W = D_MODEL // 4
SB_DH = SB_W // SB_HEADS
POOL_WINDOWS = (2, 4, 8, 16)
POOL_W = D_MODEL // 4
POOL_G = POOL_W // len(POOL_WINDOWS)
POOL_BUF = max(POOL_WINDOWS) - 1
POOL_HALO = POOL_BUF + 1
X_HEADS = 4
X_W = D_MODEL // 4
X_DH = X_W // X_HEADS
D_FF = 4 * D_MODEL
N_BRANCH = 3
ALPHA = (2 * DEPTH) ** 0.25
LN_EPS = 1e-5
HEAD_NORM_EPS = 1e-6

OFF_AO = 2 * MLSTM_QK_W + MLSTM_W
OFF_AI = OFF_AO + MLSTM_W
OFF_BQ = OFF_AI + 2 * MLSTM_HEADS
OFF_G = OFF_BQ + 3 * SB_W + POOL_W
IN_W = OFF_G + N_BRANCH * D_MODEL

V7X_VMEM_LIMIT_BYTES = 56 * 1024 * 1024
LANES = 128
SUBLANES = 8
SB_TQ = 256
GATE_PAD = LANES

B_BK = SB_W
B_BV = 2 * SB_W
B_CU = 3 * SB_W
B_G = 3 * SB_W + POOL_W

_NT = (((1,), (1,)), ((), ()))
_TN = (((0,), (0,)), ((), ()))


def _params(*sem):
    return pltpu.CompilerParams(dimension_semantics=sem,
                                vmem_limit_bytes=V7X_VMEM_LIMIT_BYTES)


def _dot(a, b):
    return jnp.dot(a, b, preferred_element_type=F32)


def _split(x, terms):
    out = []
    for _ in range(terms - 1):
        h = x.astype(BF16)
        out.append(h)
        x = x - h.astype(F32)
    out.append(x.astype(BF16))
    return out


def _apply_ln(x, mu_ref, rs_ref, g_ref, b_ref):
    return (x - mu_ref[:, 0:1]) * rs_ref[:, 0:1] * g_ref[...] + b_ref[...]


def _alias_prev(prev, n_in):
    if prev is None:
        return [], [], {}
    prev = list(prev) if isinstance(prev, (tuple, list)) else [prev]
    specs = [pl.BlockSpec(memory_space=pl.ANY)] * len(prev)
    return prev, specs, {n_in + i: i for i in range(len(prev))}


def _mm_kernel(*refs, act, has_bias, has_scale, has_res, res_ln, nk, side_slabs, wt):
    it = iter(refs)
    x_ref, w_ref = next(it), next(it)
    b_ref = next(it) if has_bias else None
    s_ref = next(it) if has_scale else None
    r_ref = next(it) if has_res else None
    ln_refs = [next(it) for _ in range(4)] if res_ln else None
    side_in_refs = [next(it) for _ in side_slabs]
    o_ref = next(it)
    side_out_refs = [next(it) for _ in side_slabs]
    acc_ref = next(it) if nk > 1 else None

    if side_slabs:
        step = pl.program_id(0) * pl.num_programs(1) + pl.program_id(1)
        for n_slabs, in_ref, out_ref in zip(side_slabs, side_in_refs, side_out_refs):
            @pl.when(step < n_slabs)
            def _(in_ref=in_ref, out_ref=out_ref):
                out_ref[...] = in_ref[0].astype(BF16)

    def epilogue(y):
        if has_bias:
            y = y + b_ref[...]
        if has_scale:
            y = y * s_ref[...]
        if act == "sigmoid":
            y = jax.nn.sigmoid(y)
        elif act == "relu2":
            y = jnp.square(jnp.maximum(y, 0.0))
        if has_res:
            r = r_ref[...]
            if res_ln:
                r = _apply_ln(r, *ln_refs)
            y = ALPHA * r + y
        o_ref[...] = y.astype(o_ref.dtype)

    w = w_ref[...].astype(BF16)
    part = lax.dot_general(x_ref[...], w, _NT if wt else (((1,), (0,)), ((), ())),
                           preferred_element_type=F32)
    if nk == 1:
        epilogue(part)
    else:
        k = pl.program_id(2)

        @pl.when(k == 0)
        def _():
            acc_ref[...] = part

        @pl.when(k > 0)
        def _():
            acc_ref[...] += part

        @pl.when(k == nk - 1)
        def _():
            epilogue(acc_ref[...])


def _matmul(x, w, l, *, n, col_off=0, bias=None, scale=None, res=None, res_ln=None, act="none",
            out_dtype=F32, bm=1024, bn=1024, bk=None, side_cast=(), wt=False, name="matmul"):
    m, kdim = x.shape
    bm, bn = min(bm, m), min(bn, n)
    bk = kdim if bk is None else bk
    assert m % bm == 0 and n % bn == 0 and kdim % bk == 0 and col_off % bn == 0
    nk = kdim // bk
    cb = col_off // bn
    nj = n // bn
    w_spec = (pl.BlockSpec((None, bn, bk), lambda i, j, k: (l, cb + j, k)) if wt
              else pl.BlockSpec((None, bk, bn), lambda i, j, k: (l, k, cb + j)))
    in_specs = [pl.BlockSpec((bm, bk), lambda i, j, k: (i, k)), w_spec]
    args = [x, w]
    for v in (bias, scale):
        if v is not None:
            in_specs.append(pl.BlockSpec((1, bn), lambda i, j, k: (0, j)))
            args.append(v.reshape(1, n).astype(F32))
    if res is not None:
        in_specs.append(pl.BlockSpec((bm, bn), lambda i, j, k: (i, j)))
        args.append(res)
    if res_ln is not None:
        mu, rs, ln_g, ln_b = res_ln
        in_specs += [pl.BlockSpec((bm, LANES), lambda i, j, k: (i, 0))] * 2
        in_specs += [pl.BlockSpec((1, bn), lambda i, j, k: (0, j))] * 2
        args += [mu, rs, ln_g.reshape(1, n), ln_b.reshape(1, n)]
    out_shape = [jax.ShapeDtypeStruct((m, n), out_dtype)]
    out_specs = [pl.BlockSpec((bm, bn), lambda i, j, k: (i, j))]
    side_slabs = []
    for side_w, side_l, row_off, sr, slab_rows in side_cast:
        sc = side_w.shape[2]
        n_slabs = sr // slab_rows
        assert nk == 1 and sr % slab_rows == 0 and n_slabs <= (m // bm) * nj
        slab = lambda i, j, k, n_slabs=n_slabs: jnp.minimum(i * nj + j, n_slabs - 1)
        in_specs.append(pl.BlockSpec(
            (pl.Element(1), pl.Element(slab_rows), pl.Element(sc)),
            lambda i, j, k, slab=slab, side_l=side_l, row_off=row_off, slab_rows=slab_rows: (
                side_l, pl.multiple_of(row_off + slab(i, j, k) * slab_rows, SUBLANES), 0)))
        args.append(side_w)
        out_shape.append(jax.ShapeDtypeStruct((1, sr, sc), BF16))
        out_specs.append(pl.BlockSpec((None, slab_rows, sc),
                                      lambda i, j, k, slab=slab: (0, slab(i, j, k), 0)))
        side_slabs.append(n_slabs)
    kern = functools.partial(_mm_kernel, act=act, has_bias=bias is not None,
                             has_scale=scale is not None, has_res=res is not None,
                             res_ln=res_ln is not None, nk=nk,
                             side_slabs=tuple(side_slabs), wt=wt)
    out = pl.pallas_call(
        kern,
        out_shape=tuple(out_shape),
        grid=(m // bm, nj, nk),
        in_specs=in_specs,
        out_specs=tuple(out_specs),
        scratch_shapes=[pltpu.VMEM((bm, bn), F32)] if nk > 1 else [],
        compiler_params=(_params("arbitrary", "arbitrary", "arbitrary") if side_slabs
                         else _params("parallel", "parallel", "arbitrary")),
        name=name,
    )(*args)
    return out if side_slabs else out[0]


def _proj_heads_kernel(x_ref, w_ref, b_ref, *rest, n_first):
    first_ref, second_ref = rest[-2:]
    y = lax.dot_general(x_ref[...], w_ref[...], _NT, preferred_element_type=F32) + b_ref[...]
    rows = y.shape[0]

    def store(o_ref):
        for h in range(SUBLANES):
            o_ref[pl.ds(h, rows, stride=SUBLANES), :] = y[:, h * LANES:(h + 1) * LANES]

    i = pl.program_id(0)

    @pl.when(i < n_first)
    def _():
        store(first_ref)

    @pl.when(i >= n_first)
    def _():
        store(second_ref)


def _proj_heads(x, w, l, bias, col_off, group_rows, prev, *, bm=1024, name="proj_heads"):
    m, kdim = x.shape
    n = SUBLANES * LANES
    rows_a, rows_b = group_rows
    assert rows_a % bm == 0 and rows_b % bm == 0 and rows_a + rows_b == m and col_off % n == 0
    n_first = rows_a // bm
    cb = col_off // n
    args = [x, w, bias.reshape(1, n)]
    prev_args, prev_specs, aliases = _alias_prev(prev, len(args))
    blk = (None, bm * SUBLANES, LANES)
    return pl.pallas_call(
        functools.partial(_proj_heads_kernel, n_first=n_first),
        out_shape=tuple(jax.ShapeDtypeStruct((DEPTH, r * SUBLANES, LANES), F32) for r in group_rows),
        grid=(m // bm,),
        in_specs=[pl.BlockSpec((bm, kdim), lambda i: (i, 0)),
                  pl.BlockSpec((None, n, kdim), lambda i: (0, cb, 0)),
                  pl.BlockSpec((1, n), lambda i: (0, 0))] + prev_specs,
        out_specs=(pl.BlockSpec(blk, lambda i: (l, jnp.minimum(i, n_first - 1), 0)),
                   pl.BlockSpec(blk, lambda i: (l, jnp.maximum(i - n_first, 0), 0))),
        input_output_aliases=aliases,
        compiler_params=_params("arbitrary"),
        name=name,
    )(*args, *prev_args)


def _ln_kernel(*refs, x_ln, has_res, n_prev, emit_f32, emit_bf16, n_blocks):
    it = iter(refs)
    x_ref = next(it)
    x_ln_refs = [next(it) for _ in range(4)] if x_ln else None
    r_ref = next(it) if has_res else None
    g_ref, b_ref = next(it), next(it)
    for _ in range(n_prev):
        next(it)
    x = x_ref[...]
    if x_ln:
        x = _apply_ln(x, *x_ln_refs)
    if has_res:
        x = ALPHA * x + r_ref[...]
    mu = jnp.mean(x, axis=-1, keepdims=True)
    xc = x - mu
    var = jnp.mean(xc * xc, axis=-1, keepdims=True)
    rs = lax.rsqrt(var + LN_EPS)
    y = xc * rs * g_ref[...] + b_ref[...]
    if n_blocks is not None:
        y = jnp.where(pl.program_id(0) < n_blocks, y, 0.0)
    if emit_f32:
        next(it)[...] = y
    else:
        mu_ref, rs_ref = next(it), next(it)
        mu_ref[...] = jnp.broadcast_to(mu, mu_ref.shape)
        rs_ref[...] = jnp.broadcast_to(rs, rs_ref.shape)
    if emit_bf16:
        next(it)[...] = y.astype(BF16)


def _layer_norm(x, g, b, res=None, *, x_ln=None, in_off=0, n_rows=None, out_off=0, out_rows=None,
                prev=None, emit_f32=True, emit_bf16=True, rows=256):
    d = x.shape[1]
    n_rows = x.shape[0] - in_off if n_rows is None else n_rows
    out_rows = n_rows if out_rows is None else out_rows
    ib, ob = in_off // rows, out_off // rows
    n_blocks = n_rows // rows
    fill = prev is None and out_rows > n_rows
    assert not fill or out_off == 0
    in_spec = pl.BlockSpec((rows, d), lambda i: (ib + jnp.minimum(i, n_blocks - 1), 0))
    stat_in_spec = pl.BlockSpec((rows, LANES), lambda i: (ib + jnp.minimum(i, n_blocks - 1), 0))
    out_spec = pl.BlockSpec((rows, d), lambda i: (ob + i, 0))
    stat_spec = pl.BlockSpec((rows, LANES), lambda i: (ob + i, 0))
    vec_spec = pl.BlockSpec((1, d), lambda i: (0, 0))
    args, in_specs = [x], [in_spec]
    if x_ln is not None:
        args += [x_ln[0], x_ln[1], x_ln[2].reshape(1, d), x_ln[3].reshape(1, d)]
        in_specs += [stat_in_spec, stat_in_spec, vec_spec, vec_spec]
    if res is not None:
        args.append(res)
        in_specs.append(in_spec)
    args += [g.reshape(1, d), b.reshape(1, d)]
    in_specs += [vec_spec, vec_spec]
    prev_args, prev_specs, aliases = _alias_prev(prev, len(args))
    if emit_f32:
        out_shape, out_specs = [jax.ShapeDtypeStruct((out_rows, d), F32)], [out_spec]
    else:
        out_shape = [jax.ShapeDtypeStruct((out_rows, LANES), F32)] * 2
        out_specs = [stat_spec, stat_spec]
    if emit_bf16:
        out_shape.append(jax.ShapeDtypeStruct((out_rows, d), BF16))
        out_specs.append(out_spec)
    out = pl.pallas_call(
        functools.partial(_ln_kernel, x_ln=x_ln is not None, has_res=res is not None,
                          n_prev=len(prev_args), emit_f32=emit_f32, emit_bf16=emit_bf16,
                          n_blocks=n_blocks if fill else None),
        out_shape=tuple(out_shape),
        grid=(out_rows // rows if fill else n_blocks,),
        in_specs=in_specs + prev_specs,
        out_specs=tuple(out_specs),
        input_output_aliases=aliases,
        compiler_params=_params("parallel"),
        name="layer_norm",
    )(*args, *prev_args)
    return out if len(out) > 1 else out[0]


def _mlstm_kernel(q_ref, k_ref, v_ref, ao_ref, gif_ref, c0_ref, n0_ref, m0_ref, g_ref, *rest,
                  n_seq):
    ha_ref, c_ref, n_ref, m_ref = rest[-4:]
    L = q_ref.shape[0] // n_seq

    @pl.when(pl.program_id(1) == 0)
    def _():
        c_ref[...] = c0_ref[...]
        n_ref[...] = n0_ref[...]
        m_ref[...] = m0_ref[...]

    row = lax.broadcasted_iota(jnp.int32, (L, L), 0)
    col = lax.broadcasted_iota(jnp.int32, (L, L), 1)
    causal = col <= row
    tril = jnp.where(causal, 1.0, 0.0).astype(BF16)
    sel_r = lax.broadcasted_iota(jnp.int32, (2 * MLSTM_HEADS, GATE_PAD), 0)
    sel_c = lax.broadcasted_iota(jnp.int32, (2 * MLSTM_HEADS, GATE_PAD), 1)
    sel = jnp.where(sel_r == sel_c, 1.0, 0.0).astype(BF16)

    def rows_of(x):
        return sum(lax.dot_general(sel, t, _NT, preferred_element_type=F32) for t in _split(x, 3))

    for s_i in range(n_seq):
        rs = slice(s_i * L, (s_i + 1) * L)
        gif = gif_ref[rs, :]
        logf = -(jnp.maximum(-gif, 0.0) + jnp.log1p(jnp.exp(-jnp.abs(gif))))
        b_all = sum(_dot(tril, t) for t in _split(logf, 3))
        i_rows = rows_of(gif)
        b_rows = rows_of(b_all)

        for h in range(MLSTM_HEADS):
            qs = slice(h * MLSTM_DK, (h + 1) * MLSTM_DK)
            vs = slice(h * MLSTM_DV, (h + 1) * MLSTM_DV)
            q = q_ref[rs, qs]
            k = k_ref[rs, qs]
            v = v_ref[rs, vs]
            i_col = gif[:, h:h + 1]
            b_col = b_all[:, MLSTM_HEADS + h:MLSTM_HEADS + h + 1]
            i_row = i_rows[h:h + 1, :]
            b_row = b_rows[MLSTM_HEADS + h:MLSTM_HEADS + h + 1, :]
            c_old = c_ref[s_i, h]
            n_old = n_ref[s_i, h:h + 1, :]
            m_old = m_ref[s_i, h:h + 1, 0:1]

            dmat = jnp.where(causal, b_col - b_row + i_row, -jnp.inf)
            m_inter = b_col + m_old
            m_t = jnp.maximum(m_inter, jnp.max(dmat, axis=-1, keepdims=True))
            s = lax.dot_general(q, k, _NT, preferred_element_type=F32) * jnp.exp(dmat - m_t)
            w_inter = jnp.exp(m_inter - m_t)
            kf = k.astype(F32)
            num = _dot(s.astype(BF16), v) + w_inter * _dot(q, c_old.astype(BF16))
            qn = jnp.sum(q.astype(F32) * n_old, axis=-1, keepdims=True)
            den = jnp.sum(s, axis=-1, keepdims=True) + w_inter * qn
            hh = num * (1.0 / jnp.maximum(jnp.abs(den), jnp.exp(-m_t)))

            m_new = m_t[L - 1:L, :]
            b_last = b_col[L - 1:L, :]
            g_col = jnp.exp(b_last - b_col + i_col - m_new)
            decay = jnp.exp(b_last + m_old - m_new)
            kg = kf * g_col
            c_ref[s_i, h] = decay * c_old + lax.dot_general(kg.astype(BF16), v, _TN,
                                                            preferred_element_type=F32)
            n_ref[s_i, h:h + 1, :] = decay * n_old + jnp.sum(kg, axis=0, keepdims=True)
            m_ref[s_i, h:h + 1, :] = jnp.broadcast_to(m_new, (1, LANES))

            mu = jnp.mean(hh, axis=-1, keepdims=True)
            hc = hh - mu
            var = jnp.mean(hc * hc, axis=-1, keepdims=True)
            hn = hc * lax.rsqrt(var + HEAD_NORM_EPS) * g_ref[:, vs]
            ha_ref[rs, vs] = (ao_ref[rs, vs] * hn).astype(BF16)


def _mlstm(qkv, ao, gif, c0, n0, m0, norm_g, l, prev, *, bsz, t, row_off):
    chunk = min(MLSTM_CHUNK, t)
    nc = t // chunk
    n_seq = MLSTM_CHUNK // chunk if nc == 1 and bsz % (MLSTM_CHUNK // chunk) == 0 else 1
    blk = n_seq * chunk
    off = row_off // blk
    rows = lambda b, c: off + b * nc + c
    state4 = pl.BlockSpec((n_seq, MLSTM_HEADS, MLSTM_DK, MLSTM_DV), lambda b, c: (b, 0, 0, 0))
    state3 = pl.BlockSpec((n_seq, MLSTM_HEADS, LANES), lambda b, c: (b, 0, 0))
    init4 = pl.BlockSpec((None, n_seq, MLSTM_HEADS, MLSTM_DK, MLSTM_DV),
                         lambda b, c: (l, b, 0, 0, 0))
    init3 = pl.BlockSpec((None, n_seq, MLSTM_HEADS, LANES), lambda b, c: (l, b, 0, 0))
    args = [qkv, qkv, qkv, ao, gif, c0, n0, m0, norm_g.reshape(1, MLSTM_W)]
    prev_args, prev_specs, aliases = _alias_prev(prev, len(args))
    return pl.pallas_call(
        functools.partial(_mlstm_kernel, n_seq=n_seq),
        out_shape=(jax.ShapeDtypeStruct((qkv.shape[0], MLSTM_W), BF16),
                   jax.ShapeDtypeStruct(c0.shape[1:], F32),
                   jax.ShapeDtypeStruct(n0.shape[1:], F32),
                   jax.ShapeDtypeStruct(m0.shape[1:], F32)),
        grid=(bsz // n_seq, nc),
        in_specs=[
            pl.BlockSpec((blk, MLSTM_QK_W), lambda b, c: (rows(b, c), 0)),
            pl.BlockSpec((blk, MLSTM_QK_W), lambda b, c: (rows(b, c), 1)),
            pl.BlockSpec((blk, MLSTM_W), lambda b, c: (rows(b, c), 1)),
            pl.BlockSpec((blk, MLSTM_W), lambda b, c: (rows(b, c), 0)),
            pl.BlockSpec((blk, GATE_PAD), lambda b, c: (rows(b, c), 0)),
            init4, init3, init3,
            pl.BlockSpec((1, MLSTM_W), lambda b, c: (0, 0)),
        ] + prev_specs,
        out_specs=(pl.BlockSpec((blk, MLSTM_W), lambda b, c: (rows(b, c), 0)),
                   state4, state3, state3),
        input_output_aliases=aliases,
        compiler_params=_params("parallel", "arbitrary"),
        name="mlstm",
    )(*args, *prev_args)


SB_SCALE = SB_DH ** -0.5
SB_DEAD_CARRY = -110.0


def _lower_tri(n):
    return jnp.where(lax.broadcasted_iota(jnp.int32, (n, n), 0)
                     >= lax.broadcasted_iota(jnp.int32, (n, n), 1), 1.0, 0.0).astype(BF16)


def _sb_group(q_ref, load_k, load_v, heads, acc_ref, carry_ref, lower, *, diag):
    tq = q_ref.shape[0]
    tk = lower.shape[0]
    if diag:
        mask = (lax.broadcasted_iota(jnp.int32, (tq, tk), 1)
                < lax.broadcasted_iota(jnp.int32, (tq, tk), 0))
    zs, lks = [], []
    for h in heads:
        z = lax.dot_general(q_ref[:, h * SB_DH:(h + 1) * SB_DH], load_k(h), _NT,
                            preferred_element_type=F32) * SB_SCALE
        log_keep = -(jnp.maximum(z, 0.0) + jnp.log(1.0 + jnp.exp(-jnp.abs(z))))
        if diag:
            log_keep = jnp.where(mask, log_keep, 0.0)
        zs.append(z)
        lks.append(log_keep)
    stacked = jnp.concatenate(lks, axis=0)
    rc_all = sum(_dot(t, lower) for t in _split(stacked, 2))
    for i, h in enumerate(heads):
        rc = rc_all[i * tq:(i + 1) * tq]
        carry = carry_ref[h]
        attn = jnp.exp(zs[i] + rc + carry)
        if diag:
            attn = jnp.where(mask, attn, 0.0)
        acc_ref[h] += _dot(attn.astype(BF16), load_v(h))
        carry_ref[h] = carry + rc[:, 0:1]


def _sb_block(q_ref, k_ref, v_ref, base, tk, acc_ref, carry_ref, lower, *, group, diag):
    def loader(ref):
        return lambda h: ref[pl.ds(base + h, tk, stride=SB_HEADS), :].astype(BF16)
    for g in range(0, SB_HEADS, group):
        _sb_group(q_ref, loader(k_ref), loader(v_ref), range(g, g + group),
                  acc_ref, carry_ref, lower, diag=diag)


def _sb_alive(carry_ref):
    return jnp.max(carry_ref[...]) > SB_DEAD_CARRY


def _sb_store(o_ref, acc_ref):
    for h in range(SB_HEADS):
        o_ref[:, h * SB_DH:(h + 1) * SB_DH] = acc_ref[h].astype(BF16)


def _sb_prompt_kernel(q_ref, k_ref, v_ref, *rest, tq, group):
    o_ref, acc_ref, carry_ref = rest[-3:]
    qi = pl.program_id(1)
    acc_ref[...] = jnp.zeros_like(acc_ref)
    carry_ref[...] = jnp.zeros_like(carry_ref)
    lower = _lower_tri(tq)
    blk_rows = tq * SB_HEADS

    _sb_block(q_ref, k_ref, v_ref, pl.multiple_of(qi * blk_rows, blk_rows), tq,
              acc_ref, carry_ref, lower, group=group, diag=True)

    def more(state):
        j, alive = state
        return jnp.logical_and(j < qi, alive)

    def body(state):
        j, _ = state
        base = pl.multiple_of((qi - 1 - j) * blk_rows, blk_rows)
        _sb_block(q_ref, k_ref, v_ref, base, tq, acc_ref, carry_ref, lower,
                  group=group, diag=False)
        return j + 1, _sb_alive(carry_ref)

    lax.while_loop(more, body, (jnp.int32(0), _sb_alive(carry_ref)))
    _sb_store(o_ref, acc_ref)


def _sb_head_kernel(q_ref, k_ref, v_ref, kp_ref, vp_ref, *rest, tq):
    o_ref, acc_out_ref, carry_out_ref, acc_ref, carry_ref = rest[-5:]
    s = pl.program_id(1)

    @pl.when(s == 0)
    def _():
        acc_ref[...] = jnp.zeros_like(acc_ref)
        carry_ref[...] = jnp.zeros_like(carry_ref)
        _sb_block(q_ref, k_ref, v_ref, 0, tq, acc_ref, carry_ref, _lower_tri(tq),
                  group=SB_HEADS, diag=True)

    @pl.when(s == 1)
    def _():
        @pl.when(_sb_alive(carry_ref))
        def _():
            _sb_block(q_ref, kp_ref, vp_ref, 0, SB_TQ, acc_ref, carry_ref, _lower_tri(SB_TQ),
                      group=SB_HEADS, diag=False)

        _sb_store(o_ref, acc_ref)
        acc_out_ref[0] = acc_ref[...]
        carry_out_ref[0] = jnp.broadcast_to(carry_ref[...], carry_out_ref.shape[1:])


def _sb_tail_kernel(alive_ref, fetch_ref, q_ref, kp_ref, vp_ref, acc_in_ref, carry_in_ref, hb_ref,
                    o_ref, acc_ref, carry_ref, *, n_tail):
    del fetch_ref
    b = pl.program_id(0)

    @pl.when(alive_ref[b] == 0)
    def _():
        o_ref[...] = hb_ref[...]

    @pl.when(alive_ref[b] != 0)
    def _():
        acc_ref[...] = acc_in_ref[0]
        carry_ref[...] = carry_in_ref[0][:, :, 0:1]
        lower = _lower_tri(SB_TQ)
        blk_rows = SB_TQ * SB_HEADS

        def body(j, c):
            @pl.when(_sb_alive(carry_ref))
            def _():
                base = pl.multiple_of((n_tail - 1 - j) * blk_rows, blk_rows)
                _sb_block(q_ref, kp_ref, vp_ref, base, SB_TQ, acc_ref, carry_ref, lower,
                          group=SB_HEADS, diag=False)
            return c

        lax.fori_loop(0, n_tail, body, 0)
        _sb_store(o_ref, acc_ref)


def _stick_breaking(bq, k8, v8, past_k, past_v, l, prev, *, bsz, t, row_off):
    tq = min(SB_TQ, t)
    nq = t // tq
    off_q = row_off // tq
    scratch = [pltpu.VMEM((SB_HEADS, tq, SB_DH), F32), pltpu.VMEM((SB_HEADS, tq, 1), F32)]
    out_shape = jax.ShapeDtypeStruct((bq.shape[0], SB_W), BF16)
    q_spec = pl.BlockSpec((tq, SB_W), lambda b, i: (off_q + b * nq + i, 0))
    if past_k is None:
        seq_spec = pl.BlockSpec((None, t * SB_HEADS, SB_DH), lambda b, i: (l, b, 0))
        args = [bq, k8, v8]
        prev_args, prev_specs, aliases = _alias_prev(prev, len(args))
        return pl.pallas_call(
            functools.partial(_sb_prompt_kernel, tq=tq, group=SB_HEADS // 2),
            out_shape=out_shape,
            grid=(bsz, nq),
            in_specs=[q_spec, seq_spec, seq_spec] + prev_specs,
            out_specs=q_spec,
            scratch_shapes=scratch,
            input_output_aliases=aliases,
            compiler_params=_params("parallel", "arbitrary"),
            name="stick_breaking_prompt",
        )(*args, *prev_args)
    assert nq == 1
    blk = SB_TQ * SB_HEADS
    n_past = past_k.shape[2] // blk
    n_tail = n_past - 1
    state_shape = jax.ShapeDtypeStruct((bsz, SB_HEADS, tq, SB_DH), F32)
    state_blk = (1, SB_HEADS, tq, SB_DH)
    q_spec = pl.BlockSpec((tq, SB_W), lambda b, s: (off_q + b, 0))
    new_spec = pl.BlockSpec((None, t * SB_HEADS, SB_DH), lambda b, s: (l, b, 0))
    near_spec = pl.BlockSpec((None, None, blk, SB_DH), lambda b, s: (l, b, n_tail, 0))
    state_spec = pl.BlockSpec(state_blk, lambda b, s: (b, 0, 0, 0))
    args = [bq, k8, v8, past_k, past_v]
    prev_args, prev_specs, aliases = _alias_prev(prev, len(args))
    hb, acc, carry = pl.pallas_call(
        functools.partial(_sb_head_kernel, tq=tq),
        out_shape=(out_shape, state_shape, state_shape),
        grid=(bsz, 2),
        in_specs=[q_spec, new_spec, new_spec, near_spec, near_spec] + prev_specs,
        out_specs=(q_spec, state_spec, state_spec),
        scratch_shapes=scratch,
        input_output_aliases=aliases,
        compiler_params=_params("parallel", "arbitrary"),
        name="stick_breaking_sample",
    )(*args, *prev_args)

    alive = jnp.max(carry, axis=(1, 2, 3)) > SB_DEAD_CARRY
    fetch = jnp.maximum(lax.cummax(jnp.where(alive, jnp.arange(bsz), -1), axis=0), 0)
    tail_spec = pl.BlockSpec((None, None, n_tail * blk, SB_DH),
                             lambda b, alive_ref, fetch_ref: (l, fetch_ref[b], 0, 0))
    state_in_spec = pl.BlockSpec(state_blk, lambda b, alive_ref, fetch_ref: (fetch_ref[b], 0, 0, 0))
    rows_spec = pl.BlockSpec((tq, SB_W), lambda b, alive_ref, fetch_ref: (off_q + b, 0))
    tail = pl.pallas_call(
        functools.partial(_sb_tail_kernel, n_tail=n_tail),
        out_shape=out_shape,
        grid_spec=pltpu.PrefetchScalarGridSpec(
            num_scalar_prefetch=2,
            grid=(bsz,),
            in_specs=[
                pl.BlockSpec((tq, SB_W), lambda b, alive_ref, fetch_ref: (off_q + fetch_ref[b], 0)),
                tail_spec, tail_spec, state_in_spec, state_in_spec,
                rows_spec,
            ],
            out_specs=rows_spec,
            scratch_shapes=scratch,
        ),
        input_output_aliases={7: 0},
        compiler_params=_params("arbitrary"),
        name="stick_breaking_tail",
    )
    return lax.cond(
        jnp.any(alive),
        lambda hb_in: tail(alive.astype(jnp.int32), fetch.astype(jnp.int32), bq, past_k, past_v,
                           acc, carry, hb_in),
        lambda hb_in: hb_in, hb)


def _pool_kernel(u_ref, buf_ref, w_ref, s_ref, *rest, tt, pos0):
    o_ref, ext_ref = rest[-2:]
    i = pl.program_id(1)

    @pl.when(i == 0)
    def _():
        ext_ref[0:POOL_HALO, :] = buf_ref[0]

    @pl.when(i > 0)
    def _():
        ext_ref[0:POOL_HALO, :] = ext_ref[tt:tt + POOL_HALO, :]

    ext_ref[POOL_HALO:POOL_HALO + tt, :] = u_ref[...]
    pos = pos0 + i * tt + lax.broadcasted_iota(jnp.int32, (tt, 1), 0)
    for gi, w in enumerate(POOL_WINDOWS):
        cs = slice(gi * POOL_G, (gi + 1) * POOL_G)
        cur = ext_ref[POOL_HALO:POOL_HALO + tt, cs]
        wsum = cur
        for j in range(1, w):
            wsum = wsum + ext_ref[POOL_HALO - j:POOL_HALO - j + tt, cs]
        cnt = jnp.minimum(pos + 1, w).astype(F32)
        d = (wsum / cnt - cur).astype(BF16)
        y = _dot(d, w_ref[gi]) * s_ref[:, cs]
        o_ref[:, cs] = y.astype(BF16)


def _pool(u, buf, w_pool, scale, l, prev, *, bsz, t, row_off, pos0):
    tt = min(256, t)
    nt = t // tt
    off = row_off // tt
    row_spec = pl.BlockSpec((tt, POOL_W), lambda b, i: (off + b * nt + i, 0))
    args = [u, buf, w_pool, scale.reshape(1, POOL_W)]
    prev_args, prev_specs, aliases = _alias_prev(prev, len(args))
    return pl.pallas_call(
        functools.partial(_pool_kernel, tt=tt, pos0=pos0),
        out_shape=jax.ShapeDtypeStruct((u.shape[0], POOL_W), BF16),
        grid=(bsz, nt),
        in_specs=[
            row_spec,
            pl.BlockSpec((1, POOL_HALO, POOL_W), lambda b, i: (b, 0, 0)),
            pl.BlockSpec((None, len(POOL_WINDOWS), POOL_G, POOL_G), lambda b, i: (l, 0, 0, 0)),
            pl.BlockSpec((1, POOL_W), lambda b, i: (0, 0)),
        ] + prev_specs,
        out_specs=row_spec,
        scratch_shapes=[pltpu.VMEM((POOL_HALO + tt, POOL_W), F32)],
        input_output_aliases=aliases,
        compiler_params=_params("parallel", "arbitrary"),
        name="pool_mix",
    )(*args, *prev_args)


X_SCALE = X_DH ** -0.5


def _cross_kernel(q_ref, k_ref, v_ref, *rest):
    o_ref = rest[-1]
    for h in range(X_HEADS):
        cs = slice(h * X_DH, (h + 1) * X_DH)
        k = k_ref[0, :, cs].astype(BF16)
        v = v_ref[0, :, cs].astype(BF16)
        s = lax.dot_general(q_ref[:, cs], k, _NT, preferred_element_type=F32) * X_SCALE
        e = jnp.exp(s - jnp.max(s, axis=-1, keepdims=True))
        p = e * (1.0 / jnp.sum(e, axis=-1, keepdims=True))
        o_ref[:, cs] = _dot(p.astype(BF16), v).astype(BF16)


def _cross_attend(qx, mk, mv, l, prev, *, bsz, t, row_off):
    tq = min(256, t)
    nq = t // tq
    off = row_off // tq
    row_spec = pl.BlockSpec((tq, X_W), lambda b, i: (off + b * nq + i, 0))
    mem_spec = pl.BlockSpec((None, 1, mk.shape[2], X_W), lambda b, i: (l, b, 0, 0))
    args = [qx, mk, mv]
    prev_args, prev_specs, aliases = _alias_prev(prev, len(args))
    return pl.pallas_call(
        _cross_kernel,
        out_shape=jax.ShapeDtypeStruct((qx.shape[0], X_W), BF16),
        grid=(bsz, nq),
        in_specs=[row_spec, mem_spec, mem_spec] + prev_specs,
        out_specs=row_spec,
        input_output_aliases=aliases,
        compiler_params=_params("parallel", "arbitrary"),
        name="cross_attend",
    )(*args, *prev_args)


def _merge_kernel(x_ref, wg0_ref, wg1_ref, wg2_ref, bg0_ref, bg1_ref, bg2_ref,
                  ha_ref, hb_ref, hc_ref, wa_ref, wb_ref, wc_ref, side_in_ref,
                  o_ref, side_out_ref, *, side_slabs):
    step = pl.program_id(0) * pl.num_programs(1) + pl.program_id(1)

    @pl.when(step < side_slabs)
    def _():
        side_out_ref[...] = side_in_ref[...].astype(BF16)

    x = x_ref[...]
    out = None
    for wg_ref, bg_ref, h_ref, wp_ref in ((wg0_ref, bg0_ref, ha_ref, wa_ref),
                                          (wg1_ref, bg1_ref, hb_ref, wb_ref),
                                          (wg2_ref, bg2_ref, hc_ref, wc_ref)):
        gate = jax.nn.sigmoid(lax.dot_general(x, wg_ref[...], _NT, preferred_element_type=F32)
                              + bg_ref[...])
        term = gate * _dot(h_ref[...], wp_ref[...])
        out = term if out is None else out + term
    o_ref[...] = out.astype(BF16)


MERGE_SIDE_ROWS = 32


def _merge(xb, wt_b, bg, ha, hb, hc, w_pa, w_pb, w_pc, side_w, l, *, bm=1024, bn=256):
    m = xb.shape[0]
    _, sr, sc = side_w.shape
    side_slabs = sr // MERGE_SIDE_ROWS
    assert sr % MERGE_SIDE_ROWS == 0 and side_slabs <= (m // bm) * (D_MODEL // bn)
    nb = D_MODEL // bn
    gb = B_G // bn
    row = lambda width: pl.BlockSpec((bm, width), lambda i, j: (i, 0))
    row1 = lambda width: pl.BlockSpec((bm, width), lambda i, j: (i, 0),
                                      pipeline_mode=pl.Buffered(1))
    colw = lambda kdim: pl.BlockSpec((None, kdim, bn), lambda i, j: (0, 0, j))
    gate_w = [pl.BlockSpec((None, bn, D_MODEL), lambda i, j, g=g: (0, gb + g * nb + j, 0))
              for g in range(N_BRANCH)]
    gate_b = [pl.BlockSpec((1, bn), lambda i, j, g=g: (0, g * nb + j)) for g in range(N_BRANCH)]
    slab = lambda i, j: jnp.minimum(i * nb + j, side_slabs - 1)
    return pl.pallas_call(
        functools.partial(_merge_kernel, side_slabs=side_slabs),
        out_shape=(jax.ShapeDtypeStruct((m, D_MODEL), BF16),
                   jax.ShapeDtypeStruct((1, sr, sc), BF16)),
        grid=(m // bm, nb),
        in_specs=[row(D_MODEL)] + gate_w + gate_b
                 + [row(MLSTM_W), row1(SB_W), row1(POOL_W), colw(MLSTM_W), colw(SB_W), colw(POOL_W),
                    pl.BlockSpec((None, MERGE_SIDE_ROWS, sc), lambda i, j: (l, slab(i, j), 0))],
        out_specs=(pl.BlockSpec((bm, bn), lambda i, j: (i, j)),
                   pl.BlockSpec((None, MERGE_SIDE_ROWS, sc), lambda i, j: (0, slab(i, j), 0))),
        compiler_params=pltpu.CompilerParams(dimension_semantics=("arbitrary", "arbitrary"),
                                             vmem_limit_bytes=62 * 1024 * 1024),
        name="gated_merge",
    )(xb, wt_b, wt_b, wt_b, bg, bg, bg, ha, hb, hc, w_pa, w_pb, w_pc, side_w)


def _trunk_layer(x, xb, W, l, groups, kv_prev, bufs, last):
    b_in = W["b_in"][l]
    proj_a = functools.partial(_matmul, xb, W["wt_in"], l, wt=True, bn=512)
    group_rows = tuple(g["bsz"] * g["t"] for g in groups)

    k_scale = jnp.concatenate([jnp.ones((MLSTM_QK_W,), F32),
                               jnp.full((MLSTM_QK_W,), MLSTM_DK ** -0.5, F32),
                               jnp.ones((MLSTM_W,), F32)])
    qkv, wt_b = proj_a(n=OFF_AO, bias=b_in[:OFF_AO], scale=k_scale, out_dtype=BF16,
                       side_cast=[(W["wt_in"], l, OFF_BQ, IN_W - OFF_BQ, 256)], name="proj_qkv")
    proj_b = functools.partial(_matmul, xb, wt_b, 0, wt=True)
    whole = lambda name, slab: (W[name], l, 0, W[name].shape[1], slab)
    ao, w_out = proj_a(n=MLSTM_W, col_off=OFF_AO, bias=b_in[OFF_AO:OFF_AI], act="sigmoid",
                       side_cast=[whole("w_out", 128)], name="proj_ogate")
    b_if = jnp.pad(b_in[OFF_AI:OFF_BQ], (0, GATE_PAD - 2 * MLSTM_HEADS))
    gif, w_pa, w_pb, w_pc = proj_a(
        n=GATE_PAD, col_off=OFF_AI, bias=b_if, bn=GATE_PAD,
        side_cast=[whole("w_pa", 256), whole("w_pb", 128), whole("w_pc", 128)], name="proj_if")
    sec = lambda i: b_in[OFF_BQ + i * SB_W:OFF_BQ + (i + 1) * SB_W]
    bq, wq_x, wo_x = proj_b(n=SB_W, bias=sec(0), out_dtype=BF16,
                            side_cast=[whole("wq_x", 512), whole("wo_x", 128)], name="proj_sb_q")
    k8 = _proj_heads(xb, wt_b, l, sec(1), B_BK, group_rows, kv_prev[0], name="proj_sb_k")
    v8 = _proj_heads(xb, wt_b, l, sec(2), B_BV, group_rows, kv_prev[1], name="proj_sb_v")
    u = proj_b(n=POOL_W, col_off=B_CU, bias=sec(3), name="proj_pool_u")

    rows = xb.shape[0]
    if bufs is None:
        bufs = tuple(jnp.zeros((rows, w), BF16) for w in (MLSTM_W, SB_W, POOL_W, X_W))
    ha, hb, hc, ox = bufs
    states = []
    for gi, g in enumerate(groups):
        dims = dict(bsz=g["bsz"], t=g["t"], row_off=g["row_off"])
        ha, c_new, n_new, m_new = _mlstm(qkv, ao, gif, g["c"], g["n"], g["m"],
                                         W["mlstm_norm_g"][l], l, ha, **dims)
        hb = _stick_breaking(bq, k8[gi], v8[gi], g["sb_k"], g["sb_v"], l, hb, **dims)
        hc = _pool(u, g["pool_buf"][l], W["w_pool"], W["pool_scale"][l], l, hc,
                   pos0=g["pos0"], **dims)
        states.append((c_new, n_new, m_new))

    merged, w_up = _merge(xb, wt_b, b_in[OFF_G:].reshape(1, -1), ha, hb, hc,
                          w_pa, w_pb, w_pc, W["w_up"], l)
    y1 = _matmul(merged, w_out, 0, n=D_MODEL, res=x, name="w_out")
    ln1 = (W["ln1_g"][l], W["ln1_b"][l])
    mu1, rs1, xb = _layer_norm(y1, *ln1, emit_f32=False)

    qx = _matmul(xb, wq_x, 0, n=X_W, out_dtype=BF16, name="wq_x")
    for g in groups:
        ox = _cross_attend(qx, g["mk"], g["mv"], l, ox, bsz=g["bsz"], t=g["t"],
                           row_off=g["row_off"])
    y2 = _matmul(ox, wo_x, 0, n=D_MODEL, res=y1, res_ln=(mu1, rs1, *ln1), name="wo_x")
    ln2 = (W["ln2_g"][l], W["ln2_b"][l])
    mu2, rs2, xb = _layer_norm(y2, *ln2, emit_f32=False)

    hid, w_down = _matmul(xb, w_up, 0, n=D_FF, act="relu2", out_dtype=BF16,
                          side_cast=[whole("w_down", 128)], name="w_up")
    y = _matmul(hid, w_down, 0, n=D_MODEL, bk=D_MODEL, name="w_down")
    if last:
        x = tuple(_layer_norm(y2, W["ln3_g"][l], W["ln3_b"][l], res=y, x_ln=(mu2, rs2, *ln2),
                              in_off=g["row_off"], n_rows=g["bsz"] * g["t"], emit_bf16=False)
                  for g in groups)
        xb = None
    else:
        x, xb = _layer_norm(y2, W["ln3_g"][l], W["ln3_b"][l], res=y, x_ln=(mu2, rs2, *ln2))
    return x, xb, (k8, v8), u, states, (ha, hb, hc, ox)


def kernel(x_prompt, x_sample, cache_sb_k, cache_sb_v, state_mlstm_c, state_mlstm_n, state_mlstm_m, state_pool, cache_mem_k, cache_mem_v, mem_prompt, ln_in_g, ln_in_b, w_in, b_in, mlstm_norm_g, w_pool, pool_scale, w_pa, w_pb, w_pc, w_out, ln1_g, ln1_b, wq_x, wk_x, wv_x, wo_x, ln2_g, ln2_b, w_up, w_down, ln3_g, ln3_b):
    bf = lambda a: a.astype(BF16)
    wt_in = jnp.swapaxes(w_in, 1, 2)
    W = dict(wt_in=wt_in,
             b_in=b_in, mlstm_norm_g=mlstm_norm_g, w_pool=bf(w_pool),
             pool_scale=pool_scale, w_pa=w_pa, w_pb=w_pb, w_pc=w_pc, w_out=w_out,
             ln1_g=ln1_g, ln1_b=ln1_b, wq_x=wq_x, wo_x=wo_x, ln2_g=ln2_g, ln2_b=ln2_b,
             w_up=w_up, w_down=w_down, ln3_g=ln3_g, ln3_b=ln3_b)
    bp, tp, _ = x_prompt.shape
    bs, ts, _ = x_sample.shape
    rows_p, rows_s = bp * tp, bs * ts
    rows = rows_p + rows_s
    past = cache_sb_k.shape[2]
    mem_len = mem_prompt.shape[1]

    mem_b = mem_prompt.reshape(bp * mem_len, D_MODEL).astype(BF16)
    mem4 = lambda a, b: a.reshape(DEPTH, b, mem_len, X_W)
    mem_proj = lambda w, name: mem4(jnp.stack(
        [_matmul(mem_b, w, l, n=X_W, bn=512, name=name) for l in range(DEPTH)]), bp)
    mk_p, mv_p = mem_proj(wk_x, "mem_k"), mem_proj(wv_x, "mem_v")

    lane_m = lambda m: jnp.broadcast_to(m[..., None], m.shape + (LANES,))
    halo = lambda b: jnp.pad(b, ((0, 0), (0, 0), (POOL_HALO - POOL_BUF, 0), (0, 0)))
    heads_rows = lambda c: c.reshape(DEPTH, bs, past * SB_HEADS, SB_DH)
    groups = [
        dict(bsz=bp, t=tp, row_off=0, pos0=0,
             c=jnp.zeros((DEPTH, bp, MLSTM_HEADS, MLSTM_DK, MLSTM_DV), F32),
             n=jnp.zeros((DEPTH, bp, MLSTM_HEADS, MLSTM_DK), F32),
             m=jnp.zeros((DEPTH, bp, MLSTM_HEADS, LANES), F32),
             sb_k=None, sb_v=None,
             pool_buf=jnp.zeros((DEPTH, bp, POOL_HALO, POOL_W), F32),
             mk=mk_p, mv=mv_p),
        dict(bsz=bs, t=ts, row_off=rows_p, pos0=past,
             c=state_mlstm_c, n=state_mlstm_n, m=lane_m(state_mlstm_m),
             sb_k=heads_rows(cache_sb_k), sb_v=heads_rows(cache_sb_v),
             pool_buf=halo(state_pool),
             mk=mem4(cache_mem_k, bs), mv=mem4(cache_mem_v, bs)),
    ]

    xs = None
    for g, xin in zip(groups, (x_prompt, x_sample)):
        xs = _layer_norm(xin.reshape(g["bsz"] * g["t"], D_MODEL), ln_in_g, ln_in_b,
                         out_off=g["row_off"], out_rows=rows, prev=xs)
    x, xb = xs

    kv = tuple(tuple(jnp.zeros((DEPTH, g["bsz"] * g["t"] * SB_HEADS, SB_DH), F32) for g in groups)
               for _ in range(2))
    per_layer = []
    bufs = None
    for l in range(DEPTH):
        x, xb, kv, u, states, bufs = _trunk_layer(x, xb, W, l, groups, kv, bufs,
                                                  last=l == DEPTH - 1)
        per_layer.append((u, states))

    def group_out(gi, g):
        bsz, t, r0 = g["bsz"], g["t"], g["row_off"]
        sbk = kv[0][gi].reshape(DEPTH, bsz, t, SB_HEADS, SB_DH)
        sbv = kv[1][gi].reshape(DEPTH, bsz, t, SB_HEADS, SB_DH)
        c = jnp.stack([st[gi][0] for _, st in per_layer])
        n = jnp.stack([st[gi][1] for _, st in per_layer])
        m = jnp.stack([st[gi][2][..., 0] for _, st in per_layer])
        pool = jnp.stack([u[r0:r0 + bsz * t].reshape(bsz, t, POOL_W)[:, t - POOL_BUF:, :]
                          for u, _ in per_layer])
        return sbk, sbv, c, n, m, pool

    out_p = group_out(0, groups[0])
    out_s = group_out(1, groups[1])
    y_prompt = x[0].reshape(bp, tp, D_MODEL)
    y_sample = x[1].reshape(bs, ts, D_MODEL)
    mem5 = lambda a: a.reshape(DEPTH, bp, mem_len, X_HEADS, X_DH)
    return (y_prompt, y_sample) + out_p + (mem5(mk_p), mem5(mv_p)) + out_s
```
